```python
import jax, jax.numpy as jnp
from jax import lax
import numpy as np

D_MODEL = 1024
BATCH = 32
SEQ = 2048
DEPTH = 1
DEC_BATCH = 16
DEC_SEQ = 16
PAST_LEN = 1024

CHUNK = 64
Q_BLOCK = 128
EPS = 1e-6

FOX_HEADS = 8
FOX_HEAD_DIM = 64
FOX_WIDTH = FOX_HEADS * FOX_HEAD_DIM
FORGET_BIAS_INIT = 2.0

MLA_HEADS = 8
MLA_NOPE_DIM = 64
MLA_ROPE_DIM = 32
MLA_V_DIM = 64
MLA_Q_LORA = 256
MLA_KV_LORA = 128
MLA_WIDTH = MLA_HEADS * MLA_V_DIM
ROPE_BASE = 10000.0

MIX_WIDTH = FOX_WIDTH + MLA_WIDTH

OFF_FQ = 0
OFF_FK = OFF_FQ + FOX_WIDTH
OFF_FV = OFF_FK + FOX_WIDTH
OFF_FF = OFF_FV + FOX_WIDTH
OFF_CQ = OFF_FF + FOX_HEADS
OFF_CKV = OFF_CQ + MLA_Q_LORA
OFF_KR = OFF_CKV + MLA_KV_LORA
IN_WIDTH = OFF_KR + MLA_ROPE_DIM

PEER_HEADS = 8
PEER_N_KEYS = 128
PEER_EXPERTS = PEER_N_KEYS * PEER_N_KEYS
PEER_KEY_DIM = 256
PEER_HALF = PEER_KEY_DIM // 2
PEER_TOPK = 16
PEER_TOK_BLOCK = 128

kernel_name = "fox_mla_peer_streaming_step"

F32 = jnp.float32


def _rmsnorm(x, g):
    xf = x.astype(F32)
    y = xf * lax.rsqrt(jnp.mean(xf * xf, axis=-1, keepdims=True) + EPS)
    return (y * g.astype(F32)).astype(x.dtype)


def _rope(x, pos):
    half = x.shape[-1] // 2
    inv = ROPE_BASE ** (-jnp.arange(half, dtype=F32) / half)
    ang = pos.astype(F32)[:, None] * inv[None, :]
    shape = (1, pos.shape[0]) + (1,) * (x.ndim - 3) + (half,)
    cos = jnp.cos(ang).reshape(shape)
    sin = jnp.sin(ang).reshape(shape)
    xf = x.astype(F32)
    x1, x2 = xf[..., :half], xf[..., half:]
    return jnp.concatenate([x1 * cos - x2 * sin, x1 * sin + x2 * cos], axis=-1).astype(x.dtype)


def _attend(q, k, v, q_pos, k_pos, c_q, c_k, chunk_causal):
    scale = q.shape[-1] ** -0.5
    s = jnp.einsum('bqhd,bkhd->bhqk', q, k, preferred_element_type=F32) * scale
    if c_q is not None:
        s = s + (jnp.transpose(c_q, (0, 2, 1))[..., :, None]
                 - jnp.transpose(c_k, (0, 2, 1))[..., None, :])
    if chunk_causal:
        mask = (k_pos[None, :] // CHUNK) <= (q_pos[:, None] // CHUNK)
    else:
        mask = k_pos[None, :] <= q_pos[:, None]
    s = jnp.where(mask, s, -jnp.inf)
    p = jax.nn.softmax(s, axis=-1)
    return jnp.einsum('bhqk,bkhd->bqhd', p.astype(v.dtype), v)


def _sweep(q, k, v, pos, c, chunk_causal):
    B, S = q.shape[0], q.shape[1]
    nb = S // Q_BLOCK

    def one(i):
        start = i * Q_BLOCK
        qb = lax.dynamic_slice_in_dim(q, start, Q_BLOCK, 1)
        pb = lax.dynamic_slice_in_dim(pos, start, Q_BLOCK, 0)
        cb = None if c is None else lax.dynamic_slice_in_dim(c, start, Q_BLOCK, 1)
        return _attend(qb, k, v, pb, pos, cb, c, chunk_causal)

    out = lax.map(one, jnp.arange(nb))
    return jnp.moveaxis(out, 0, 1).reshape(B, S, out.shape[-2], out.shape[-1])


def _peer(h, w_pq, sub_keys, peer_u, peer_v):
    T, D = h.shape
    pad = (-T) % PEER_TOK_BLOCK
    blocks = jnp.pad(h, ((0, pad), (0, 0))).reshape(-1, PEER_TOK_BLOCK, D)

    def one(xb):
        q = (xb @ w_pq).reshape(PEER_TOK_BLOCK, PEER_HEADS, PEER_KEY_DIM).astype(F32)
        q = q * lax.rsqrt(jnp.mean(q * q, axis=-1, keepdims=True) + EPS)
        s1 = jnp.einsum('thd,hnd->thn', q[..., :PEER_HALF], sub_keys[0].astype(F32))
        s2 = jnp.einsum('thd,hnd->thn', q[..., PEER_HALF:], sub_keys[1].astype(F32))
        v1, i1 = lax.top_k(s1, PEER_TOPK)
        v2, i2 = lax.top_k(s2, PEER_TOPK)
        cand = (v1[..., :, None] + v2[..., None, :]).reshape(PEER_TOK_BLOCK, PEER_HEADS, PEER_TOPK * PEER_TOPK)
        sv, si = lax.top_k(cand, PEER_TOPK)
        e = (jnp.take_along_axis(i1, si // PEER_TOPK, axis=-1) * PEER_N_KEYS
             + jnp.take_along_axis(i2, si % PEER_TOPK, axis=-1))
        g = jax.nn.softmax(sv, axis=-1).reshape(PEER_TOK_BLOCK, PEER_HEADS * PEER_TOPK)
        e = e.reshape(PEER_TOK_BLOCK, PEER_HEADS * PEER_TOPK)
        a = jnp.einsum('tkd,td->tk', peer_u[e], xb, preferred_element_type=F32)
        act = (jax.nn.gelu(a, approximate=False) * g).astype(peer_v.dtype)
        return jnp.einsum('tk,tkd->td', act, peer_v[e])

    out = lax.map(one, blocks).reshape(-1, D)
    return out[:T]


def _layer(x, pos, past, g_mix, w_in, b_f, g_q, w_uq, g_kv, w_ukv, w_o, g_ffn,
           w_pq, sub_keys, peer_u, peer_v):
    B, S, D = x.shape
    h = _rmsnorm(x, g_mix)
    z = h @ w_in
    fq = z[..., OFF_FQ:OFF_FK].reshape(B, S, FOX_HEADS, FOX_HEAD_DIM)
    fk = z[..., OFF_FK:OFF_FV].reshape(B, S, FOX_HEADS, FOX_HEAD_DIM)
    fv = z[..., OFF_FV:OFF_FF].reshape(B, S, FOX_HEADS, FOX_HEAD_DIM)
    flogf = jax.nn.log_sigmoid(z[..., OFF_FF:OFF_CQ].astype(F32) + b_f.astype(F32))
    cq = _rmsnorm(z[..., OFF_CQ:OFF_CKV], g_q)
    q = (cq @ w_uq).reshape(B, S, MLA_HEADS, MLA_NOPE_DIM + MLA_ROPE_DIM)
    q = jnp.concatenate([q[..., :MLA_NOPE_DIM], _rope(q[..., MLA_NOPE_DIM:], pos)], axis=-1)
    ckv = _rmsnorm(z[..., OFF_CKV:OFF_KR], g_kv)
    krope = _rope(z[..., OFF_KR:IN_WIDTH], pos)
    new_state = (fk, fv, flogf, ckv, krope)

    if past is None:
        fk_all, fv_all, logf_all, ckv_all, kr_all = new_state
        k_pos = pos
    else:
        pk, pv, plf, pckv, pkr = past
        fk_all = jnp.concatenate([pk, fk.astype(pk.dtype)], axis=1)
        fv_all = jnp.concatenate([pv, fv.astype(pv.dtype)], axis=1)
        logf_all = jnp.concatenate([plf.astype(F32), flogf], axis=1)
        ckv_all = jnp.concatenate([pckv, ckv.astype(pckv.dtype)], axis=1)
        kr_all = jnp.concatenate([pkr, krope.astype(pkr.dtype)], axis=1)
        k_pos = jnp.arange(pk.shape[1] + S, dtype=jnp.int32)
    L = fk_all.shape[1]
    c_all = jnp.cumsum(logf_all.astype(F32), axis=1)
    kv = (ckv_all @ w_ukv).reshape(B, L, MLA_HEADS, MLA_NOPE_DIM + MLA_V_DIM)
    mk = jnp.concatenate([kv[..., :MLA_NOPE_DIM],
                          jnp.broadcast_to(kr_all[:, :, None, :], (B, L, MLA_HEADS, MLA_ROPE_DIM)).astype(kv.dtype)],
                         axis=-1)
    mv = kv[..., MLA_NOPE_DIM:]

    if past is None:
        fo = _sweep(fq, fk_all, fv_all, pos, c_all, False)
        mo = _sweep(q, mk, mv, pos, None, True)
    else:
        fo = _attend(fq, fk_all, fv_all, pos, k_pos, c_all[:, L - S:], c_all, False)
        mo = _attend(q, mk, mv, pos, k_pos, None, None, True)

    o = jnp.concatenate([fo.reshape(B, S, FOX_WIDTH), mo.reshape(B, S, MLA_WIDTH).astype(fo.dtype)], axis=-1) @ w_o
    x = x + o
    h2 = _rmsnorm(x, g_ffn)
    x = x + _peer(h2.reshape(B * S, D), w_pq, sub_keys, peer_u, peer_v).reshape(B, S, D)
    return x, new_state


def setup_inputs(seed: int = 0) -> dict:
    key = jax.random.key(seed)
    ks = jax.random.split(key, 24)
    n = jax.random.normal
    D = D_MODEL
    return {
        "x_prompt": n(ks[0], (BATCH, SEQ, D), F32),
        "x_sample": n(ks[1], (DEC_BATCH, DEC_SEQ, D), F32),
        "cache_fox_k": n(ks[2], (DEPTH, DEC_BATCH, PAST_LEN, FOX_HEADS, FOX_HEAD_DIM), F32),
        "cache_fox_v": n(ks[3], (DEPTH, DEC_BATCH, PAST_LEN, FOX_HEADS, FOX_HEAD_DIM), F32),
        "cache_fox_logf": jax.nn.log_sigmoid(FORGET_BIAS_INIT + n(ks[4], (DEPTH, DEC_BATCH, PAST_LEN, FOX_HEADS), F32)),
        "cache_mla_ckv": n(ks[5], (DEPTH, DEC_BATCH, PAST_LEN, MLA_KV_LORA), F32),
        "cache_mla_krope": n(ks[6], (DEPTH, DEC_BATCH, PAST_LEN, MLA_ROPE_DIM), F32),
        "g_mix": 1.0 + 0.02 * n(ks[7], (DEPTH, D), F32),
        "w_in": n(ks[8], (DEPTH, D, IN_WIDTH), F32) * D ** -0.5,
        "b_f": FORGET_BIAS_INIT + 0.1 * n(ks[9], (DEPTH, FOX_HEADS), F32),
        "g_q": 1.0 + 0.02 * n(ks[10], (DEPTH, MLA_Q_LORA), F32),
        "w_uq": n(ks[11], (DEPTH, MLA_Q_LORA, MLA_HEADS * (MLA_NOPE_DIM + MLA_ROPE_DIM)), F32) * MLA_Q_LORA ** -0.5,
        "g_kv": 1.0 + 0.02 * n(ks[12], (DEPTH, MLA_KV_LORA), F32),
        "w_ukv": n(ks[13], (DEPTH, MLA_KV_LORA, MLA_HEADS * (MLA_NOPE_DIM + MLA_V_DIM)), F32) * MLA_KV_LORA ** -0.5,
        "w_o": n(ks[14], (DEPTH, MIX_WIDTH, D), F32) * MIX_WIDTH ** -0.5,
        "g_ffn": 1.0 + 0.02 * n(ks[15], (DEPTH, D), F32),
        "w_pq": n(ks[16], (DEPTH, D, PEER_HEADS * PEER_KEY_DIM), F32) * D ** -0.5,
        "peer_sub_keys": n(ks[17], (DEPTH, 2, PEER_HEADS, PEER_N_KEYS, PEER_HALF), F32) * PEER_HALF ** -0.5,
        "peer_u": n(ks[18], (DEPTH, PEER_EXPERTS, D), F32) * D ** -0.5,
        "peer_v": n(ks[19], (DEPTH, PEER_EXPERTS, D), F32) * PEER_HEADS ** -0.5,
        "g_final": 1.0 + 0.02 * n(ks[20], (D,), F32),
    }


def reference(x_prompt, x_sample, cache_fox_k, cache_fox_v, cache_fox_logf, cache_mla_ckv,
              cache_mla_krope, g_mix, w_in, b_f, g_q, w_uq, g_kv, w_ukv, w_o, g_ffn, w_pq,
              peer_sub_keys, peer_u, peer_v, g_final):
    pos_p = jnp.arange(x_prompt.shape[1], dtype=jnp.int32)
    past_len = cache_fox_k.shape[2]
    pos_s = past_len + jnp.arange(x_sample.shape[1], dtype=jnp.int32)
    yp, ys = x_prompt, x_sample
    st_p, st_s = [], []
    for l in range(DEPTH):
        lw = (g_mix[l], w_in[l], b_f[l], g_q[l], w_uq[l], g_kv[l], w_ukv[l], w_o[l], g_ffn[l],
              w_pq[l], peer_sub_keys[l], peer_u[l], peer_v[l])
        yp, new_p = _layer(yp, pos_p, None, *lw)
        past = (cache_fox_k[l], cache_fox_v[l], cache_fox_logf[l], cache_mla_ckv[l], cache_mla_krope[l])
        ys, new_s = _layer(ys, pos_s, past, *lw)
        st_p.append(new_p)
        st_s.append(new_s)
    y_prompt = _rmsnorm(yp, g_final)
    y_sample = _rmsnorm(ys, g_final)
    p_fox_k = jnp.stack([s[0] for s in st_p])
    p_fox_v = jnp.stack([s[1] for s in st_p])
    p_fox_logf = jnp.stack([s[2] for s in st_p])
    p_mla_ckv = jnp.stack([s[3] for s in st_p])
    p_mla_krope = jnp.stack([s[4] for s in st_p])
    s_fox_k = jnp.stack([s[0] for s in st_s])
    s_fox_v = jnp.stack([s[1] for s in st_s])
    s_fox_logf = jnp.stack([s[2] for s in st_s])
    s_mla_ckv = jnp.stack([s[3] for s in st_s])
    s_mla_krope = jnp.stack([s[4] for s in st_s])
    return (y_prompt, y_sample, p_fox_k, p_fox_v, p_fox_logf, p_mla_ckv, p_mla_krope,
            s_fox_k, s_fox_v, s_fox_logf, s_mla_ckv, s_mla_krope)
```

```python
import functools
import math

import numpy as np
import jax
import jax.numpy as jnp
from jax import lax
from jax.experimental import pallas as pl
from jax.experimental.pallas import tpu as pltpu

F32 = jnp.float32
BF16 = jnp.bfloat16

EPS = 1e-6
CHUNK = 64
ROPE_BASE = 10000.0

FOX_HEADS = 8
FOX_HEAD_DIM = 64
FOX_WIDTH = FOX_HEADS * FOX_HEAD_DIM
MLA_HEADS = 8
MLA_NOPE_DIM = 64
MLA_ROPE_DIM = 32
MLA_V_DIM = 64
MLA_Q_LORA = 256
MLA_KV_LORA = 128
ATTN_HEADS = FOX_HEADS + MLA_HEADS
HEAD_ROW = 128
V_DIM = 64

PEER_HEADS = 8
PEER_N_KEYS = 128
PEER_HALF = 128
PEER_TOPK = 16
N_RANK = PEER_TOPK + 1
RANK_ROWS = 24

Z_FQ = 0
Z_FK = Z_FQ + FOX_WIDTH
Z_FV = Z_FK + FOX_WIDTH
Z_CQ = Z_FV + FOX_WIDTH
Z_CKV = Z_CQ + MLA_Q_LORA
Z_TAIL = Z_CKV + MLA_KV_LORA
Z_WIDTH = Z_TAIL + 128
TAIL_KR = FOX_HEADS
TAIL_KRR = TAIL_KR + MLA_ROPE_DIM

NEG = -1e30
V7X_VMEM_LIMIT_BYTES = 56 * 1024 * 1024


def _params(sem):
    return pltpu.CompilerParams(dimension_semantics=sem, vmem_limit_bytes=V7X_VMEM_LIMIT_BYTES)


def _rms(x, g):
    return x * lax.rsqrt(jnp.mean(x * x, axis=-1, keepdims=True) + EPS) * g


def _log_sigmoid(x):
    return jnp.minimum(x, 0.0) - jnp.log1p(jnp.exp(-jnp.abs(x)))


def _split3(x):
    hi = x.astype(BF16)
    r1 = x - hi.astype(F32)
    mid = r1.astype(BF16)
    lo = (r1 - mid.astype(F32)).astype(BF16)
    return hi, mid, lo


def _cumsum_rows(tri, x, carry):
    acc = carry
    for piece in _split3(x):
        acc = acc + jnp.dot(tri, piece, preferred_element_type=F32)
    return acc


def _fox_aux(c_col):
    hi, mid, lo = (p.astype(F32) for p in _split3(c_col))
    lane = lax.broadcasted_iota(jnp.int32, (1, HEAD_ROW - FOX_HEAD_DIM), 1)
    q_aux = jnp.where(lane == 0, hi, jnp.where(lane == 1, mid, jnp.where(
        lane == 2, lo, jnp.where(lane < 6, 1.0, 0.0))))
    k_aux = jnp.where(lane < 3, 1.0, jnp.where(lane == 3, -hi, jnp.where(
        lane == 4, -mid, jnp.where(lane == 5, -lo, 0.0))))
    return q_aux, k_aux


def _write_fox_kv(kh_ref, vh_ref, k, v, c_slab):
    for h in range(FOX_HEADS):
        sl = slice(h * FOX_HEAD_DIM, (h + 1) * FOX_HEAD_DIM)
        _, k_aux = _fox_aux(c_slab[:, h:h + 1])
        kh_ref[0, h, :, 0:FOX_HEAD_DIM] = k[:, sl].astype(BF16)
        kh_ref[0, h, :, FOX_HEAD_DIM:HEAD_ROW] = k_aux.astype(BF16)
        vh_ref[0, h] = v[:, sl].astype(BF16)


def _write_mla_kv(kh_ref, vh_ref, kv, krope):
    rows = kv.shape[0]
    k_rope = krope.astype(BF16)
    pad = jnp.zeros((rows, HEAD_ROW - MLA_NOPE_DIM - MLA_ROPE_DIM), BF16)
    for h in range(MLA_HEADS):
        sl = slice(h * MLA_NOPE_DIM, (h + 1) * MLA_NOPE_DIM)
        kh_ref[0, FOX_HEADS + h, :, 0:MLA_NOPE_DIM] = kv[:, sl].astype(BF16)
        kh_ref[0, FOX_HEADS + h, :, MLA_NOPE_DIM:MLA_NOPE_DIM + MLA_ROPE_DIM] = k_rope
        kh_ref[0, FOX_HEADS + h, :, MLA_NOPE_DIM + MLA_ROPE_DIM:HEAD_ROW] = pad
        vsl = slice(MLA_HEADS * MLA_NOPE_DIM + h * MLA_V_DIM, MLA_HEADS * MLA_NOPE_DIM + (h + 1) * MLA_V_DIM)
        vh_ref[0, FOX_HEADS + h] = kv[:, vsl].astype(BF16)


def _proj_body(x_ref, c0_ref, tri_ref, cosq_ref, sinq_ref, cosk_ref, sink_ref, gmix_ref, win_ref, bf_ref,
               gq_ref, wuq_ref, gkv_ref, wukv_ref,
               fk_ref, fv_ref, logf_ref, ckv_ref, kr_ref, qh_ref, kh_ref, vh_ref, carry_ref):
    @pl.when(pl.program_id(1) == 0)
    def _():
        carry_ref[...] = c0_ref[0]

    rows = x_ref.shape[1]
    h = _rms(x_ref[0], gmix_ref[...]).astype(BF16)
    z = jnp.dot(h, win_ref[...], preferred_element_type=F32)

    fk = z[:, Z_FK:Z_FV]
    fv = z[:, Z_FV:Z_CQ]
    fk_ref[0] = fk
    fv_ref[0] = fv

    tail = z[:, Z_TAIL:Z_WIDTH]
    logf = _log_sigmoid(tail + bf_ref[...])
    logf_ref[0] = logf[:, 0:FOX_HEADS]
    c_slab = _cumsum_rows(tri_ref[...], logf, carry_ref[...])
    carry_ref[...] = c_slab[rows - 1:rows, :]

    krope = (tail[:, TAIL_KR:TAIL_KR + MLA_ROPE_DIM] * cosk_ref[...]
             + tail[:, TAIL_KRR:TAIL_KRR + MLA_ROPE_DIM] * sink_ref[...])
    kr_ref[0] = krope

    cqn = _rms(z[:, Z_CQ:Z_CKV], gq_ref[...]).astype(BF16)
    q = jnp.dot(cqn, wuq_ref[...], preferred_element_type=F32)
    nope_w = MLA_HEADS * MLA_NOPE_DIM
    rope_w = MLA_HEADS * MLA_ROPE_DIM
    qrope = q[:, nope_w:nope_w + rope_w] * cosq_ref[...] + q[:, nope_w + rope_w:nope_w + 2 * rope_w] * sinq_ref[...]

    ckv = _rms(z[:, Z_CKV:Z_TAIL], gkv_ref[...])
    ckv_ref[0] = ckv
    kv = jnp.dot(ckv.astype(BF16), wukv_ref[...], preferred_element_type=F32)

    fox_scale = FOX_HEAD_DIM ** -0.5
    mla_scale = (MLA_NOPE_DIM + MLA_ROPE_DIM) ** -0.5
    qpad = jnp.zeros((rows, HEAD_ROW - MLA_NOPE_DIM - MLA_ROPE_DIM), BF16)
    for hd in range(FOX_HEADS):
        sl = slice(hd * FOX_HEAD_DIM, (hd + 1) * FOX_HEAD_DIM)
        q_aux, _ = _fox_aux(c_slab[:, hd:hd + 1])
        qh_ref[0, hd, :, 0:FOX_HEAD_DIM] = (z[:, sl] * fox_scale).astype(BF16)
        qh_ref[0, hd, :, FOX_HEAD_DIM:HEAD_ROW] = q_aux.astype(BF16)
    for hd in range(MLA_HEADS):
        sl = slice(hd * MLA_NOPE_DIM, (hd + 1) * MLA_NOPE_DIM)
        rsl = slice(hd * MLA_ROPE_DIM, (hd + 1) * MLA_ROPE_DIM)
        qh_ref[0, FOX_HEADS + hd, :, 0:MLA_NOPE_DIM] = (q[:, sl] * mla_scale).astype(BF16)
        qh_ref[0, FOX_HEADS + hd, :, MLA_NOPE_DIM:MLA_NOPE_DIM + MLA_ROPE_DIM] = (qrope[:, rsl] * mla_scale).astype(BF16)
        qh_ref[0, FOX_HEADS + hd, :, MLA_NOPE_DIM + MLA_ROPE_DIM:HEAD_ROW] = qpad
    _write_fox_kv(kh_ref, vh_ref, fk, fv, c_slab)
    _write_mla_kv(kh_ref, vh_ref, kv, krope)


def _proj(x, c0, tables, w, tm):
    B, S, D = x.shape
    cosq, sinq, cosk, sink = tables
    tri = jnp.tril(jnp.ones((tm, tm), F32)).astype(BF16)
    const = lambda shape: pl.BlockSpec(shape, lambda b, t: (0,) * len(shape))
    tab = lambda width: pl.BlockSpec((tm, width), lambda b, t: (t, 0))
    rope_w = MLA_HEADS * MLA_ROPE_DIM
    in_specs = [
        pl.BlockSpec((1, tm, D), lambda b, t: (b, t, 0)),
        pl.BlockSpec((1, 1, 128), lambda b, t: (b, 0, 0)),
        const((tm, tm)),
        tab(rope_w), tab(rope_w), tab(MLA_ROPE_DIM), tab(MLA_ROPE_DIM),
        const((1, D)), const((D, Z_WIDTH)), const((1, 128)),
        const((1, MLA_Q_LORA)), const(w["wuq"].shape), const((1, MLA_KV_LORA)), const(w["wukv"].shape),
    ]
    tok = lambda width: pl.BlockSpec((1, tm, width), lambda b, t: (b, t, 0))
    head = lambda width: pl.BlockSpec((1, ATTN_HEADS, tm, width), lambda b, t: (b, 0, t, 0))
    out_specs = [tok(FOX_WIDTH), tok(FOX_WIDTH), tok(FOX_HEADS), tok(MLA_KV_LORA), tok(MLA_ROPE_DIM),
                 head(HEAD_ROW), head(HEAD_ROW), head(V_DIM)]
    sds = jax.ShapeDtypeStruct
    out_shape = [sds((B, S, FOX_WIDTH), F32), sds((B, S, FOX_WIDTH), F32), sds((B, S, FOX_HEADS), F32),
                 sds((B, S, MLA_KV_LORA), F32), sds((B, S, MLA_ROPE_DIM), F32),
                 sds((B, ATTN_HEADS, S, HEAD_ROW), BF16), sds((B, ATTN_HEADS, S, HEAD_ROW), BF16),
                 sds((B, ATTN_HEADS, S, V_DIM), BF16)]
    return pl.pallas_call(
        _proj_body, grid=(B, S // tm), in_specs=in_specs, out_specs=out_specs, out_shape=out_shape,
        scratch_shapes=[pltpu.VMEM((1, 128), F32)],
        compiler_params=_params(("arbitrary", "arbitrary")), name="proj",
    )(x, c0, tri, cosq, sinq, cosk, sink, w["gmix"], w["win"], w["bf"], w["gq"], w["wuq"], w["gkv"], w["wukv"])


def _cache_body(ck_ref, cv_ref, clogf_ref, cckv_ref, ckr_ref, tri_ref, wukv_ref, kh_ref, vh_ref, clast_ref):
    rows = ck_ref.shape[1]
    c_slab = _cumsum_rows(tri_ref[...], clogf_ref[0], jnp.zeros((1, 128), F32))
    clast_ref[0] = c_slab[rows - 1:rows, :]
    _write_fox_kv(kh_ref, vh_ref, ck_ref[0], cv_ref[0], c_slab)
    kv = jnp.dot(cckv_ref[0].astype(BF16), wukv_ref[...], preferred_element_type=F32)
    _write_mla_kv(kh_ref, vh_ref, kv, ckr_ref[0])


def _cache_rows(ck, cv, clogf_slab, cckv, ckr, wukv):
    B, P, _ = ck.shape
    tri = jnp.tril(jnp.ones((P, P), F32)).astype(BF16)
    per_b = lambda width: pl.BlockSpec((1, P, width), lambda b: (b, 0, 0))
    const = lambda shape: pl.BlockSpec(shape, lambda b: (0,) * len(shape))
    head = lambda width: pl.BlockSpec((1, ATTN_HEADS, P, width), lambda b: (b, 0, 0, 0))
    sds = jax.ShapeDtypeStruct
    return pl.pallas_call(
        _cache_body, grid=(B,),
        in_specs=[per_b(FOX_WIDTH), per_b(FOX_WIDTH), per_b(128), per_b(MLA_KV_LORA), per_b(MLA_ROPE_DIM),
                  const((P, P)), const(wukv.shape)],
        out_specs=[head(HEAD_ROW), head(V_DIM), pl.BlockSpec((1, 1, 128), lambda b: (b, 0, 0))],
        out_shape=[sds((B, ATTN_HEADS, P, HEAD_ROW), BF16), sds((B, ATTN_HEADS, P, V_DIM), BF16),
                   sds((B, 1, 128), F32)],
        compiler_params=_params(("arbitrary",)), name="cache_rows",
    )(ck, cv, clogf_slab, cckv, ckr, tri, wukv)


def _attn_body(q_ref, k_ref, v_ref, o_ref, *, tq, tk, nkv_total, q_off, kv_len):
    pair = pl.program_id(1)
    qi = pl.program_id(2)
    q_lo = q_off + qi * tq
    row = q_lo + lax.broadcasted_iota(jnp.int32, (tq, 1), 0)
    lim = jnp.where(pair >= FOX_HEADS // 2, row | (CHUNK - 1), row)
    lim = jnp.minimum(lim, kv_len - 1)
    n_full = jnp.minimum(lax.div(q_lo + 1, tk), nkv_total)
    n_all = jnp.minimum(lax.div((q_lo + tq - 1) | (CHUNK - 1), tk) + 1, nkv_total)

    outs = []
    for hh in range(2):
        q = q_ref[0, hh]

        def step(j, carry, masked, hh=hh, q=q):
            m, l, acc = carry
            off = pl.multiple_of(j * tk, tk)
            kb = k_ref[0, hh, pl.ds(off, tk), :]
            vb = v_ref[0, hh, pl.ds(off, tk), :]
            s = lax.dot_general(q, kb, (((1,), (1,)), ((), ())), preferred_element_type=F32)
            if masked:
                col = off + lax.broadcasted_iota(jnp.int32, (1, tk), 1)
                s = jnp.where(col <= lim, s, NEG)
            m_new = jnp.maximum(m, jnp.max(s, axis=-1, keepdims=True))
            alpha = jnp.exp(m - m_new)
            p = jnp.exp(s - m_new)
            l = alpha * l + jnp.sum(p, axis=-1, keepdims=True)
            acc = alpha * acc + jnp.dot(p.astype(BF16), vb, preferred_element_type=F32)
            return m_new, l, acc

        carry = (jnp.full((tq, 1), NEG, F32), jnp.zeros((tq, 1), F32), jnp.zeros((tq, V_DIM), F32))
        carry = lax.fori_loop(0, n_full, functools.partial(step, masked=False), carry)
        carry = lax.fori_loop(n_full, n_all, functools.partial(step, masked=True), carry)
        _, l, acc = carry
        outs.append(acc / l)
    o_ref[0] = jnp.concatenate(outs, axis=-1).astype(BF16)


def _attention(qh, kh, vh, tq, tk, q_off, kv_len):
    B, _, Tq, _ = qh.shape
    Tk = kh.shape[2]
    body = functools.partial(_attn_body, tq=tq, tk=tk, nkv_total=Tk // tk, q_off=q_off, kv_len=kv_len)
    return pl.pallas_call(
        body, grid=(B, ATTN_HEADS // 2, Tq // tq),
        in_specs=[pl.BlockSpec((1, 2, tq, HEAD_ROW), lambda b, p, i: (b, p, i, 0)),
                  pl.BlockSpec((1, 2, Tk, HEAD_ROW), lambda b, p, i: (b, p, 0, 0)),
                  pl.BlockSpec((1, 2, Tk, V_DIM), lambda b, p, i: (b, p, 0, 0))],
        out_specs=pl.BlockSpec((1, tq, 2 * V_DIM), lambda b, p, i: (b, i, p)),
        out_shape=jax.ShapeDtypeStruct((B, Tq, ATTN_HEADS * V_DIM), BF16),
        compiler_params=_params(("arbitrary", "arbitrary", "arbitrary")), name="attn",
    )(qh, kh, vh)


def _top_ranks(work, rows_ref):
    rows_ref[...] = jnp.full(rows_ref.shape, -jnp.inf, F32)
    for r in range(N_RANK):
        m = jnp.max(work, axis=0, keepdims=True)
        rows_ref[r:r + 1, :] = m
        if r + 1 < N_RANK:
            work = jnp.where(work == m, -jnp.inf, work)


def _route_body(x_ref, o_ref, wo_ref, gffn_ref, wpq_ref, k1_ref, k2_ref,
                x1_ref, h2_ref, s2_ref, e2_ref, thr_ref, w1_ref, qp_ref, a_ref, b_ref):
    x1 = x_ref[...] + jnp.dot(o_ref[...], wo_ref[...], preferred_element_type=F32)
    x1_ref[...] = x1
    h2 = _rms(x1, gffn_ref[...]).astype(BF16)
    h2_ref[...] = h2
    qp_ref[...] = jnp.dot(h2, wpq_ref[...], preferred_element_type=F32)
    key_dim = 2 * PEER_HALF

    def head(h, _):
        qh = qp_ref[:, pl.ds(pl.multiple_of(h * key_dim, key_dim), key_dim)]
        qn = qh * lax.rsqrt(jnp.mean(qh * qh, axis=-1, keepdims=True) + EPS)
        nt = (((1,), (1,)), ((), ()))
        s1 = lax.dot_general(k1_ref[h], qn[:, :PEER_HALF].astype(BF16), nt, preferred_element_type=F32)
        s2 = lax.dot_general(k2_ref[h], qn[:, PEER_HALF:].astype(BF16), nt, preferred_element_type=F32)
        _top_ranks(s1, a_ref)
        _top_ranks(s2, b_ref)
        blocks = [a_ref[0:1, :] + b_ref[...]]
        for r in range(1, 8):
            blocks.append(a_ref[r:r + 1, :] + b_ref[0:8, :])
        blocks.append(a_ref[8:RANK_ROWS, :] + b_ref[0:1, :])
        cand = jnp.concatenate(blocks, axis=0)
        top = a_ref[0:1, :] + b_ref[0:1, :]
        zsum = jnp.zeros_like(top)
        v = top
        for k in range(PEER_TOPK):
            v = jnp.max(cand, axis=0, keepdims=True)
            zsum = zsum + jnp.exp(v - top)
            cand = jnp.where(cand == v, -jnp.inf, cand)
        tau = 0.5 * (v + jnp.max(cand, axis=0, keepdims=True))
        s2_ref[0, h] = s2
        e2_ref[0, h] = jnp.exp(s2 - b_ref[0:1, :])
        thr_ref[0, h] = tau - s1
        w1_ref[0, h] = jnp.exp(s1 - a_ref[0:1, :]) / zsum
        return 0

    lax.fori_loop(0, PEER_HEADS, head, 0)


def _route(x2d, o2d, w, tm, tt):
    T, D = x2d.shape
    r = tt // tm
    kq = w["wpq"].shape[1]
    const = lambda shape: pl.BlockSpec(shape, lambda i: (0,) * len(shape))
    lane_out = pl.BlockSpec((1, PEER_HEADS, PEER_N_KEYS, tm), lambda i: (i // r, 0, 0, i % r))
    sds = jax.ShapeDtypeStruct
    lane_shape = sds((T // tt, PEER_HEADS, PEER_N_KEYS, tt), F32)
    return pl.pallas_call(
        _route_body, grid=(T // tm,),
        in_specs=[pl.BlockSpec((tm, D), lambda i: (i, 0)), pl.BlockSpec((tm, D), lambda i: (i, 0)),
                  const((D, D)), const((1, D)), const((D, kq)),
                  const((PEER_HEADS, PEER_N_KEYS, PEER_HALF)), const((PEER_HEADS, PEER_N_KEYS, PEER_HALF))],
        out_specs=[pl.BlockSpec((tm, D), lambda i: (i, 0)), pl.BlockSpec((tm, D), lambda i: (i, 0)),
                   lane_out, lane_out, lane_out, lane_out],
        out_shape=[sds((T, D), F32), sds((T, D), BF16), lane_shape, lane_shape, lane_shape, lane_shape],
        scratch_shapes=[pltpu.VMEM((tm, kq), F32), pltpu.VMEM((RANK_ROWS, tm), F32), pltpu.VMEM((RANK_ROWS, tm), F32)],
        compiler_params=_params(("arbitrary",)), name="route",
    )(x2d, o2d, w["wo"], w["gffn"], w["wpq"], w["k1"], w["k2"])


def _peer_body(x1_ref, h2_ref, s2_ref, e2_ref, thr_ref, w1_ref, u_ref, v_ref, gfin_ref, y_ref,
               acc_ref, a_ref, act_ref, *, tt, rows_per_step):
    j = pl.program_id(1)

    @pl.when(j == 0)
    def _():
        acc_ref[...] = jnp.zeros_like(acc_ref)

    a_ref[...] = lax.dot_general(u_ref[...], h2_ref[...], (((1,), (1,)), ((), ())), preferred_element_type=F32)
    inv_sqrt2 = 1.0 / math.sqrt(2.0)
    for ii in range(rows_per_step):
        rs = slice(ii * PEER_N_KEYS, (ii + 1) * PEER_N_KEYS)
        for tc in range(tt // 128):
            ls = slice(tc * 128, (tc + 1) * 128)
            gate = jnp.zeros((PEER_N_KEYS, 128), F32)
            for h in range(PEER_HEADS):
                sel = s2_ref[0, h, :, ls] >= thr_ref[0, h, ii:ii + 1, ls]
                gate = gate + jnp.where(sel, e2_ref[0, h, :, ls] * w1_ref[0, h, ii:ii + 1, ls], 0.0)
            a = a_ref[rs, ls]
            act = 0.5 * a * (1.0 + lax.erf(a * inv_sqrt2)) * gate
            act_ref[rs, ls] = act.astype(BF16)
    acc_ref[...] += lax.dot_general(act_ref[...], v_ref[...], (((0,), (0,)), ((), ())), preferred_element_type=F32)

    @pl.when(j == pl.num_programs(1) - 1)
    def _():
        y_ref[...] = _rms(x1_ref[...] + acc_ref[...], gfin_ref[...])


def _peer(x1, h2, s2, e2, thr, w1, w, tt, rows_per_step):
    T, D = x1.shape
    n_exp = w["u"].shape[0]
    eb = rows_per_step * PEER_N_KEYS
    lane_in = pl.BlockSpec((1, PEER_HEADS, PEER_N_KEYS, tt), lambda t, j: (t, 0, 0, 0))
    row_in = pl.BlockSpec((1, PEER_HEADS, rows_per_step, tt), lambda t, j: (t, 0, j, 0))
    body = functools.partial(_peer_body, tt=tt, rows_per_step=rows_per_step)
    return pl.pallas_call(
        body, grid=(T // tt, n_exp // eb),
        in_specs=[pl.BlockSpec((tt, D), lambda t, j: (t, 0)), pl.BlockSpec((tt, D), lambda t, j: (t, 0)),
                  lane_in, lane_in, row_in, row_in,
                  pl.BlockSpec((eb, D), lambda t, j: (j, 0)), pl.BlockSpec((eb, D), lambda t, j: (j, 0)),
                  pl.BlockSpec((1, D), lambda t, j: (0, 0))],
        out_specs=pl.BlockSpec((tt, D), lambda t, j: (t, 0)),
        out_shape=jax.ShapeDtypeStruct((T, D), F32),
        scratch_shapes=[pltpu.VMEM((tt, D), F32), pltpu.VMEM((eb, tt), F32), pltpu.VMEM((eb, tt), BF16)],
        compiler_params=_params(("arbitrary", "arbitrary")), name="peer",
    )(x1, h2, s2, e2, thr, w1, w["u"], w["v"], w["gfin"])


def _rope_tables(pos):
    half = MLA_ROPE_DIM // 2
    inv = ROPE_BASE ** (-jnp.arange(half, dtype=F32) / half)
    ang = pos.astype(F32)[:, None] * inv[None, :]
    cos = jnp.concatenate([jnp.cos(ang)] * 2, axis=-1)
    sin = jnp.concatenate([jnp.sin(ang)] * 2, axis=-1)
    return jnp.tile(cos, (1, MLA_HEADS)), jnp.tile(sin, (1, MLA_HEADS)), cos, sin


def _rot_cols(wcols):
    half = wcols.shape[-1] // 2
    return jnp.concatenate([-wcols[..., half:], wcols[..., :half]], axis=-1)


def _prep_weights(g_mix, w_in, b_f, g_q, w_uq, g_kv, w_ukv, w_o, g_ffn, w_pq, sub_keys, peer_u, peer_v, g_final):
    D = w_in.shape[0]
    off_ff = 3 * FOX_WIDTH
    off_cq = off_ff + FOX_HEADS
    off_ckv = off_cq + MLA_Q_LORA
    off_kr = off_ckv + MLA_KV_LORA
    kr_cols = w_in[:, off_kr:off_kr + MLA_ROPE_DIM]
    tail_pad = jnp.zeros((D, 128 - FOX_HEADS - 2 * MLA_ROPE_DIM), F32)
    win = jnp.concatenate([w_in[:, :off_ff], w_in[:, off_cq:off_kr], w_in[:, off_ff:off_cq],
                           kr_cols, _rot_cols(kr_cols), tail_pad], axis=1).astype(BF16)
    bf = jnp.pad(b_f, (0, 128 - FOX_HEADS)).reshape(1, 128)
    per_q = MLA_NOPE_DIM + MLA_ROPE_DIM
    wq = w_uq.reshape(MLA_Q_LORA, MLA_HEADS, per_q)
    q_rope = wq[:, :, MLA_NOPE_DIM:]
    wuq = jnp.concatenate([wq[:, :, :MLA_NOPE_DIM].reshape(MLA_Q_LORA, -1), q_rope.reshape(MLA_Q_LORA, -1),
                           _rot_cols(q_rope).reshape(MLA_Q_LORA, -1)], axis=1).astype(BF16)
    wkv = w_ukv.reshape(MLA_KV_LORA, MLA_HEADS, MLA_NOPE_DIM + MLA_V_DIM)
    wukv = jnp.concatenate([wkv[:, :, :MLA_NOPE_DIM].reshape(MLA_KV_LORA, -1),
                            wkv[:, :, MLA_NOPE_DIM:].reshape(MLA_KV_LORA, -1)], axis=1).astype(BF16)
    return dict(
        gmix=g_mix.reshape(1, -1), win=win, bf=bf, gq=g_q.reshape(1, -1), wuq=wuq, gkv=g_kv.reshape(1, -1),
        wukv=wukv, wo=w_o.astype(BF16), gffn=g_ffn.reshape(1, -1), wpq=w_pq.astype(BF16),
        k1=sub_keys[0].astype(BF16), k2=sub_keys[1].astype(BF16),
        u=peer_u.astype(BF16), v=peer_v.astype(BF16), gfin=g_final.reshape(1, -1))


def _ffn(x, o, w):
    B, S, D = x.shape
    T = B * S
    tm = min(256, T)
    tt = min(512, T)
    x1, h2, s2, e2, thr, w1 = _route(x.reshape(T, D), o.reshape(T, D), w, tm, tt)
    return _peer(x1, h2, s2, e2, thr, w1, w, tt, 8).reshape(B, S, D)


def kernel(x_prompt, x_sample, cache_fox_k, cache_fox_v, cache_fox_logf, cache_mla_ckv, cache_mla_krope, g_mix, w_in, b_f, g_q, w_uq, g_kv, w_ukv, w_o, g_ffn, w_pq, peer_sub_keys, peer_u, peer_v, g_final):
    assert g_mix.shape[0] == 1, "single-layer model"
    w = _prep_weights(g_mix[0], w_in[0], b_f[0], g_q[0], w_uq[0], g_kv[0], w_ukv[0], w_o[0], g_ffn[0], w_pq[0],
                      peer_sub_keys[0], peer_u[0], peer_v[0], g_final)
    B, S, D = x_prompt.shape
    Bs, Ss, _ = x_sample.shape
    P = cache_fox_k.shape[2]

    tables_p = _rope_tables(jnp.arange(S, dtype=jnp.int32))
    tm = min(512, S)
    fk, fv, logf, ckv, kr, qh, kh, vh = _proj(x_prompt, jnp.zeros((B, 1, 128), F32), tables_p, w, tm)
    tq = min(256, S)
    o_p = _attention(qh, kh, vh, tq, tq, 0, S)
    y_p = _ffn(x_prompt, o_p, w)

    kh_c, vh_c, c_last = _cache_rows(
        cache_fox_k[0].reshape(Bs, P, FOX_WIDTH), cache_fox_v[0].reshape(Bs, P, FOX_WIDTH),
        jnp.pad(cache_fox_logf[0], ((0, 0), (0, 0), (0, 128 - FOX_HEADS))),
        cache_mla_ckv[0], cache_mla_krope[0], w["wukv"])
    tables_s = _rope_tables(P + jnp.arange(Ss, dtype=jnp.int32))
    sfk, sfv, slogf, sckv, skr, sqh, skh, svh = _proj(x_sample, c_last, tables_s, w, Ss)
    tk = 256
    L = P + Ss
    pad = (-L) % tk
    kh_all = jnp.pad(jnp.concatenate([kh_c, skh], axis=2), ((0, 0), (0, 0), (0, pad), (0, 0)))
    vh_all = jnp.pad(jnp.concatenate([vh_c, svh], axis=2), ((0, 0), (0, 0), (0, pad), (0, 0)))
    o_s = _attention(sqh, kh_all, vh_all, Ss, tk, P, L)
    y_s = _ffn(x_sample, o_s, w)

    return (y_p, y_s,
            fk.reshape(1, B, S, FOX_HEADS, FOX_HEAD_DIM), fv.reshape(1, B, S, FOX_HEADS, FOX_HEAD_DIM),
            logf[None], ckv[None], kr[None],
            sfk.reshape(1, Bs, Ss, FOX_HEADS, FOX_HEAD_DIM), sfv.reshape(1, Bs, Ss, FOX_HEADS, FOX_HEAD_DIM),
            slogf[None], sckv[None], skr[None])
```

```python
import functools
import math

import numpy as np
import jax
import jax.numpy as jnp
from jax import lax
from jax.experimental import pallas as pl
from jax.experimental.pallas import tpu as pltpu

F32 = jnp.float32
BF16 = jnp.bfloat16

EPS = 1e-6
CHUNK = 64
ROPE_BASE = 10000.0

FOX_HEADS = 8
FOX_HEAD_DIM = 64
FOX_WIDTH = FOX_HEADS * FOX_HEAD_DIM
MLA_HEADS = 8
MLA_NOPE_DIM = 64
MLA_ROPE_DIM = 32
MLA_V_DIM = 64
MLA_Q_LORA = 256
MLA_KV_LORA = 128
ATTN_HEADS = FOX_HEADS + MLA_HEADS
HEAD_ROW = 128
V_DIM = 64

PEER_HEADS = 8
PEER_N_KEYS = 128
PEER_HALF = 128
PEER_TOPK = 16
N_RANK = PEER_TOPK + 1
RANK_ROWS = 24

Z_FQ = 0
Z_FK = Z_FQ + FOX_WIDTH
Z_FV = Z_FK + FOX_WIDTH
Z_CQ = Z_FV + FOX_WIDTH
Z_CKV = Z_CQ + MLA_Q_LORA
Z_TAIL = Z_CKV + MLA_KV_LORA
Z_WIDTH = Z_TAIL + 128
TAIL_KR = FOX_HEADS
TAIL_KRR = TAIL_KR + MLA_ROPE_DIM

NEG = -1e30
ATTN_HEAD_GROUP = 4
V7X_VMEM_LIMIT_BYTES = 56 * 1024 * 1024


def _params(sem):
    return pltpu.CompilerParams(dimension_semantics=sem, vmem_limit_bytes=V7X_VMEM_LIMIT_BYTES)


def _rms(x, g):
    return x * lax.rsqrt(jnp.mean(x * x, axis=-1, keepdims=True) + EPS) * g


def _log_sigmoid(x):
    return jnp.minimum(x, 0.0) - jnp.log1p(jnp.exp(-jnp.abs(x)))


def _split3(x):
    hi = x.astype(BF16)
    r1 = x - hi.astype(F32)
    mid = r1.astype(BF16)
    lo = (r1 - mid.astype(F32)).astype(BF16)
    return hi, mid, lo


def _cumsum_rows(tri, x, carry):
    acc = carry
    for piece in _split3(x):
        acc = acc + jnp.dot(tri, piece, preferred_element_type=F32)
    return acc


def _fox_aux(c_col):
    hi, mid, lo = (p.astype(F32) for p in _split3(c_col))
    lane = lax.broadcasted_iota(jnp.int32, (1, HEAD_ROW - FOX_HEAD_DIM), 1)
    q_aux = jnp.where(lane == 0, hi, jnp.where(lane == 1, mid, jnp.where(
        lane == 2, lo, jnp.where(lane < 6, 1.0, 0.0))))
    k_aux = jnp.where(lane < 3, 1.0, jnp.where(lane == 3, -hi, jnp.where(
        lane == 4, -mid, jnp.where(lane == 5, -lo, 0.0))))
    return q_aux, k_aux


def _write_fox_kv(kh_ref, vh_ref, k, v, c_slab):
    for h in range(FOX_HEADS):
        sl = slice(h * FOX_HEAD_DIM, (h + 1) * FOX_HEAD_DIM)
        _, k_aux = _fox_aux(c_slab[:, h:h + 1])
        kh_ref[0, h, :, 0:FOX_HEAD_DIM] = k[:, sl].astype(BF16)
        kh_ref[0, h, :, FOX_HEAD_DIM:HEAD_ROW] = k_aux.astype(BF16)
        vh_ref[0, h] = v[:, sl].astype(BF16)


def _write_mla_kv(kh_ref, vh_ref, kv, krope):
    rows = kv.shape[0]
    k_rope = krope.astype(BF16)
    pad = jnp.zeros((rows, HEAD_ROW - MLA_NOPE_DIM - MLA_ROPE_DIM), BF16)
    for h in range(MLA_HEADS):
        sl = slice(h * MLA_NOPE_DIM, (h + 1) * MLA_NOPE_DIM)
        kh_ref[0, FOX_HEADS + h, :, 0:MLA_NOPE_DIM] = kv[:, sl].astype(BF16)
        kh_ref[0, FOX_HEADS + h, :, MLA_NOPE_DIM:MLA_NOPE_DIM + MLA_ROPE_DIM] = k_rope
        kh_ref[0, FOX_HEADS + h, :, MLA_NOPE_DIM + MLA_ROPE_DIM:HEAD_ROW] = pad
        vsl = slice(MLA_HEADS * MLA_NOPE_DIM + h * MLA_V_DIM, MLA_HEADS * MLA_NOPE_DIM + (h + 1) * MLA_V_DIM)
        vh_ref[0, FOX_HEADS + h] = kv[:, vsl].astype(BF16)


def _proj_body(x_ref, c0_ref, tri_ref, cosq_ref, sinq_ref, cosk_ref, sink_ref, gmix_ref, win_ref, bf_ref,
               gq_ref, wuq_ref, gkv_ref, wukv_ref,
               fk_ref, fv_ref, logf_ref, ckv_ref, kr_ref, qh_ref, kh_ref, vh_ref, carry_ref):
    @pl.when(pl.program_id(1) == 0)
    def _():
        carry_ref[...] = c0_ref[0]

    rows = x_ref.shape[1]
    h = _rms(x_ref[0], gmix_ref[...]).astype(BF16)
    z = jnp.dot(h, win_ref[...], preferred_element_type=F32)

    fk = z[:, Z_FK:Z_FV]
    fv = z[:, Z_FV:Z_CQ]
    fk_ref[0] = fk
    fv_ref[0] = fv

    tail = z[:, Z_TAIL:Z_WIDTH]
    logf = _log_sigmoid(tail + bf_ref[...])
    logf_ref[0] = logf[:, 0:FOX_HEADS]
    c_slab = _cumsum_rows(tri_ref[...], logf, carry_ref[...])
    carry_ref[...] = c_slab[rows - 1:rows, :]

    krope = (tail[:, TAIL_KR:TAIL_KR + MLA_ROPE_DIM] * cosk_ref[...]
             + tail[:, TAIL_KRR:TAIL_KRR + MLA_ROPE_DIM] * sink_ref[...])
    kr_ref[0] = krope

    cqn = _rms(z[:, Z_CQ:Z_CKV], gq_ref[...]).astype(BF16)
    q = jnp.dot(cqn, wuq_ref[...], preferred_element_type=F32)
    nope_w = MLA_HEADS * MLA_NOPE_DIM
    rope_w = MLA_HEADS * MLA_ROPE_DIM
    qrope = q[:, nope_w:nope_w + rope_w] * cosq_ref[...] + q[:, nope_w + rope_w:nope_w + 2 * rope_w] * sinq_ref[...]

    ckv = _rms(z[:, Z_CKV:Z_TAIL], gkv_ref[...])
    ckv_ref[0] = ckv
    kv = jnp.dot(ckv.astype(BF16), wukv_ref[...], preferred_element_type=F32)

    fox_scale = FOX_HEAD_DIM ** -0.5
    mla_scale = (MLA_NOPE_DIM + MLA_ROPE_DIM) ** -0.5
    qpad = jnp.zeros((rows, HEAD_ROW - MLA_NOPE_DIM - MLA_ROPE_DIM), BF16)
    for hd in range(FOX_HEADS):
        sl = slice(hd * FOX_HEAD_DIM, (hd + 1) * FOX_HEAD_DIM)
        q_aux, _ = _fox_aux(c_slab[:, hd:hd + 1])
        qh_ref[0, hd, :, 0:FOX_HEAD_DIM] = (z[:, sl] * fox_scale).astype(BF16)
        qh_ref[0, hd, :, FOX_HEAD_DIM:HEAD_ROW] = q_aux.astype(BF16)
    for hd in range(MLA_HEADS):
        sl = slice(hd * MLA_NOPE_DIM, (hd + 1) * MLA_NOPE_DIM)
        rsl = slice(hd * MLA_ROPE_DIM, (hd + 1) * MLA_ROPE_DIM)
        qh_ref[0, FOX_HEADS + hd, :, 0:MLA_NOPE_DIM] = (q[:, sl] * mla_scale).astype(BF16)
        qh_ref[0, FOX_HEADS + hd, :, MLA_NOPE_DIM:MLA_NOPE_DIM + MLA_ROPE_DIM] = (qrope[:, rsl] * mla_scale).astype(BF16)
        qh_ref[0, FOX_HEADS + hd, :, MLA_NOPE_DIM + MLA_ROPE_DIM:HEAD_ROW] = qpad
    _write_fox_kv(kh_ref, vh_ref, fk, fv, c_slab)
    _write_mla_kv(kh_ref, vh_ref, kv, krope)


def _proj(x, c0, tables, w, tm):
    B, S, D = x.shape
    cosq, sinq, cosk, sink = tables
    tri = jnp.tril(jnp.ones((tm, tm), F32)).astype(BF16)
    const = lambda shape: pl.BlockSpec(shape, lambda b, t: (0,) * len(shape))
    tab = lambda width: pl.BlockSpec((tm, width), lambda b, t: (t, 0))
    rope_w = MLA_HEADS * MLA_ROPE_DIM
    in_specs = [
        pl.BlockSpec((1, tm, D), lambda b, t: (b, t, 0)),
        pl.BlockSpec((1, 1, 128), lambda b, t: (b, 0, 0)),
        const((tm, tm)),
        tab(rope_w), tab(rope_w), tab(MLA_ROPE_DIM), tab(MLA_ROPE_DIM),
        const((1, D)), const((D, Z_WIDTH)), const((1, 128)),
        const((1, MLA_Q_LORA)), const(w["wuq"].shape), const((1, MLA_KV_LORA)), const(w["wukv"].shape),
    ]
    tok = lambda width: pl.BlockSpec((1, tm, width), lambda b, t: (b, t, 0))
    head = lambda width: pl.BlockSpec((1, ATTN_HEADS, tm, width), lambda b, t: (b, 0, t, 0))
    out_specs = [tok(FOX_WIDTH), tok(FOX_WIDTH), tok(FOX_HEADS), tok(MLA_KV_LORA), tok(MLA_ROPE_DIM),
                 head(HEAD_ROW), head(HEAD_ROW), head(V_DIM)]
    sds = jax.ShapeDtypeStruct
    out_shape = [sds((B, S, FOX_WIDTH), F32), sds((B, S, FOX_WIDTH), F32), sds((B, S, FOX_HEADS), F32),
                 sds((B, S, MLA_KV_LORA), F32), sds((B, S, MLA_ROPE_DIM), F32),
                 sds((B, ATTN_HEADS, S, HEAD_ROW), BF16), sds((B, ATTN_HEADS, S, HEAD_ROW), BF16),
                 sds((B, ATTN_HEADS, S, V_DIM), BF16)]
    return pl.pallas_call(
        _proj_body, grid=(B, S // tm), in_specs=in_specs, out_specs=out_specs, out_shape=out_shape,
        scratch_shapes=[pltpu.VMEM((1, 128), F32)],
        compiler_params=_params(("arbitrary", "arbitrary")), name="proj",
    )(x, c0, tri, cosq, sinq, cosk, sink, w["gmix"], w["win"], w["bf"], w["gq"], w["wuq"], w["gkv"], w["wukv"])


def _cache_body(ck_ref, cv_ref, clogf_ref, cckv_ref, ckr_ref, tri_ref, wukv_ref, kh_ref, vh_ref, clast_ref):
    rows = ck_ref.shape[1]
    c_slab = _cumsum_rows(tri_ref[...], clogf_ref[0], jnp.zeros((1, 128), F32))
    clast_ref[0] = c_slab[rows - 1:rows, :]
    _write_fox_kv(kh_ref, vh_ref, ck_ref[0], cv_ref[0], c_slab)
    kv = jnp.dot(cckv_ref[0].astype(BF16), wukv_ref[...], preferred_element_type=F32)
    _write_mla_kv(kh_ref, vh_ref, kv, ckr_ref[0])


def _cache_rows(ck, cv, clogf_slab, cckv, ckr, wukv):
    B, P, _ = ck.shape
    tri = jnp.tril(jnp.ones((P, P), F32)).astype(BF16)
    per_b = lambda width: pl.BlockSpec((1, P, width), lambda b: (b, 0, 0))
    const = lambda shape: pl.BlockSpec(shape, lambda b: (0,) * len(shape))
    head = lambda width: pl.BlockSpec((1, ATTN_HEADS, P, width), lambda b: (b, 0, 0, 0))
    sds = jax.ShapeDtypeStruct
    return pl.pallas_call(
        _cache_body, grid=(B,),
        in_specs=[per_b(FOX_WIDTH), per_b(FOX_WIDTH), per_b(128), per_b(MLA_KV_LORA), per_b(MLA_ROPE_DIM),
                  const((P, P)), const(wukv.shape)],
        out_specs=[head(HEAD_ROW), head(V_DIM), pl.BlockSpec((1, 1, 128), lambda b: (b, 0, 0))],
        out_shape=[sds((B, ATTN_HEADS, P, HEAD_ROW), BF16), sds((B, ATTN_HEADS, P, V_DIM), BF16),
                   sds((B, 1, 128), F32)],
        compiler_params=_params(("arbitrary",)), name="cache_rows",
    )(ck, cv, clogf_slab, cckv, ckr, tri, wukv)


def _attn_body(q_ref, k_ref, v_ref, o_ref, *, hg, tq, tk, nkv_total, q_off, kv_len):
    grp = pl.program_id(1)
    qi = pl.program_id(2)
    q_lo = q_off + qi * tq
    row = q_lo + lax.broadcasted_iota(jnp.int32, (tq, 1), 0)
    lim = jnp.where(grp >= FOX_HEADS // hg, row | (CHUNK - 1), row)
    lim = jnp.minimum(lim, kv_len - 1)
    n_full = jnp.minimum(lax.div(q_lo + 1, tk), nkv_total)
    n_all = jnp.minimum(lax.div((q_lo + tq - 1) | (CHUNK - 1), tk) + 1, nkv_total)

    def step(j, carry, masked):
        off = pl.multiple_of(j * tk, tk)
        if masked:
            visible = off + lax.broadcasted_iota(jnp.int32, (1, tk), 1) <= lim
        out = []
        for hh in range(hg):
            m, l, acc = carry[3 * hh:3 * hh + 3]
            kb = k_ref[0, hh, pl.ds(off, tk), :]
            vb = v_ref[0, hh, pl.ds(off, tk), :]
            s = lax.dot_general(q_ref[0, hh], kb, (((1,), (1,)), ((), ())), preferred_element_type=F32)
            if masked:
                s = jnp.where(visible, s, NEG)
            m_new = jnp.maximum(m, jnp.max(s, axis=-1, keepdims=True))
            alpha = jnp.exp(m - m_new)
            p = jnp.exp(s - m_new)
            l = alpha * l + jnp.sum(p, axis=-1, keepdims=True)
            acc = alpha * acc + jnp.dot(p.astype(BF16), vb, preferred_element_type=F32)
            out += [m_new, l, acc]
        return tuple(out)

    init = (jnp.full((tq, 1), NEG, F32), jnp.zeros((tq, 1), F32), jnp.zeros((tq, V_DIM), F32)) * hg
    carry = lax.fori_loop(0, n_full, functools.partial(step, masked=False), init)
    carry = lax.fori_loop(n_full, n_all, functools.partial(step, masked=True), carry)
    o_ref[0] = jnp.concatenate([carry[3 * hh + 2] / carry[3 * hh + 1] for hh in range(hg)], axis=-1).astype(BF16)


def _attention(qh, kh, vh, hg, tq, tk, q_off, kv_len):
    B, _, Tq, _ = qh.shape
    Tk = kh.shape[2]
    body = functools.partial(_attn_body, hg=hg, tq=tq, tk=tk, nkv_total=Tk // tk, q_off=q_off, kv_len=kv_len)
    return pl.pallas_call(
        body, grid=(B, ATTN_HEADS // hg, Tq // tq),
        in_specs=[pl.BlockSpec((1, hg, tq, HEAD_ROW), lambda b, g, i: (b, g, i, 0)),
                  pl.BlockSpec((1, hg, Tk, HEAD_ROW), lambda b, g, i: (b, g, 0, 0)),
                  pl.BlockSpec((1, hg, Tk, V_DIM), lambda b, g, i: (b, g, 0, 0))],
        out_specs=pl.BlockSpec((1, tq, hg * V_DIM), lambda b, g, i: (b, i, g)),
        out_shape=jax.ShapeDtypeStruct((B, Tq, ATTN_HEADS * V_DIM), BF16),
        compiler_params=_params(("arbitrary", "arbitrary", "arbitrary")), name="attn",
    )(qh, kh, vh)


def _top_ranks(work, rows_ref):
    rows_ref[...] = jnp.full(rows_ref.shape, -jnp.inf, F32)
    for r in range(N_RANK):
        m = jnp.max(work, axis=0, keepdims=True)
        rows_ref[r:r + 1, :] = m
        if r + 1 < N_RANK:
            work = jnp.where(work == m, -jnp.inf, work)


def _route_body(x_ref, o_ref, wo_ref, gffn_ref, wpq_ref, k1_ref, k2_ref,
                x1_ref, h2_ref, s2_ref, e2_ref, thr_ref, w1_ref, qp_ref, a_ref, b_ref):
    x1 = x_ref[...] + jnp.dot(o_ref[...], wo_ref[...], preferred_element_type=F32)
    x1_ref[...] = x1
    h2 = _rms(x1, gffn_ref[...]).astype(BF16)
    h2_ref[...] = h2
    qp_ref[...] = jnp.dot(h2, wpq_ref[...], preferred_element_type=F32)
    key_dim = 2 * PEER_HALF

    def head(h, _):
        qh = qp_ref[:, pl.ds(pl.multiple_of(h * key_dim, key_dim), key_dim)]
        qn = qh * lax.rsqrt(jnp.mean(qh * qh, axis=-1, keepdims=True) + EPS)
        nt = (((1,), (1,)), ((), ()))
        s1 = lax.dot_general(k1_ref[h], qn[:, :PEER_HALF].astype(BF16), nt, preferred_element_type=F32)
        s2 = lax.dot_general(k2_ref[h], qn[:, PEER_HALF:].astype(BF16), nt, preferred_element_type=F32)
        _top_ranks(s1, a_ref)
        _top_ranks(s2, b_ref)
        blocks = [a_ref[0:1, :] + b_ref[...]]
        for r in range(1, 8):
            blocks.append(a_ref[r:r + 1, :] + b_ref[0:8, :])
        blocks.append(a_ref[8:RANK_ROWS, :] + b_ref[0:1, :])
        cand = jnp.concatenate(blocks, axis=0)
        top = a_ref[0:1, :] + b_ref[0:1, :]
        zsum = jnp.zeros_like(top)
        v = top
        for k in range(PEER_TOPK):
            v = jnp.max(cand, axis=0, keepdims=True)
            zsum = zsum + jnp.exp(v - top)
            cand = jnp.where(cand == v, -jnp.inf, cand)
        tau = 0.5 * (v + jnp.max(cand, axis=0, keepdims=True))
        s2_ref[0, h] = s2
        e2_ref[0, h] = jnp.exp(s2 - b_ref[0:1, :])
        thr_ref[0, h] = tau - s1
        w1_ref[0, h] = jnp.exp(s1 - a_ref[0:1, :]) / zsum
        return 0

    lax.fori_loop(0, PEER_HEADS, head, 0)


def _route(x2d, o2d, w, tm, tt):
    T, D = x2d.shape
    r = tt // tm
    kq = w["wpq"].shape[1]
    const = lambda shape: pl.BlockSpec(shape, lambda i: (0,) * len(shape))
    lane_out = pl.BlockSpec((1, PEER_HEADS, PEER_N_KEYS, tm), lambda i: (i // r, 0, 0, i % r))
    sds = jax.ShapeDtypeStruct
    lane_shape = sds((T // tt, PEER_HEADS, PEER_N_KEYS, tt), F32)
    return pl.pallas_call(
        _route_body, grid=(T // tm,),
        in_specs=[pl.BlockSpec((tm, D), lambda i: (i, 0)), pl.BlockSpec((tm, D), lambda i: (i, 0)),
                  const((D, D)), const((1, D)), const((D, kq)),
                  const((PEER_HEADS, PEER_N_KEYS, PEER_HALF)), const((PEER_HEADS, PEER_N_KEYS, PEER_HALF))],
        out_specs=[pl.BlockSpec((tm, D), lambda i: (i, 0)), pl.BlockSpec((tm, D), lambda i: (i, 0)),
                   lane_out, lane_out, lane_out, lane_out],
        out_shape=[sds((T, D), F32), sds((T, D), BF16), lane_shape, lane_shape, lane_shape, lane_shape],
        scratch_shapes=[pltpu.VMEM((tm, kq), F32), pltpu.VMEM((RANK_ROWS, tm), F32), pltpu.VMEM((RANK_ROWS, tm), F32)],
        compiler_params=_params(("arbitrary",)), name="route",
    )(x2d, o2d, w["wo"], w["gffn"], w["wpq"], w["k1"], w["k2"])


GATE_ROWS = 32


def _peer_stage(h2_ref, s2_ref, e2_ref, thr_ref, w1_ref, u_ref, v_ref, acc_ref, a_new, a_cur, act_new, act_prev,
                *, tt, rows_per_step):
    eb = rows_per_step * PEER_N_KEYS
    d_model = v_ref.shape[1]
    inv_sqrt2 = 1.0 / math.sqrt(2.0)

    def mxu_pieces():
        m_half, n_tok = eb // 2, min(256, tt)
        for nt in range(tt // n_tok):
            for mh in range(2):
                def piece(nt=nt, mh=mh):
                    rs, ts = slice(mh * m_half, (mh + 1) * m_half), slice(nt * n_tok, (nt + 1) * n_tok)
                    a_new[rs, ts] = lax.dot_general(u_ref[rs, :], h2_ref[ts, :], (((1,), (1,)), ((), ())),
                                                    preferred_element_type=F32)
                yield piece
        n_col = 256
        for nc in range(d_model // n_col):
            def piece(nc=nc):
                cs = slice(nc * n_col, (nc + 1) * n_col)
                acc_ref[:, cs] += lax.dot_general(act_prev[...], v_ref[:, cs], (((0,), (0,)), ((), ())),
                                                  preferred_element_type=F32)
            yield piece

    def vpu_pieces():
        for tc in range(tt // 128):
            ls = slice(tc * 128, (tc + 1) * 128)
            for jb in range(PEER_N_KEYS // GATE_ROWS):
                def piece(ls=ls, jb=jb):
                    js = slice(jb * GATE_ROWS, (jb + 1) * GATE_ROWS)
                    gates = [jnp.zeros((GATE_ROWS, 128), F32)] * rows_per_step
                    for h in range(PEER_HEADS):
                        s2 = s2_ref[0, h, js, ls]
                        e2 = e2_ref[0, h, js, ls]
                        for ii in range(rows_per_step):
                            sel = s2 >= thr_ref[0, h, ii:ii + 1, ls]
                            gates[ii] = gates[ii] + jnp.where(sel, e2 * w1_ref[0, h, ii:ii + 1, ls], 0.0)
                    for ii in range(rows_per_step):
                        rs = slice(ii * PEER_N_KEYS + jb * GATE_ROWS, ii * PEER_N_KEYS + (jb + 1) * GATE_ROWS)
                        a = a_cur[rs, ls]
                        act_new[rs, ls] = (0.5 * a * (1.0 + lax.erf(a * inv_sqrt2)) * gates[ii]).astype(BF16)
                yield piece

    mxu, vpu = list(mxu_pieces()), list(vpu_pieces())
    per = -(-len(vpu) // len(mxu))
    for k, piece in enumerate(mxu):
        piece()
        for vp in vpu[k * per:(k + 1) * per]:
            vp()
    for vp in vpu[len(mxu) * per:]:
        vp()


def _peer_body(x1_ref, h2_ref, s2_ref, e2_ref, thr_ref, w1_ref, u_ref, v_ref, gfin_ref, y_ref,
               acc_ref, a0_ref, a1_ref, act0_ref, act1_ref, *, tt, rows_per_step, blocks_per_tile):
    g = pl.program_id(0)

    @pl.when(g == 0)
    def _():
        for ref in (acc_ref, a0_ref, a1_ref, act0_ref, act1_ref):
            ref[...] = jnp.zeros_like(ref)

    stage = functools.partial(_peer_stage, h2_ref, s2_ref, e2_ref, thr_ref, w1_ref, u_ref, v_ref, acc_ref,
                              tt=tt, rows_per_step=rows_per_step)
    parity = lax.rem(g, 2)

    @pl.when(parity == 0)
    def _():
        stage(a0_ref, a1_ref, act1_ref, act0_ref)

    @pl.when(parity == 1)
    def _():
        stage(a1_ref, a0_ref, act0_ref, act1_ref)

    @pl.when((g >= 2) & (lax.rem(g - 2, blocks_per_tile) == blocks_per_tile - 1))
    def _():
        y_ref[...] = _rms(x1_ref[...] + acc_ref[...], gfin_ref[...])
        acc_ref[...] = jnp.zeros_like(acc_ref)


def _peer(x1, h2, s2, e2, thr, w1, w, tt, rows_per_step):
    T, D = x1.shape
    n_exp = w["u"].shape[0]
    eb = rows_per_step * PEER_N_KEYS
    nj = n_exp // eb
    n_items = (T // tt) * nj

    def tile(lag):
        return lambda g: lax.div(jnp.clip(g - lag, 0, n_items - 1), nj)

    def blk(lag):
        return lambda g: lax.rem(jnp.clip(g - lag, 0, n_items - 1), nj)

    tok = lambda lag: pl.BlockSpec((tt, D), lambda g: (tile(lag)(g), 0))
    lane_in = pl.BlockSpec((1, PEER_HEADS, PEER_N_KEYS, tt), lambda g: (tile(1)(g), 0, 0, 0))
    row_in = pl.BlockSpec((1, PEER_HEADS, rows_per_step, tt), lambda g: (tile(1)(g), 0, blk(1)(g), 0))
    body = functools.partial(_peer_body, tt=tt, rows_per_step=rows_per_step, blocks_per_tile=nj)
    return pl.pallas_call(
        body, grid=(n_items + 2,),
        in_specs=[tok(2), tok(0), lane_in, lane_in, row_in, row_in,
                  pl.BlockSpec((eb, D), lambda g: (blk(0)(g), 0)), pl.BlockSpec((eb, D), lambda g: (blk(2)(g), 0)),
                  pl.BlockSpec((1, D), lambda g: (0, 0))],
        out_specs=tok(2),
        out_shape=jax.ShapeDtypeStruct((T, D), F32),
        scratch_shapes=[pltpu.VMEM((tt, D), F32), pltpu.VMEM((eb, tt), F32), pltpu.VMEM((eb, tt), F32),
                        pltpu.VMEM((eb, tt), BF16), pltpu.VMEM((eb, tt), BF16)],
        compiler_params=_params(("arbitrary",)), name="peer",
    )(x1, h2, s2, e2, thr, w1, w["u"], w["v"], w["gfin"])


def _rope_tables(pos):
    half = MLA_ROPE_DIM // 2
    inv = ROPE_BASE ** (-jnp.arange(half, dtype=F32) / half)
    ang = pos.astype(F32)[:, None] * inv[None, :]
    cos = jnp.concatenate([jnp.cos(ang)] * 2, axis=-1)
    sin = jnp.concatenate([jnp.sin(ang)] * 2, axis=-1)
    return jnp.tile(cos, (1, MLA_HEADS)), jnp.tile(sin, (1, MLA_HEADS)), cos, sin


def _rot_cols(wcols):
    half = wcols.shape[-1] // 2
    return jnp.concatenate([-wcols[..., half:], wcols[..., :half]], axis=-1)


def _prep_weights(g_mix, w_in, b_f, g_q, w_uq, g_kv, w_ukv, w_o, g_ffn, w_pq, sub_keys, peer_u, peer_v, g_final):
    D = w_in.shape[0]
    off_ff = 3 * FOX_WIDTH
    off_cq = off_ff + FOX_HEADS
    off_ckv = off_cq + MLA_Q_LORA
    off_kr = off_ckv + MLA_KV_LORA
    kr_cols = w_in[:, off_kr:off_kr + MLA_ROPE_DIM]
    tail_pad = jnp.zeros((D, 128 - FOX_HEADS - 2 * MLA_ROPE_DIM), F32)
    win = jnp.concatenate([w_in[:, :off_ff], w_in[:, off_cq:off_kr], w_in[:, off_ff:off_cq],
                           kr_cols, _rot_cols(kr_cols), tail_pad], axis=1).astype(BF16)
    bf = jnp.pad(b_f, (0, 128 - FOX_HEADS)).reshape(1, 128)
    per_q = MLA_NOPE_DIM + MLA_ROPE_DIM
    wq = w_uq.reshape(MLA_Q_LORA, MLA_HEADS, per_q)
    q_rope = wq[:, :, MLA_NOPE_DIM:]
    wuq = jnp.concatenate([wq[:, :, :MLA_NOPE_DIM].reshape(MLA_Q_LORA, -1), q_rope.reshape(MLA_Q_LORA, -1),
                           _rot_cols(q_rope).reshape(MLA_Q_LORA, -1)], axis=1).astype(BF16)
    wkv = w_ukv.reshape(MLA_KV_LORA, MLA_HEADS, MLA_NOPE_DIM + MLA_V_DIM)
    wukv = jnp.concatenate([wkv[:, :, :MLA_NOPE_DIM].reshape(MLA_KV_LORA, -1),
                            wkv[:, :, MLA_NOPE_DIM:].reshape(MLA_KV_LORA, -1)], axis=1).astype(BF16)
    return dict(
        gmix=g_mix.reshape(1, -1), win=win, bf=bf, gq=g_q.reshape(1, -1), wuq=wuq, gkv=g_kv.reshape(1, -1),
        wukv=wukv, wo=w_o.astype(BF16), gffn=g_ffn.reshape(1, -1), wpq=w_pq.astype(BF16),
        k1=sub_keys[0].astype(BF16), k2=sub_keys[1].astype(BF16),
        u=peer_u.astype(BF16), v=peer_v.astype(BF16), gfin=g_final.reshape(1, -1))


def _ffn(x, o, w):
    B, S, D = x.shape
    T = B * S
    tm = min(256, T)
    tt = min(512, T)
    x1, h2, s2, e2, thr, w1 = _route(x.reshape(T, D), o.reshape(T, D), w, tm, tt)
    return _peer(x1, h2, s2, e2, thr, w1, w, tt, 8).reshape(B, S, D)


def kernel(x_prompt, x_sample, cache_fox_k, cache_fox_v, cache_fox_logf, cache_mla_ckv, cache_mla_krope, g_mix, w_in, b_f, g_q, w_uq, g_kv, w_ukv, w_o, g_ffn, w_pq, peer_sub_keys, peer_u, peer_v, g_final):
    assert g_mix.shape[0] == 1, "single-layer model"
    w = _prep_weights(g_mix[0], w_in[0], b_f[0], g_q[0], w_uq[0], g_kv[0], w_ukv[0], w_o[0], g_ffn[0], w_pq[0],
                      peer_sub_keys[0], peer_u[0], peer_v[0], g_final)
    B, S, D = x_prompt.shape
    Bs, Ss, _ = x_sample.shape
    P = cache_fox_k.shape[2]

    tables_p = _rope_tables(jnp.arange(S, dtype=jnp.int32))
    tm = min(512, S)
    fk, fv, logf, ckv, kr, qh, kh, vh = _proj(x_prompt, jnp.zeros((B, 1, 128), F32), tables_p, w, tm)
    tq = min(256, S)
    o_p = _attention(qh, kh, vh, ATTN_HEAD_GROUP, tq, tq, 0, S)
    y_p = _ffn(x_prompt, o_p, w)

    kh_c, vh_c, c_last = _cache_rows(
        cache_fox_k[0].reshape(Bs, P, FOX_WIDTH), cache_fox_v[0].reshape(Bs, P, FOX_WIDTH),
        jnp.pad(cache_fox_logf[0], ((0, 0), (0, 0), (0, 128 - FOX_HEADS))),
        cache_mla_ckv[0], cache_mla_krope[0], w["wukv"])
    tables_s = _rope_tables(P + jnp.arange(Ss, dtype=jnp.int32))
    sfk, sfv, slogf, sckv, skr, sqh, skh, svh = _proj(x_sample, c_last, tables_s, w, Ss)
    tk = 256
    L = P + Ss
    pad = (-L) % tk
    kh_all = jnp.pad(jnp.concatenate([kh_c, skh], axis=2), ((0, 0), (0, 0), (0, pad), (0, 0)))
    vh_all = jnp.pad(jnp.concatenate([vh_c, svh], axis=2), ((0, 0), (0, 0), (0, pad), (0, 0)))
    o_s = _attention(sqh, kh_all, vh_all, ATTN_HEAD_GROUP, Ss, tk, P, L)
    y_s = _ffn(x_sample, o_s, w)

    return (y_p, y_s,
            fk.reshape(1, B, S, FOX_HEADS, FOX_HEAD_DIM), fv.reshape(1, B, S, FOX_HEADS, FOX_HEAD_DIM),
            logf[None], ckv[None], kr[None],
            sfk.reshape(1, Bs, Ss, FOX_HEADS, FOX_HEAD_DIM), sfv.reshape(1, Bs, Ss, FOX_HEADS, FOX_HEAD_DIM),
            slogf[None], sckv[None], skr[None])
```

```python
import functools
import math

import numpy as np
import jax
import jax.numpy as jnp
from jax import lax
from jax.experimental import pallas as pl
from jax.experimental.pallas import tpu as pltpu

F32 = jnp.float32
BF16 = jnp.bfloat16

EPS = 1e-6
CHUNK = 64
ROPE_BASE = 10000.0

FOX_HEADS = 8
FOX_HEAD_DIM = 64
FOX_WIDTH = FOX_HEADS * FOX_HEAD_DIM
MLA_HEADS = 8
MLA_NOPE_DIM = 64
MLA_ROPE_DIM = 32
MLA_V_DIM = 64
MLA_Q_LORA = 256
MLA_KV_LORA = 128
ATTN_HEADS = FOX_HEADS + MLA_HEADS
HEAD_ROW = 128
V_DIM = 64

PEER_HEADS = 8
PEER_N_KEYS = 128
PEER_HALF = 128
PEER_TOPK = 16
N_RANK = PEER_TOPK + 1
RANK_ROWS = 24

Z_FQ = 0
Z_FK = Z_FQ + FOX_WIDTH
Z_FV = Z_FK + FOX_WIDTH
Z_CQ = Z_FV + FOX_WIDTH
Z_CKV = Z_CQ + MLA_Q_LORA
Z_TAIL = Z_CKV + MLA_KV_LORA
Z_WIDTH = Z_TAIL + 128
TAIL_KR = FOX_HEADS
TAIL_KRR = TAIL_KR + MLA_ROPE_DIM

NEG = -1e30
ATTN_HEAD_GROUP = 8
V7X_VMEM_LIMIT_BYTES = 56 * 1024 * 1024


def _params(sem, flags=None):
    return pltpu.CompilerParams(dimension_semantics=sem, vmem_limit_bytes=V7X_VMEM_LIMIT_BYTES, flags=flags)


def _rms(x, g):
    return x * lax.rsqrt(jnp.mean(x * x, axis=-1, keepdims=True) + EPS) * g


def _log_sigmoid(x):
    return jnp.minimum(x, 0.0) - jnp.log1p(jnp.exp(-jnp.abs(x)))


def _split3(x):
    hi = x.astype(BF16)
    r1 = x - hi.astype(F32)
    mid = r1.astype(BF16)
    lo = (r1 - mid.astype(F32)).astype(BF16)
    return hi, mid, lo


def _cumsum_rows(tri, x, carry):
    acc = carry
    for piece in _split3(x):
        acc = acc + jnp.dot(tri, piece, preferred_element_type=F32)
    return acc


def _fox_aux(c_col):
    hi, mid, lo = (p.astype(F32) for p in _split3(c_col))
    lane = lax.broadcasted_iota(jnp.int32, (1, HEAD_ROW - FOX_HEAD_DIM), 1)
    q_aux = jnp.where(lane == 0, hi, jnp.where(lane == 1, mid, jnp.where(
        lane == 2, lo, jnp.where(lane < 6, 1.0, 0.0))))
    k_aux = jnp.where(lane < 3, 1.0, jnp.where(lane == 3, -hi, jnp.where(
        lane == 4, -mid, jnp.where(lane == 5, -lo, 0.0))))
    return q_aux, k_aux


def _write_fox_kv(kh_ref, vh_ref, k, v, c_slab):
    for h in range(FOX_HEADS):
        sl = slice(h * FOX_HEAD_DIM, (h + 1) * FOX_HEAD_DIM)
        _, k_aux = _fox_aux(c_slab[:, h:h + 1])
        kh_ref[0, h, :, 0:FOX_HEAD_DIM] = k[:, sl].astype(BF16)
        kh_ref[0, h, :, FOX_HEAD_DIM:HEAD_ROW] = k_aux.astype(BF16)
        vh_ref[0, h] = v[:, sl].astype(BF16)


def _write_mla_kv(kh_ref, vh_ref, kv, krope):
    rows = kv.shape[0]
    k_rope = krope.astype(BF16)
    pad = jnp.zeros((rows, HEAD_ROW - MLA_NOPE_DIM - MLA_ROPE_DIM), BF16)
    for h in range(MLA_HEADS):
        sl = slice(h * MLA_NOPE_DIM, (h + 1) * MLA_NOPE_DIM)
        kh_ref[0, FOX_HEADS + h, :, 0:MLA_NOPE_DIM] = kv[:, sl].astype(BF16)
        kh_ref[0, FOX_HEADS + h, :, MLA_NOPE_DIM:MLA_NOPE_DIM + MLA_ROPE_DIM] = k_rope
        kh_ref[0, FOX_HEADS + h, :, MLA_NOPE_DIM + MLA_ROPE_DIM:HEAD_ROW] = pad
        vsl = slice(MLA_HEADS * MLA_NOPE_DIM + h * MLA_V_DIM, MLA_HEADS * MLA_NOPE_DIM + (h + 1) * MLA_V_DIM)
        vh_ref[0, FOX_HEADS + h] = kv[:, vsl].astype(BF16)


def _proj_body(x_ref, c0_ref, tri_ref, cosq_ref, sinq_ref, cosk_ref, sink_ref, gmix_ref, win_ref, bf_ref,
               gq_ref, wuq_ref, gkv_ref, wukv_ref,
               fk_ref, fv_ref, logf_ref, ckv_ref, kr_ref, qh_ref, kh_ref, vh_ref, carry_ref):
    @pl.when(pl.program_id(1) == 0)
    def _():
        carry_ref[...] = c0_ref[0]

    rows = x_ref.shape[1]
    h = _rms(x_ref[0], gmix_ref[...]).astype(BF16)
    z = jnp.dot(h, win_ref[...], preferred_element_type=F32)

    fk = z[:, Z_FK:Z_FV]
    fv = z[:, Z_FV:Z_CQ]
    fk_ref[0] = fk
    fv_ref[0] = fv

    tail = z[:, Z_TAIL:Z_WIDTH]
    logf = _log_sigmoid(tail + bf_ref[...])
    logf_ref[0] = logf[:, 0:FOX_HEADS]
    c_slab = _cumsum_rows(tri_ref[...], logf, carry_ref[...])
    carry_ref[...] = c_slab[rows - 1:rows, :]

    krope = (tail[:, TAIL_KR:TAIL_KR + MLA_ROPE_DIM] * cosk_ref[...]
             + tail[:, TAIL_KRR:TAIL_KRR + MLA_ROPE_DIM] * sink_ref[...])
    kr_ref[0] = krope

    cqn = _rms(z[:, Z_CQ:Z_CKV], gq_ref[...]).astype(BF16)
    q = jnp.dot(cqn, wuq_ref[...], preferred_element_type=F32)
    nope_w = MLA_HEADS * MLA_NOPE_DIM
    rope_w = MLA_HEADS * MLA_ROPE_DIM
    qrope = q[:, nope_w:nope_w + rope_w] * cosq_ref[...] + q[:, nope_w + rope_w:nope_w + 2 * rope_w] * sinq_ref[...]

    ckv = _rms(z[:, Z_CKV:Z_TAIL], gkv_ref[...])
    ckv_ref[0] = ckv
    kv = jnp.dot(ckv.astype(BF16), wukv_ref[...], preferred_element_type=F32)

    fox_scale = FOX_HEAD_DIM ** -0.5
    mla_scale = (MLA_NOPE_DIM + MLA_ROPE_DIM) ** -0.5
    qpad = jnp.zeros((rows, HEAD_ROW - MLA_NOPE_DIM - MLA_ROPE_DIM), BF16)
    for hd in range(FOX_HEADS):
        sl = slice(hd * FOX_HEAD_DIM, (hd + 1) * FOX_HEAD_DIM)
        q_aux, _ = _fox_aux(c_slab[:, hd:hd + 1])
        qh_ref[0, hd, :, 0:FOX_HEAD_DIM] = (z[:, sl] * fox_scale).astype(BF16)
        qh_ref[0, hd, :, FOX_HEAD_DIM:HEAD_ROW] = q_aux.astype(BF16)
    for hd in range(MLA_HEADS):
        sl = slice(hd * MLA_NOPE_DIM, (hd + 1) * MLA_NOPE_DIM)
        rsl = slice(hd * MLA_ROPE_DIM, (hd + 1) * MLA_ROPE_DIM)
        qh_ref[0, FOX_HEADS + hd, :, 0:MLA_NOPE_DIM] = (q[:, sl] * mla_scale).astype(BF16)
        qh_ref[0, FOX_HEADS + hd, :, MLA_NOPE_DIM:MLA_NOPE_DIM + MLA_ROPE_DIM] = (qrope[:, rsl] * mla_scale).astype(BF16)
        qh_ref[0, FOX_HEADS + hd, :, MLA_NOPE_DIM + MLA_ROPE_DIM:HEAD_ROW] = qpad
    _write_fox_kv(kh_ref, vh_ref, fk, fv, c_slab)
    _write_mla_kv(kh_ref, vh_ref, kv, krope)


def _proj(x, c0, tables, w, tm):
    B, S, D = x.shape
    cosq, sinq, cosk, sink = tables
    tri = jnp.tril(jnp.ones((tm, tm), F32)).astype(BF16)
    const = lambda shape: pl.BlockSpec(shape, lambda b, t: (0,) * len(shape))
    tab = lambda width: pl.BlockSpec((tm, width), lambda b, t: (t, 0))
    rope_w = MLA_HEADS * MLA_ROPE_DIM
    in_specs = [
        pl.BlockSpec((1, tm, D), lambda b, t: (b, t, 0)),
        pl.BlockSpec((1, 1, 128), lambda b, t: (b, 0, 0)),
        const((tm, tm)),
        tab(rope_w), tab(rope_w), tab(MLA_ROPE_DIM), tab(MLA_ROPE_DIM),
        const((1, D)), const((D, Z_WIDTH)), const((1, 128)),
        const((1, MLA_Q_LORA)), const(w["wuq"].shape), const((1, MLA_KV_LORA)), const(w["wukv"].shape),
    ]
    tok = lambda width: pl.BlockSpec((1, tm, width), lambda b, t: (b, t, 0))
    head = lambda width: pl.BlockSpec((1, ATTN_HEADS, tm, width), lambda b, t: (b, 0, t, 0))
    out_specs = [tok(FOX_WIDTH), tok(FOX_WIDTH), tok(FOX_HEADS), tok(MLA_KV_LORA), tok(MLA_ROPE_DIM),
                 head(HEAD_ROW), head(HEAD_ROW), head(V_DIM)]
    sds = jax.ShapeDtypeStruct
    out_shape = [sds((B, S, FOX_WIDTH), F32), sds((B, S, FOX_WIDTH), F32), sds((B, S, FOX_HEADS), F32),
                 sds((B, S, MLA_KV_LORA), F32), sds((B, S, MLA_ROPE_DIM), F32),
                 sds((B, ATTN_HEADS, S, HEAD_ROW), BF16), sds((B, ATTN_HEADS, S, HEAD_ROW), BF16),
                 sds((B, ATTN_HEADS, S, V_DIM), BF16)]
    return pl.pallas_call(
        _proj_body, grid=(B, S // tm), in_specs=in_specs, out_specs=out_specs, out_shape=out_shape,
        scratch_shapes=[pltpu.VMEM((1, 128), F32)],
        compiler_params=_params(("arbitrary", "arbitrary")), name="proj",
    )(x, c0, tri, cosq, sinq, cosk, sink, w["gmix"], w["win"], w["bf"], w["gq"], w["wuq"], w["gkv"], w["wukv"])


def _cache_body(ck_ref, cv_ref, clogf_ref, cckv_ref, ckr_ref, tri_ref, wukv_ref, kh_ref, vh_ref, clast_ref):
    rows = ck_ref.shape[1]
    c_slab = _cumsum_rows(tri_ref[...], clogf_ref[0], jnp.zeros((1, 128), F32))
    clast_ref[0] = c_slab[rows - 1:rows, :]
    _write_fox_kv(kh_ref, vh_ref, ck_ref[0], cv_ref[0], c_slab)
    kv = jnp.dot(cckv_ref[0].astype(BF16), wukv_ref[...], preferred_element_type=F32)
    _write_mla_kv(kh_ref, vh_ref, kv, ckr_ref[0])


def _cache_rows(ck, cv, clogf_slab, cckv, ckr, wukv):
    B, P, _ = ck.shape
    tri = jnp.tril(jnp.ones((P, P), F32)).astype(BF16)
    per_b = lambda width: pl.BlockSpec((1, P, width), lambda b: (b, 0, 0))
    const = lambda shape: pl.BlockSpec(shape, lambda b: (0,) * len(shape))
    head = lambda width: pl.BlockSpec((1, ATTN_HEADS, P, width), lambda b: (b, 0, 0, 0))
    sds = jax.ShapeDtypeStruct
    return pl.pallas_call(
        _cache_body, grid=(B,),
        in_specs=[per_b(FOX_WIDTH), per_b(FOX_WIDTH), per_b(128), per_b(MLA_KV_LORA), per_b(MLA_ROPE_DIM),
                  const((P, P)), const(wukv.shape)],
        out_specs=[head(HEAD_ROW), head(V_DIM), pl.BlockSpec((1, 1, 128), lambda b: (b, 0, 0))],
        out_shape=[sds((B, ATTN_HEADS, P, HEAD_ROW), BF16), sds((B, ATTN_HEADS, P, V_DIM), BF16),
                   sds((B, 1, 128), F32)],
        compiler_params=_params(("arbitrary",)), name="cache_rows",
    )(ck, cv, clogf_slab, cckv, ckr, tri, wukv)


def _reduce_rows(pair_op, final_op, x):
    while x.shape[0] > 8 and x.shape[0] % 16 == 0:
        half = x.shape[0] // 2
        x = pair_op(x[:half], x[half:])
    return final_op(x, axis=0, keepdims=True)


def _attn_body(q_ref, k_ref, v_ref, o_ref, *, hg, tq, tk, nkv_total, q_off, kv_len):
    grp = pl.program_id(1)
    qi = pl.program_id(2)
    q_lo = q_off + qi * tq
    qpos = q_lo + lax.broadcasted_iota(jnp.int32, (1, tq), 1)
    lim = jnp.where(grp >= FOX_HEADS // hg, qpos | (CHUNK - 1), qpos)
    lim = jnp.minimum(lim, kv_len - 1)
    n_full = jnp.minimum(lax.div(q_lo + 1, tk), nkv_total)
    n_all = jnp.minimum(lax.div((q_lo + tq - 1) | (CHUNK - 1), tk) + 1, nkv_total)

    def step(j, carry, masked):
        off = pl.multiple_of(j * tk, tk)
        if masked:
            visible = off + lax.broadcasted_iota(jnp.int32, (tk, 1), 0) <= lim
        out = []
        scores = [lax.dot_general(k_ref[0, hh, pl.ds(off, tk), :], q_ref[0, hh], (((1,), (1,)), ((), ())),
                                  preferred_element_type=F32) for hh in range(hg)]
        for hh in range(hg):
            m, l, acc = carry[3 * hh:3 * hh + 3]
            vb = v_ref[0, hh, pl.ds(off, tk), :]
            s = scores[hh]
            if masked:
                s = jnp.where(visible, s, NEG)
            m_new = jnp.maximum(m, _reduce_rows(jnp.maximum, jnp.max, s))
            alpha = jnp.exp(m - m_new)
            p = jnp.exp(s - m_new)
            l = alpha * l + _reduce_rows(jnp.add, jnp.sum, p)
            pv = lax.dot_general(vb, p.astype(BF16), (((0,), (0,)), ((), ())), preferred_element_type=F32)
            out += [m_new, l, alpha * acc + pv]
        return tuple(out)

    init = (jnp.full((1, tq), NEG, F32), jnp.zeros((1, tq), F32), jnp.zeros((V_DIM, tq), F32)) * hg
    carry = lax.fori_loop(0, n_full, functools.partial(step, masked=False), init)
    carry = lax.fori_loop(n_full, n_all, functools.partial(step, masked=True), carry)
    o_ref[0] = jnp.concatenate([carry[3 * hh + 2] / carry[3 * hh + 1] for hh in range(hg)], axis=0).astype(BF16)


def _attention(qh, kh, vh, hg, tq, tk, q_off, kv_len):
    B, _, Tq, _ = qh.shape
    Tk = kh.shape[2]
    body = functools.partial(_attn_body, hg=hg, tq=tq, tk=tk, nkv_total=Tk // tk, q_off=q_off, kv_len=kv_len)
    return pl.pallas_call(
        body, grid=(B, ATTN_HEADS // hg, Tq // tq),
        in_specs=[pl.BlockSpec((1, hg, tq, HEAD_ROW), lambda b, g, i: (b, g, i, 0)),
                  pl.BlockSpec((1, hg, Tk, HEAD_ROW), lambda b, g, i: (b, g, 0, 0)),
                  pl.BlockSpec((1, hg, Tk, V_DIM), lambda b, g, i: (b, g, 0, 0))],
        out_specs=pl.BlockSpec((1, hg * V_DIM, tq), lambda b, g, i: (b, g, i)),
        out_shape=jax.ShapeDtypeStruct((B, ATTN_HEADS * V_DIM, Tq), BF16),
        compiler_params=_params(("arbitrary", "arbitrary", "arbitrary")), name="attn",
    )(qh, kh, vh)


def _top_ranks(work, rows_ref):
    rows_ref[...] = jnp.full(rows_ref.shape, -jnp.inf, F32)
    for r in range(N_RANK):
        m = jnp.max(work, axis=0, keepdims=True)
        rows_ref[r:r + 1, :] = m
        if r + 1 < N_RANK:
            work = jnp.where(work == m, -jnp.inf, work)


def _route_body(x_ref, o_ref, wo_ref, gffn_ref, wpq_ref, k1_ref, k2_ref,
                x1_ref, h2_ref, rank_ref, e2_ref, cnt_ref, w1_ref, qp_ref, a_ref, b_ref):
    x1 = x_ref[...] + lax.dot_general(o_ref[0], wo_ref[...], (((0,), (0,)), ((), ())), preferred_element_type=F32)
    x1_ref[...] = x1
    h2 = _rms(x1, gffn_ref[...]).astype(BF16)
    h2_ref[...] = h2
    qp_ref[...] = jnp.dot(h2, wpq_ref[...], preferred_element_type=F32)
    key_dim = 2 * PEER_HALF

    def head(h, _):
        qh = qp_ref[:, pl.ds(pl.multiple_of(h * key_dim, key_dim), key_dim)]
        qn = qh * lax.rsqrt(jnp.mean(qh * qh, axis=-1, keepdims=True) + EPS)
        nt = (((1,), (1,)), ((), ()))
        s1 = lax.dot_general(k1_ref[h], qn[:, :PEER_HALF].astype(BF16), nt, preferred_element_type=F32)
        s2 = lax.dot_general(k2_ref[h], qn[:, PEER_HALF:].astype(BF16), nt, preferred_element_type=F32)
        _top_ranks(s1, a_ref)
        _top_ranks(s2, b_ref)
        blocks = [a_ref[0:1, :] + b_ref[...]]
        for r in range(1, 8):
            blocks.append(a_ref[r:r + 1, :] + b_ref[0:8, :])
        blocks.append(a_ref[8:RANK_ROWS, :] + b_ref[0:1, :])
        cand = jnp.concatenate(blocks, axis=0)
        top = a_ref[0:1, :] + b_ref[0:1, :]
        zsum = jnp.zeros_like(top)
        v = top
        for k in range(PEER_TOPK):
            v = jnp.max(cand, axis=0, keepdims=True)
            zsum = zsum + jnp.exp(v - top)
            cand = jnp.where(cand == v, -jnp.inf, cand)
        tau = 0.5 * (v + jnp.max(cand, axis=0, keepdims=True))
        thr = tau - s1
        cnt = jnp.zeros_like(s1)
        rank = jnp.zeros_like(s2)
        for c in range(N_RANK):
            b_c = b_ref[c:c + 1, :]
            cnt = cnt + jnp.where(b_c >= thr, 1.0, 0.0)
            rank = rank + jnp.where(b_c > s2, 1.0, 0.0)
        rank_ref[0, h] = rank.astype(BF16)
        e2_ref[0, h] = jnp.exp(s2 - b_ref[0:1, :]).astype(BF16)
        cnt_ref[0, h] = cnt
        w1_ref[0, h] = jnp.exp(s1 - a_ref[0:1, :]) / zsum
        return 0

    lax.fori_loop(0, PEER_HEADS, head, 0)


def _route(x2d, o_t, w, tm, tt):
    T, D = x2d.shape
    per_seq = o_t.shape[2] // tm
    r = tt // tm
    kq = w["wpq"].shape[1]
    const = lambda shape: pl.BlockSpec(shape, lambda i: (0,) * len(shape))
    lane_out = pl.BlockSpec((1, PEER_HEADS, PEER_N_KEYS, tm), lambda i: (i // r, 0, 0, i % r))
    sds = jax.ShapeDtypeStruct
    lane_shape = lambda dtype: sds((T // tt, PEER_HEADS, PEER_N_KEYS, tt), dtype)
    return pl.pallas_call(
        _route_body, grid=(T // tm,),
        in_specs=[pl.BlockSpec((tm, D), lambda i: (i, 0)),
                  pl.BlockSpec((1, o_t.shape[1], tm), lambda i: (i // per_seq, 0, i % per_seq)),
                  const((D, D)), const((1, D)), const((D, kq)),
                  const((PEER_HEADS, PEER_N_KEYS, PEER_HALF)), const((PEER_HEADS, PEER_N_KEYS, PEER_HALF))],
        out_specs=[pl.BlockSpec((tm, D), lambda i: (i, 0)), pl.BlockSpec((tm, D), lambda i: (i, 0)),
                   lane_out, lane_out, lane_out, lane_out],
        out_shape=[sds((T, D), F32), sds((T, D), BF16),
                   lane_shape(BF16), lane_shape(BF16), lane_shape(F32), lane_shape(F32)],
        scratch_shapes=[pltpu.VMEM((tm, kq), F32), pltpu.VMEM((RANK_ROWS, tm), F32), pltpu.VMEM((RANK_ROWS, tm), F32)],
        compiler_params=_params(("arbitrary",)), name="route",
    )(x2d, o_t, w["wo"], w["gffn"], w["wpq"], w["k1"], w["k2"])


GATE_ROWS = 32
GATE_LANES = 256
CHUNK_I = 2


def _row_tile(ref, h, i, ls):
    tile = jnp.broadcast_to(ref[0, h, i:i + 1, ls], (8, GATE_LANES)).astype(BF16)
    return jnp.concatenate([tile] * (GATE_ROWS // 8), axis=0)


def _peer_body(x1_ref, h2_ref, rank_ref, e2_ref, cnt_ref, w1_ref, u_ref, v_ref, gfin_ref, y_ref,
               acc_ref, a_ref, *act_refs, tt, rows_per_step):
    j = pl.program_id(1)

    @pl.when(j == 0)
    def _():
        acc_ref[...] = jnp.zeros_like(acc_ref)

    a_ref[...] = lax.dot_general(u_ref[...], h2_ref[...], (((1,), (1,)), ((), ())), preferred_element_type=F32)
    inv_sqrt2 = 1.0 / math.sqrt(2.0)
    rows = CHUNK_I * PEER_N_KEYS
    for c, act_ref in enumerate(act_refs):
        cs = slice(c * rows, (c + 1) * rows)
        a = a_ref.at[cs]
        for k in range(CHUNK_I):
            i = c * CHUNK_I + k
            for tc in range(tt // GATE_LANES):
                ls = slice(tc * GATE_LANES, (tc + 1) * GATE_LANES)
                cnt = [_row_tile(cnt_ref, h, i, ls) for h in range(PEER_HEADS)]
                w1 = [_row_tile(w1_ref, h, i, ls) for h in range(PEER_HEADS)]
                for jb in range(PEER_N_KEYS // GATE_ROWS):
                    js = slice(jb * GATE_ROWS, (jb + 1) * GATE_ROWS)
                    gate = jnp.zeros((GATE_ROWS, GATE_LANES), BF16)
                    for h in range(PEER_HEADS):
                        sel = rank_ref[0, h, js, ls] < cnt[h]
                        gate = gate + jnp.where(sel, e2_ref[0, h, js, ls] * w1[h], 0.0)
                    rs = slice(k * PEER_N_KEYS + jb * GATE_ROWS, k * PEER_N_KEYS + (jb + 1) * GATE_ROWS)
                    av = a[rs, ls]
                    gelu = 0.5 * av * (1.0 + lax.erf(av * inv_sqrt2))
                    act_ref[rs, ls] = gelu.astype(BF16) * gate
        acc_ref[...] += lax.dot_general(act_ref[...], v_ref[cs, :], (((0,), (0,)), ((), ())),
                                        preferred_element_type=F32)

    @pl.when(j == pl.num_programs(1) - 1)
    def _():
        y_ref[...] = _rms(x1_ref[...] + acc_ref[...], gfin_ref[...])


def _peer(x1, h2, rank, e2, cnt, w1, w, tt, rows_per_step):
    T, D = x1.shape
    n_exp = w["v"].shape[0]
    eb = rows_per_step * PEER_N_KEYS
    lane_in = pl.BlockSpec((1, PEER_HEADS, PEER_N_KEYS, tt), lambda t, j: (t, 0, 0, 0))
    row_in = pl.BlockSpec((1, PEER_HEADS, rows_per_step, tt), lambda t, j: (t, 0, j, 0))
    body = functools.partial(_peer_body, tt=tt, rows_per_step=rows_per_step)
    act_chunks = [pltpu.VMEM((CHUNK_I * PEER_N_KEYS, tt), BF16)] * (rows_per_step // CHUNK_I)
    return pl.pallas_call(
        body, grid=(T // tt, n_exp // eb),
        in_specs=[pl.BlockSpec((tt, D), lambda t, j: (t, 0)), pl.BlockSpec((tt, D), lambda t, j: (t, 0)),
                  lane_in, lane_in, row_in, row_in,
                  pl.BlockSpec((eb, D), lambda t, j: (j, 0)), pl.BlockSpec((eb, D), lambda t, j: (j, 0)),
                  pl.BlockSpec((1, D), lambda t, j: (0, 0))],
        out_specs=pl.BlockSpec((tt, D), lambda t, j: (t, 0)),
        out_shape=jax.ShapeDtypeStruct((T, D), F32),
        scratch_shapes=[pltpu.VMEM((tt, D), F32), pltpu.VMEM((eb, tt), F32)] + act_chunks,
        compiler_params=_params(("arbitrary", "arbitrary")), name="peer",
    )(x1, h2, rank, e2, cnt, w1, w["u"], w["v"], w["gfin"])


def _rope_tables(pos):
    half = MLA_ROPE_DIM // 2
    inv = ROPE_BASE ** (-jnp.arange(half, dtype=F32) / half)
    ang = pos.astype(F32)[:, None] * inv[None, :]
    cos = jnp.concatenate([jnp.cos(ang)] * 2, axis=-1)
    sin = jnp.concatenate([jnp.sin(ang)] * 2, axis=-1)
    return jnp.tile(cos, (1, MLA_HEADS)), jnp.tile(sin, (1, MLA_HEADS)), cos, sin


def _rot_cols(wcols):
    half = wcols.shape[-1] // 2
    return jnp.concatenate([-wcols[..., half:], wcols[..., :half]], axis=-1)


def _prep_weights(g_mix, w_in, b_f, g_q, w_uq, g_kv, w_ukv, w_o, g_ffn, w_pq, sub_keys, peer_u, peer_v, g_final):
    D = w_in.shape[0]
    off_ff = 3 * FOX_WIDTH
    off_cq = off_ff + FOX_HEADS
    off_ckv = off_cq + MLA_Q_LORA
    off_kr = off_ckv + MLA_KV_LORA
    kr_cols = w_in[:, off_kr:off_kr + MLA_ROPE_DIM]
    tail_pad = jnp.zeros((D, 128 - FOX_HEADS - 2 * MLA_ROPE_DIM), F32)
    win = jnp.concatenate([w_in[:, :off_ff], w_in[:, off_cq:off_kr], w_in[:, off_ff:off_cq],
                           kr_cols, _rot_cols(kr_cols), tail_pad], axis=1).astype(BF16)
    bf = jnp.pad(b_f, (0, 128 - FOX_HEADS)).reshape(1, 128)
    per_q = MLA_NOPE_DIM + MLA_ROPE_DIM
    wq = w_uq.reshape(MLA_Q_LORA, MLA_HEADS, per_q)
    q_rope = wq[:, :, MLA_NOPE_DIM:]
    wuq = jnp.concatenate([wq[:, :, :MLA_NOPE_DIM].reshape(MLA_Q_LORA, -1), q_rope.reshape(MLA_Q_LORA, -1),
                           _rot_cols(q_rope).reshape(MLA_Q_LORA, -1)], axis=1).astype(BF16)
    wkv = w_ukv.reshape(MLA_KV_LORA, MLA_HEADS, MLA_NOPE_DIM + MLA_V_DIM)
    wukv = jnp.concatenate([wkv[:, :, :MLA_NOPE_DIM].reshape(MLA_KV_LORA, -1),
                            wkv[:, :, MLA_NOPE_DIM:].reshape(MLA_KV_LORA, -1)], axis=1).astype(BF16)
    return dict(
        gmix=g_mix.reshape(1, -1), win=win, bf=bf, gq=g_q.reshape(1, -1), wuq=wuq, gkv=g_kv.reshape(1, -1),
        wukv=wukv, wo=w_o.astype(BF16), gffn=g_ffn.reshape(1, -1), wpq=w_pq.astype(BF16),
        k1=sub_keys[0].astype(BF16), k2=sub_keys[1].astype(BF16),
        u=peer_u.astype(BF16), v=peer_v.astype(BF16), gfin=g_final.reshape(1, -1))


def _ffn(x, o, w):
    B, S, D = x.shape
    T = B * S
    tm = min(256, T)
    tt = min(512, T)
    if S % tm:
        o = jnp.transpose(o, (1, 0, 2)).reshape(1, o.shape[1], T)
    x1, h2, rank, e2, cnt, w1 = _route(x.reshape(T, D), o, w, tm, tt)
    return _peer(x1, h2, rank, e2, cnt, w1, w, tt, 8).reshape(B, S, D)


def kernel(x_prompt, x_sample, cache_fox_k, cache_fox_v, cache_fox_logf, cache_mla_ckv, cache_mla_krope, g_mix, w_in, b_f, g_q, w_uq, g_kv, w_ukv, w_o, g_ffn, w_pq, peer_sub_keys, peer_u, peer_v, g_final):
    assert g_mix.shape[0] == 1, "single-layer model"
    w = _prep_weights(g_mix[0], w_in[0], b_f[0], g_q[0], w_uq[0], g_kv[0], w_ukv[0], w_o[0], g_ffn[0], w_pq[0],
                      peer_sub_keys[0], peer_u[0], peer_v[0], g_final)
    B, S, D = x_prompt.shape
    Bs, Ss, _ = x_sample.shape
    P = cache_fox_k.shape[2]

    tables_p = _rope_tables(jnp.arange(S, dtype=jnp.int32))
    tm = min(512, S)
    fk, fv, logf, ckv, kr, qh, kh, vh = _proj(x_prompt, jnp.zeros((B, 1, 128), F32), tables_p, w, tm)
    tq = min(256, S)
    o_p = _attention(qh, kh, vh, ATTN_HEAD_GROUP, tq, tq, 0, S)
    y_p = _ffn(x_prompt, o_p, w)

    kh_c, vh_c, c_last = _cache_rows(
        cache_fox_k[0].reshape(Bs, P, FOX_WIDTH), cache_fox_v[0].reshape(Bs, P, FOX_WIDTH),
        jnp.pad(cache_fox_logf[0], ((0, 0), (0, 0), (0, 128 - FOX_HEADS))),
        cache_mla_ckv[0], cache_mla_krope[0], w["wukv"])
    tables_s = _rope_tables(P + jnp.arange(Ss, dtype=jnp.int32))
    sfk, sfv, slogf, sckv, skr, sqh, skh, svh = _proj(x_sample, c_last, tables_s, w, Ss)
    tk = 256
    L = P + Ss
    pad = (-L) % tk
    kh_all = jnp.pad(jnp.concatenate([kh_c, skh], axis=2), ((0, 0), (0, 0), (0, pad), (0, 0)))
    vh_all = jnp.pad(jnp.concatenate([vh_c, svh], axis=2), ((0, 0), (0, 0), (0, pad), (0, 0)))
    o_s = _attention(sqh, kh_all, vh_all, ATTN_HEAD_GROUP, Ss, tk, P, L)
    y_s = _ffn(x_sample, o_s, w)

    return (y_p, y_s,
            fk.reshape(1, B, S, FOX_HEADS, FOX_HEAD_DIM), fv.reshape(1, B, S, FOX_HEADS, FOX_HEAD_DIM),
            logf[None], ckv[None], kr[None],
            sfk.reshape(1, Bs, Ss, FOX_HEADS, FOX_HEAD_DIM), sfv.reshape(1, Bs, Ss, FOX_HEADS, FOX_HEAD_DIM),
            slogf[None], sckv[None], skr[None])
```

```python
import functools
import math

import numpy as np
import jax
import jax.numpy as jnp
from jax import lax
from jax.experimental import pallas as pl
from jax.experimental.pallas import tpu as pltpu

F32 = jnp.float32
BF16 = jnp.bfloat16

EPS = 1e-6
CHUNK = 64
ROPE_BASE = 10000.0

FOX_HEADS = 8
FOX_HEAD_DIM = 64
FOX_WIDTH = FOX_HEADS * FOX_HEAD_DIM
MLA_HEADS = 8
MLA_NOPE_DIM = 64
MLA_ROPE_DIM = 32
MLA_V_DIM = 64
MLA_Q_LORA = 256
MLA_KV_LORA = 128
ATTN_HEADS = FOX_HEADS + MLA_HEADS
HEAD_ROW = 128
V_DIM = 64

PEER_HEADS = 8
PEER_N_KEYS = 128
PEER_HALF = 128
PEER_TOPK = 16
N_RANK = PEER_TOPK + 1
RANK_ROWS = 24

Z_FQ = 0
Z_FK = Z_FQ + FOX_WIDTH
Z_FV = Z_FK + FOX_WIDTH
Z_CQ = Z_FV + FOX_WIDTH
Z_CKV = Z_CQ + MLA_Q_LORA
Z_TAIL = Z_CKV + MLA_KV_LORA
Z_WIDTH = Z_TAIL + 128
TAIL_KR = FOX_HEADS
TAIL_KRR = TAIL_KR + MLA_ROPE_DIM

NEG = -1e30
ATTN_HEAD_GROUP = 8
V7X_VMEM_LIMIT_BYTES = 56 * 1024 * 1024


def _params(sem, flags=None):
    return pltpu.CompilerParams(dimension_semantics=sem, vmem_limit_bytes=V7X_VMEM_LIMIT_BYTES, flags=flags)


def _rms(x, g):
    return x * lax.rsqrt(jnp.mean(x * x, axis=-1, keepdims=True) + EPS) * g


def _log_sigmoid(x):
    return jnp.minimum(x, 0.0) - jnp.log1p(jnp.exp(-jnp.abs(x)))


def _split3(x):
    hi = x.astype(BF16)
    r1 = x - hi.astype(F32)
    mid = r1.astype(BF16)
    lo = (r1 - mid.astype(F32)).astype(BF16)
    return hi, mid, lo


def _cumsum_rows(tri, x, carry):
    acc = carry
    for piece in _split3(x):
        acc = acc + jnp.dot(tri, piece, preferred_element_type=F32)
    return acc


def _fox_aux(c_col):
    hi, mid, lo = (p.astype(F32) for p in _split3(c_col))
    lane = lax.broadcasted_iota(jnp.int32, (1, HEAD_ROW - FOX_HEAD_DIM), 1)
    q_aux = jnp.where(lane == 0, hi, jnp.where(lane == 1, mid, jnp.where(
        lane == 2, lo, jnp.where(lane < 6, 1.0, 0.0))))
    k_aux = jnp.where(lane < 3, 1.0, jnp.where(lane == 3, -hi, jnp.where(
        lane == 4, -mid, jnp.where(lane == 5, -lo, 0.0))))
    return q_aux, k_aux


def _write_fox_kv(kh_ref, vh_ref, k, v, c_slab):
    for h in range(FOX_HEADS):
        sl = slice(h * FOX_HEAD_DIM, (h + 1) * FOX_HEAD_DIM)
        _, k_aux = _fox_aux(c_slab[:, h:h + 1])
        kh_ref[0, h, :, 0:FOX_HEAD_DIM] = k[:, sl].astype(BF16)
        kh_ref[0, h, :, FOX_HEAD_DIM:HEAD_ROW] = k_aux.astype(BF16)
        vh_ref[0, h] = v[:, sl].astype(BF16)


def _write_mla_kv(kh_ref, vh_ref, kv, krope):
    rows = kv.shape[0]
    k_rope = krope.astype(BF16)
    pad = jnp.zeros((rows, HEAD_ROW - MLA_NOPE_DIM - MLA_ROPE_DIM), BF16)
    for h in range(MLA_HEADS):
        sl = slice(h * MLA_NOPE_DIM, (h + 1) * MLA_NOPE_DIM)
        kh_ref[0, FOX_HEADS + h, :, 0:MLA_NOPE_DIM] = kv[:, sl].astype(BF16)
        kh_ref[0, FOX_HEADS + h, :, MLA_NOPE_DIM:MLA_NOPE_DIM + MLA_ROPE_DIM] = k_rope
        kh_ref[0, FOX_HEADS + h, :, MLA_NOPE_DIM + MLA_ROPE_DIM:HEAD_ROW] = pad
        vsl = slice(MLA_HEADS * MLA_NOPE_DIM + h * MLA_V_DIM, MLA_HEADS * MLA_NOPE_DIM + (h + 1) * MLA_V_DIM)
        vh_ref[0, FOX_HEADS + h] = kv[:, vsl].astype(BF16)


def _proj_body(x_ref, c0_ref, tri_ref, cosq_ref, sinq_ref, cosk_ref, sink_ref, gmix_ref, win_ref, bf_ref,
               gq_ref, wuq_ref, gkv_ref, wukv_ref,
               fk_ref, fv_ref, logf_ref, ckv_ref, kr_ref, qh_ref, kh_ref, vh_ref, carry_ref):
    @pl.when(pl.program_id(1) == 0)
    def _():
        carry_ref[...] = c0_ref[0]

    rows = x_ref.shape[1]
    h = _rms(x_ref[0], gmix_ref[...]).astype(BF16)
    z = jnp.dot(h, win_ref[...], preferred_element_type=F32)

    fk = z[:, Z_FK:Z_FV]
    fv = z[:, Z_FV:Z_CQ]
    fk_ref[0] = fk
    fv_ref[0] = fv

    tail = z[:, Z_TAIL:Z_WIDTH]
    logf = _log_sigmoid(tail + bf_ref[...])
    logf_ref[0] = logf[:, 0:FOX_HEADS]
    c_slab = _cumsum_rows(tri_ref[...], logf, carry_ref[...])
    carry_ref[...] = c_slab[rows - 1:rows, :]

    krope = (tail[:, TAIL_KR:TAIL_KR + MLA_ROPE_DIM] * cosk_ref[...]
             + tail[:, TAIL_KRR:TAIL_KRR + MLA_ROPE_DIM] * sink_ref[...])
    kr_ref[0] = krope

    cqn = _rms(z[:, Z_CQ:Z_CKV], gq_ref[...]).astype(BF16)
    q = jnp.dot(cqn, wuq_ref[...], preferred_element_type=F32)
    nope_w = MLA_HEADS * MLA_NOPE_DIM
    rope_w = MLA_HEADS * MLA_ROPE_DIM
    qrope = q[:, nope_w:nope_w + rope_w] * cosq_ref[...] + q[:, nope_w + rope_w:nope_w + 2 * rope_w] * sinq_ref[...]

    ckv = _rms(z[:, Z_CKV:Z_TAIL], gkv_ref[...])
    ckv_ref[0] = ckv
    kv = jnp.dot(ckv.astype(BF16), wukv_ref[...], preferred_element_type=F32)

    fox_scale = FOX_HEAD_DIM ** -0.5
    mla_scale = (MLA_NOPE_DIM + MLA_ROPE_DIM) ** -0.5
    qpad = jnp.zeros((rows, HEAD_ROW - MLA_NOPE_DIM - MLA_ROPE_DIM), BF16)
    for hd in range(FOX_HEADS):
        sl = slice(hd * FOX_HEAD_DIM, (hd + 1) * FOX_HEAD_DIM)
        q_aux, _ = _fox_aux(c_slab[:, hd:hd + 1])
        qh_ref[0, hd, :, 0:FOX_HEAD_DIM] = (z[:, sl] * fox_scale).astype(BF16)
        qh_ref[0, hd, :, FOX_HEAD_DIM:HEAD_ROW] = q_aux.astype(BF16)
    for hd in range(MLA_HEADS):
        sl = slice(hd * MLA_NOPE_DIM, (hd + 1) * MLA_NOPE_DIM)
        rsl = slice(hd * MLA_ROPE_DIM, (hd + 1) * MLA_ROPE_DIM)
        qh_ref[0, FOX_HEADS + hd, :, 0:MLA_NOPE_DIM] = (q[:, sl] * mla_scale).astype(BF16)
        qh_ref[0, FOX_HEADS + hd, :, MLA_NOPE_DIM:MLA_NOPE_DIM + MLA_ROPE_DIM] = (qrope[:, rsl] * mla_scale).astype(BF16)
        qh_ref[0, FOX_HEADS + hd, :, MLA_NOPE_DIM + MLA_ROPE_DIM:HEAD_ROW] = qpad
    _write_fox_kv(kh_ref, vh_ref, fk, fv, c_slab)
    _write_mla_kv(kh_ref, vh_ref, kv, krope)


def _proj(x, c0, tables, w, tm):
    B, S, D = x.shape
    cosq, sinq, cosk, sink = tables
    tri = jnp.tril(jnp.ones((tm, tm), F32)).astype(BF16)
    const = lambda shape: pl.BlockSpec(shape, lambda b, t: (0,) * len(shape))
    tab = lambda width: pl.BlockSpec((tm, width), lambda b, t: (t, 0))
    rope_w = MLA_HEADS * MLA_ROPE_DIM
    in_specs = [
        pl.BlockSpec((1, tm, D), lambda b, t: (b, t, 0)),
        pl.BlockSpec((1, 1, 128), lambda b, t: (b, 0, 0)),
        const((tm, tm)),
        tab(rope_w), tab(rope_w), tab(MLA_ROPE_DIM), tab(MLA_ROPE_DIM),
        const((1, D)), const((D, Z_WIDTH)), const((1, 128)),
        const((1, MLA_Q_LORA)), const(w["wuq"].shape), const((1, MLA_KV_LORA)), const(w["wukv"].shape),
    ]
    tok = lambda width: pl.BlockSpec((1, tm, width), lambda b, t: (b, t, 0))
    head = lambda width: pl.BlockSpec((1, ATTN_HEADS, tm, width), lambda b, t: (b, 0, t, 0))
    out_specs = [tok(FOX_WIDTH), tok(FOX_WIDTH), tok(FOX_HEADS), tok(MLA_KV_LORA), tok(MLA_ROPE_DIM),
                 head(HEAD_ROW), head(HEAD_ROW), head(V_DIM)]
    sds = jax.ShapeDtypeStruct
    out_shape = [sds((B, S, FOX_WIDTH), F32), sds((B, S, FOX_WIDTH), F32), sds((B, S, FOX_HEADS), F32),
                 sds((B, S, MLA_KV_LORA), F32), sds((B, S, MLA_ROPE_DIM), F32),
                 sds((B, ATTN_HEADS, S, HEAD_ROW), BF16), sds((B, ATTN_HEADS, S, HEAD_ROW), BF16),
                 sds((B, ATTN_HEADS, S, V_DIM), BF16)]
    return pl.pallas_call(
        _proj_body, grid=(B, S // tm), in_specs=in_specs, out_specs=out_specs, out_shape=out_shape,
        scratch_shapes=[pltpu.VMEM((1, 128), F32)],
        compiler_params=_params(("arbitrary", "arbitrary")), name="proj",
    )(x, c0, tri, cosq, sinq, cosk, sink, w["gmix"], w["win"], w["bf"], w["gq"], w["wuq"], w["gkv"], w["wukv"])


def _cache_body(ck_ref, cv_ref, clogf_ref, cckv_ref, ckr_ref, tri_ref, wukv_ref, kh_ref, vh_ref, clast_ref):
    rows = ck_ref.shape[1]
    c_slab = _cumsum_rows(tri_ref[...], clogf_ref[0], jnp.zeros((1, 128), F32))
    clast_ref[0] = c_slab[rows - 1:rows, :]
    _write_fox_kv(kh_ref, vh_ref, ck_ref[0], cv_ref[0], c_slab)
    kv = jnp.dot(cckv_ref[0].astype(BF16), wukv_ref[...], preferred_element_type=F32)
    _write_mla_kv(kh_ref, vh_ref, kv, ckr_ref[0])


def _cache_rows(ck, cv, clogf_slab, cckv, ckr, wukv):
    B, P, _ = ck.shape
    tri = jnp.tril(jnp.ones((P, P), F32)).astype(BF16)
    per_b = lambda width: pl.BlockSpec((1, P, width), lambda b: (b, 0, 0))
    const = lambda shape: pl.BlockSpec(shape, lambda b: (0,) * len(shape))
    head = lambda width: pl.BlockSpec((1, ATTN_HEADS, P, width), lambda b: (b, 0, 0, 0))
    sds = jax.ShapeDtypeStruct
    return pl.pallas_call(
        _cache_body, grid=(B,),
        in_specs=[per_b(FOX_WIDTH), per_b(FOX_WIDTH), per_b(128), per_b(MLA_KV_LORA), per_b(MLA_ROPE_DIM),
                  const((P, P)), const(wukv.shape)],
        out_specs=[head(HEAD_ROW), head(V_DIM), pl.BlockSpec((1, 1, 128), lambda b: (b, 0, 0))],
        out_shape=[sds((B, ATTN_HEADS, P, HEAD_ROW), BF16), sds((B, ATTN_HEADS, P, V_DIM), BF16),
                   sds((B, 1, 128), F32)],
        compiler_params=_params(("arbitrary",)), name="cache_rows",
    )(ck, cv, clogf_slab, cckv, ckr, tri, wukv)


def _reduce_rows(pair_op, final_op, x):
    while x.shape[0] > 8 and x.shape[0] % 16 == 0:
        half = x.shape[0] // 2
        x = pair_op(x[:half], x[half:])
    return final_op(x, axis=0, keepdims=True)


def _attn_body(q_ref, k_ref, v_ref, o_ref, *, hg, tq, tk, nkv_total, q_off, kv_len):
    grp = pl.program_id(1)
    qi = pl.program_id(2)
    q_lo = q_off + qi * tq
    qpos = q_lo + lax.broadcasted_iota(jnp.int32, (1, tq), 1)
    lim = jnp.where(grp >= FOX_HEADS // hg, qpos | (CHUNK - 1), qpos)
    lim = jnp.minimum(lim, kv_len - 1)
    n_full = jnp.minimum(lax.div(q_lo + 1, tk), nkv_total)
    n_all = jnp.minimum(lax.div((q_lo + tq - 1) | (CHUNK - 1), tk) + 1, nkv_total)

    def step(j, carry, masked):
        off = pl.multiple_of(j * tk, tk)
        if masked:
            visible = off + lax.broadcasted_iota(jnp.int32, (tk, 1), 0) <= lim
        out = []
        scores = [lax.dot_general(k_ref[0, hh, pl.ds(off, tk), :], q_ref[0, hh], (((1,), (1,)), ((), ())),
                                  preferred_element_type=F32) for hh in range(hg)]
        for hh in range(hg):
            m, l, acc = carry[3 * hh:3 * hh + 3]
            vb = v_ref[0, hh, pl.ds(off, tk), :]
            s = scores[hh]
            if masked:
                s = jnp.where(visible, s, NEG)
            m_new = jnp.maximum(m, _reduce_rows(jnp.maximum, jnp.max, s))
            alpha = jnp.exp(m - m_new)
            p = jnp.exp(s - m_new)
            l = alpha * l + _reduce_rows(jnp.add, jnp.sum, p)
            pv = lax.dot_general(vb, p.astype(BF16), (((0,), (0,)), ((), ())), preferred_element_type=F32)
            out += [m_new, l, alpha * acc + pv]
        return tuple(out)

    init = (jnp.full((1, tq), NEG, F32), jnp.zeros((1, tq), F32), jnp.zeros((V_DIM, tq), F32)) * hg
    carry = lax.fori_loop(0, n_full, functools.partial(step, masked=False), init)
    carry = lax.fori_loop(n_full, n_all, functools.partial(step, masked=True), carry)
    o_ref[0] = jnp.concatenate([carry[3 * hh + 2] / carry[3 * hh + 1] for hh in range(hg)], axis=0).astype(BF16)


def _attention(qh, kh, vh, hg, tq, tk, q_off, kv_len):
    B, _, Tq, _ = qh.shape
    Tk = kh.shape[2]
    body = functools.partial(_attn_body, hg=hg, tq=tq, tk=tk, nkv_total=Tk // tk, q_off=q_off, kv_len=kv_len)
    return pl.pallas_call(
        body, grid=(B, ATTN_HEADS // hg, Tq // tq),
        in_specs=[pl.BlockSpec((1, hg, tq, HEAD_ROW), lambda b, g, i: (b, g, i, 0)),
                  pl.BlockSpec((1, hg, Tk, HEAD_ROW), lambda b, g, i: (b, g, 0, 0)),
                  pl.BlockSpec((1, hg, Tk, V_DIM), lambda b, g, i: (b, g, 0, 0))],
        out_specs=pl.BlockSpec((1, hg * V_DIM, tq), lambda b, g, i: (b, g, i)),
        out_shape=jax.ShapeDtypeStruct((B, ATTN_HEADS * V_DIM, Tq), BF16),
        compiler_params=_params(("arbitrary", "arbitrary", "arbitrary")), name="attn",
    )(qh, kh, vh)


def _top_ranks(work, rows_ref):
    rows_ref[...] = jnp.full(rows_ref.shape, -jnp.inf, F32)
    rank = jnp.full(work.shape, float(N_RANK), F32)
    for r in range(N_RANK):
        m = _reduce_rows(jnp.maximum, jnp.max, work)
        rows_ref[r:r + 1, :] = m
        hit = work == m
        rank = jnp.where(hit, float(r), rank)
        if r + 1 < N_RANK:
            work = jnp.where(hit, -jnp.inf, work)
    return rank


def _route_body(x_ref, o_ref, wo_ref, gffn_ref, wpq_ref, k1_ref, k2_ref,
                x1_ref, h2_ref, rank_ref, e2_ref, cnt_ref, w1_ref, qp_ref, a0_ref, b0_ref, a1_ref, b1_ref):
    x1 = x_ref[...] + lax.dot_general(o_ref[0], wo_ref[...], (((0,), (0,)), ((), ())), preferred_element_type=F32)
    x1_ref[...] = x1
    h2 = _rms(x1, gffn_ref[...]).astype(BF16)
    h2_ref[...] = h2
    qp_ref[...] = jnp.dot(h2, wpq_ref[...], preferred_element_type=F32)
    key_dim = 2 * PEER_HALF

    def head(h, a_ref, b_ref):
        qh = qp_ref[:, pl.ds(pl.multiple_of(h * key_dim, key_dim), key_dim)]
        qn = qh * lax.rsqrt(jnp.mean(qh * qh, axis=-1, keepdims=True) + EPS)
        nt = (((1,), (1,)), ((), ()))
        s1 = lax.dot_general(k1_ref[h], qn[:, :PEER_HALF].astype(BF16), nt, preferred_element_type=F32)
        s2 = lax.dot_general(k2_ref[h], qn[:, PEER_HALF:].astype(BF16), nt, preferred_element_type=F32)
        _top_ranks(s1, a_ref)
        rank = _top_ranks(s2, b_ref)
        blocks = [a_ref[0:1, :] + b_ref[...]]
        for r in range(1, 8):
            blocks.append(a_ref[r:r + 1, :] + b_ref[0:8, :])
        blocks.append(a_ref[8:RANK_ROWS, :] + b_ref[0:1, :])
        cand = jnp.concatenate(blocks, axis=0)
        top = a_ref[0:1, :] + b_ref[0:1, :]
        zsum = jnp.zeros_like(top)
        v = top
        for k in range(PEER_TOPK):
            v = jnp.max(cand, axis=0, keepdims=True)
            zsum = zsum + jnp.exp(v - top)
            cand = jnp.where(cand == v, -jnp.inf, cand)
        tau = 0.5 * (v + jnp.max(cand, axis=0, keepdims=True))
        thr = tau - s1
        cnt = jnp.zeros_like(s1)
        for c in range(N_RANK):
            cnt = cnt + jnp.where(b_ref[c:c + 1, :] >= thr, 1.0, 0.0)
        rank_ref[0, h] = rank.astype(BF16)
        e2_ref[0, h] = jnp.exp(s2 - b_ref[0:1, :]).astype(BF16)
        cnt_ref[0, h] = cnt
        w1_ref[0, h] = jnp.exp(s1 - a_ref[0:1, :]) / zsum

    def head_pair(p, carry):
        head(2 * p, a0_ref, b0_ref)
        head(2 * p + 1, a1_ref, b1_ref)
        return carry

    lax.fori_loop(0, PEER_HEADS // 2, head_pair, 0)


def _route(x2d, o_t, w, tm, tt):
    T, D = x2d.shape
    per_seq = o_t.shape[2] // tm
    r = tt // tm
    kq = w["wpq"].shape[1]
    const = lambda shape: pl.BlockSpec(shape, lambda i: (0,) * len(shape))
    lane_out = pl.BlockSpec((1, PEER_HEADS, PEER_N_KEYS, tm), lambda i: (i // r, 0, 0, i % r))
    sds = jax.ShapeDtypeStruct
    lane_shape = lambda dtype: sds((T // tt, PEER_HEADS, PEER_N_KEYS, tt), dtype)
    return pl.pallas_call(
        _route_body, grid=(T // tm,),
        in_specs=[pl.BlockSpec((tm, D), lambda i: (i, 0)),
                  pl.BlockSpec((1, o_t.shape[1], tm), lambda i: (i // per_seq, 0, i % per_seq)),
                  const((D, D)), const((1, D)), const((D, kq)),
                  const((PEER_HEADS, PEER_N_KEYS, PEER_HALF)), const((PEER_HEADS, PEER_N_KEYS, PEER_HALF))],
        out_specs=[pl.BlockSpec((tm, D), lambda i: (i, 0)), pl.BlockSpec((tm, D), lambda i: (i, 0)),
                   lane_out, lane_out, lane_out, lane_out],
        out_shape=[sds((T, D), F32), sds((T, D), BF16),
                   lane_shape(BF16), lane_shape(BF16), lane_shape(F32), lane_shape(F32)],
        scratch_shapes=[pltpu.VMEM((tm, kq), F32)] + [pltpu.VMEM((RANK_ROWS, tm), F32)] * 4,
        compiler_params=_params(("arbitrary",)), name="route",
    )(x2d, o_t, w["wo"], w["gffn"], w["wpq"], w["k1"], w["k2"])


GATE_ROWS = 32
GATE_LANES = 256
CHUNK_I = 2


def _row_tile(ref, h, i, ls):
    tile = jnp.broadcast_to(ref[0, h, i:i + 1, ls], (8, GATE_LANES)).astype(BF16)
    return jnp.concatenate([tile] * (GATE_ROWS // 8), axis=0)


def _peer_body(x1_ref, h2_ref, rank_ref, e2_ref, cnt_ref, w1_ref, u_ref, v_ref, gfin_ref, y_ref,
               acc_ref, *chunk_refs, tt, rows_per_step):
    j = pl.program_id(1)

    @pl.when(j == 0)
    def _():
        acc_ref[...] = jnp.zeros_like(acc_ref)

    n_chunks = len(chunk_refs) // 2
    a_refs, act_refs = chunk_refs[:n_chunks], chunk_refs[n_chunks:]
    inv_sqrt2 = 1.0 / math.sqrt(2.0)
    rows = CHUNK_I * PEER_N_KEYS

    def score_matmul(c):
        cs = slice(c * rows, (c + 1) * rows)
        a_refs[c][...] = lax.dot_general(u_ref[cs, :], h2_ref[...], (((1,), (1,)), ((), ())),
                                         preferred_element_type=F32)

    score_matmul(0)
    score_matmul(1)
    for c, act_ref in enumerate(act_refs):
        if c + 2 < n_chunks:
            score_matmul(c + 2)
        cs = slice(c * rows, (c + 1) * rows)
        a = a_refs[c]
        for k in range(CHUNK_I):
            i = c * CHUNK_I + k
            for tc in range(tt // GATE_LANES):
                ls = slice(tc * GATE_LANES, (tc + 1) * GATE_LANES)
                cnt = [_row_tile(cnt_ref, h, i, ls) for h in range(PEER_HEADS)]
                w1 = [_row_tile(w1_ref, h, i, ls) for h in range(PEER_HEADS)]
                for jb in range(PEER_N_KEYS // GATE_ROWS):
                    js = slice(jb * GATE_ROWS, (jb + 1) * GATE_ROWS)
                    gate = jnp.zeros((GATE_ROWS, GATE_LANES), BF16)
                    for h in range(PEER_HEADS):
                        sel = rank_ref[0, h, js, ls] < cnt[h]
                        gate = gate + jnp.where(sel, e2_ref[0, h, js, ls] * w1[h], 0.0)
                    rs = slice(k * PEER_N_KEYS + jb * GATE_ROWS, k * PEER_N_KEYS + (jb + 1) * GATE_ROWS)
                    av = a[rs, ls]
                    gelu = 0.5 * av * (1.0 + lax.erf(av * inv_sqrt2))
                    act_ref[rs, ls] = gelu.astype(BF16) * gate
        acc_ref[...] += lax.dot_general(act_ref[...], v_ref[cs, :], (((0,), (0,)), ((), ())),
                                        preferred_element_type=F32)

    @pl.when(j == pl.num_programs(1) - 1)
    def _():
        y_ref[...] = _rms(x1_ref[...] + acc_ref[...], gfin_ref[...])


def _peer(x1, h2, rank, e2, cnt, w1, w, tt, rows_per_step):
    T, D = x1.shape
    n_exp = w["v"].shape[0]
    eb = rows_per_step * PEER_N_KEYS
    lane_in = pl.BlockSpec((1, PEER_HEADS, PEER_N_KEYS, tt), lambda t, j: (t, 0, 0, 0))
    row_in = pl.BlockSpec((1, PEER_HEADS, rows_per_step, tt), lambda t, j: (t, 0, j, 0))
    body = functools.partial(_peer_body, tt=tt, rows_per_step=rows_per_step)
    n_chunks = rows_per_step // CHUNK_I
    a_chunks = [pltpu.VMEM((CHUNK_I * PEER_N_KEYS, tt), F32)] * n_chunks
    act_chunks = [pltpu.VMEM((CHUNK_I * PEER_N_KEYS, tt), BF16)] * n_chunks
    return pl.pallas_call(
        body, grid=(T // tt, n_exp // eb),
        in_specs=[pl.BlockSpec((tt, D), lambda t, j: (t, 0)), pl.BlockSpec((tt, D), lambda t, j: (t, 0)),
                  lane_in, lane_in, row_in, row_in,
                  pl.BlockSpec((eb, D), lambda t, j: (j, 0)), pl.BlockSpec((eb, D), lambda t, j: (j, 0)),
                  pl.BlockSpec((1, D), lambda t, j: (0, 0))],
        out_specs=pl.BlockSpec((tt, D), lambda t, j: (t, 0)),
        out_shape=jax.ShapeDtypeStruct((T, D), F32),
        scratch_shapes=[pltpu.VMEM((tt, D), F32)] + a_chunks + act_chunks,
        compiler_params=_params(("arbitrary", "arbitrary")), name="peer",
    )(x1, h2, rank, e2, cnt, w1, w["u"], w["v"], w["gfin"])


def _rope_tables(pos):
    half = MLA_ROPE_DIM // 2
    inv = ROPE_BASE ** (-jnp.arange(half, dtype=F32) / half)
    ang = pos.astype(F32)[:, None] * inv[None, :]
    cos = jnp.concatenate([jnp.cos(ang)] * 2, axis=-1)
    sin = jnp.concatenate([jnp.sin(ang)] * 2, axis=-1)
    return jnp.tile(cos, (1, MLA_HEADS)), jnp.tile(sin, (1, MLA_HEADS)), cos, sin


def _rot_cols(wcols):
    half = wcols.shape[-1] // 2
    return jnp.concatenate([-wcols[..., half:], wcols[..., :half]], axis=-1)


def _prep_weights(g_mix, w_in, b_f, g_q, w_uq, g_kv, w_ukv, w_o, g_ffn, w_pq, sub_keys, peer_u, peer_v, g_final):
    D = w_in.shape[0]
    off_ff = 3 * FOX_WIDTH
    off_cq = off_ff + FOX_HEADS
    off_ckv = off_cq + MLA_Q_LORA
    off_kr = off_ckv + MLA_KV_LORA
    kr_cols = w_in[:, off_kr:off_kr + MLA_ROPE_DIM]
    tail_pad = jnp.zeros((D, 128 - FOX_HEADS - 2 * MLA_ROPE_DIM), F32)
    win = jnp.concatenate([w_in[:, :off_ff], w_in[:, off_cq:off_kr], w_in[:, off_ff:off_cq],
                           kr_cols, _rot_cols(kr_cols), tail_pad], axis=1).astype(BF16)
    bf = jnp.pad(b_f, (0, 128 - FOX_HEADS)).reshape(1, 128)
    per_q = MLA_NOPE_DIM + MLA_ROPE_DIM
    wq = w_uq.reshape(MLA_Q_LORA, MLA_HEADS, per_q)
    q_rope = wq[:, :, MLA_NOPE_DIM:]
    wuq = jnp.concatenate([wq[:, :, :MLA_NOPE_DIM].reshape(MLA_Q_LORA, -1), q_rope.reshape(MLA_Q_LORA, -1),
                           _rot_cols(q_rope).reshape(MLA_Q_LORA, -1)], axis=1).astype(BF16)
    wkv = w_ukv.reshape(MLA_KV_LORA, MLA_HEADS, MLA_NOPE_DIM + MLA_V_DIM)
    wukv = jnp.concatenate([wkv[:, :, :MLA_NOPE_DIM].reshape(MLA_KV_LORA, -1),
                            wkv[:, :, MLA_NOPE_DIM:].reshape(MLA_KV_LORA, -1)], axis=1).astype(BF16)
    return dict(
        gmix=g_mix.reshape(1, -1), win=win, bf=bf, gq=g_q.reshape(1, -1), wuq=wuq, gkv=g_kv.reshape(1, -1),
        wukv=wukv, wo=w_o.astype(BF16), gffn=g_ffn.reshape(1, -1), wpq=w_pq.astype(BF16),
        k1=sub_keys[0].astype(BF16), k2=sub_keys[1].astype(BF16),
        u=peer_u.astype(BF16), v=peer_v.astype(BF16), gfin=g_final.reshape(1, -1))


def _ffn(x, o, w):
    B, S, D = x.shape
    T = B * S
    tm = min(256, T)
    tt = min(512, T)
    if S % tm:
        o = jnp.transpose(o, (1, 0, 2)).reshape(1, o.shape[1], T)
    x1, h2, rank, e2, cnt, w1 = _route(x.reshape(T, D), o, w, tm, tt)
    return _peer(x1, h2, rank, e2, cnt, w1, w, tt, 8).reshape(B, S, D)


def kernel(x_prompt, x_sample, cache_fox_k, cache_fox_v, cache_fox_logf, cache_mla_ckv, cache_mla_krope, g_mix, w_in, b_f, g_q, w_uq, g_kv, w_ukv, w_o, g_ffn, w_pq, peer_sub_keys, peer_u, peer_v, g_final):
    assert g_mix.shape[0] == 1, "single-layer model"
    w = _prep_weights(g_mix[0], w_in[0], b_f[0], g_q[0], w_uq[0], g_kv[0], w_ukv[0], w_o[0], g_ffn[0], w_pq[0],
                      peer_sub_keys[0], peer_u[0], peer_v[0], g_final)
    B, S, D = x_prompt.shape
    Bs, Ss, _ = x_sample.shape
    P = cache_fox_k.shape[2]

    tables_p = _rope_tables(jnp.arange(S, dtype=jnp.int32))
    tm = min(512, S)
    fk, fv, logf, ckv, kr, qh, kh, vh = _proj(x_prompt, jnp.zeros((B, 1, 128), F32), tables_p, w, tm)
    tq = min(256, S)
    o_p = _attention(qh, kh, vh, ATTN_HEAD_GROUP, tq, tq, 0, S)
    y_p = _ffn(x_prompt, o_p, w)

    kh_c, vh_c, c_last = _cache_rows(
        cache_fox_k[0].reshape(Bs, P, FOX_WIDTH), cache_fox_v[0].reshape(Bs, P, FOX_WIDTH),
        jnp.pad(cache_fox_logf[0], ((0, 0), (0, 0), (0, 128 - FOX_HEADS))),
        cache_mla_ckv[0], cache_mla_krope[0], w["wukv"])
    tables_s = _rope_tables(P + jnp.arange(Ss, dtype=jnp.int32))
    sfk, sfv, slogf, sckv, skr, sqh, skh, svh = _proj(x_sample, c_last, tables_s, w, Ss)
    tk = 256
    L = P + Ss
    pad = (-L) % tk
    kh_all = jnp.pad(jnp.concatenate([kh_c, skh], axis=2), ((0, 0), (0, 0), (0, pad), (0, 0)))
    vh_all = jnp.pad(jnp.concatenate([vh_c, svh], axis=2), ((0, 0), (0, 0), (0, pad), (0, 0)))
    o_s = _attention(sqh, kh_all, vh_all, ATTN_HEAD_GROUP, Ss, tk, P, L)
    y_s = _ffn(x_sample, o_s, w)

    return (y_p, y_s,
            fk.reshape(1, B, S, FOX_HEADS, FOX_HEAD_DIM), fv.reshape(1, B, S, FOX_HEADS, FOX_HEAD_DIM),
            logf[None], ckv[None], kr[None],
            sfk.reshape(1, Bs, Ss, FOX_HEADS, FOX_HEAD_DIM), sfv.reshape(1, Bs, Ss, FOX_HEADS, FOX_HEAD_DIM),
            slogf[None], sckv[None], skr[None])
```

```python
import functools
import math

import numpy as np
import jax
import jax.numpy as jnp
from jax import lax
from jax.experimental import pallas as pl
from jax.experimental.pallas import tpu as pltpu

F32 = jnp.float32
BF16 = jnp.bfloat16

EPS = 1e-6
CHUNK = 64
ROPE_BASE = 10000.0

FOX_HEADS = 8
FOX_HEAD_DIM = 64
FOX_WIDTH = FOX_HEADS * FOX_HEAD_DIM
MLA_HEADS = 8
MLA_NOPE_DIM = 64
MLA_ROPE_DIM = 32
MLA_V_DIM = 64
MLA_Q_LORA = 256
MLA_KV_LORA = 128
ATTN_HEADS = FOX_HEADS + MLA_HEADS
HEAD_ROW = 128
V_DIM = 64

PEER_HEADS = 8
PEER_N_KEYS = 128
PEER_HALF = 128
PEER_TOPK = 16
N_RANK = PEER_TOPK + 1
RANK_ROWS = 24

Z_FQ = 0
Z_FK = Z_FQ + FOX_WIDTH
Z_FV = Z_FK + FOX_WIDTH
Z_CQ = Z_FV + FOX_WIDTH
Z_CKV = Z_CQ + MLA_Q_LORA
Z_TAIL = Z_CKV + MLA_KV_LORA
Z_WIDTH = Z_TAIL + 128
TAIL_KR = FOX_HEADS
TAIL_KRR = TAIL_KR + MLA_ROPE_DIM

NEG = -1e30
ATTN_HEAD_GROUP = 8
V7X_VMEM_LIMIT_BYTES = 56 * 1024 * 1024


def _params(sem, flags=None):
    return pltpu.CompilerParams(dimension_semantics=sem, vmem_limit_bytes=V7X_VMEM_LIMIT_BYTES, flags=flags)


def _rms(x, g):
    return x * lax.rsqrt(jnp.mean(x * x, axis=-1, keepdims=True) + EPS) * g


def _log_sigmoid(x):
    return jnp.minimum(x, 0.0) - jnp.log1p(jnp.exp(-jnp.abs(x)))


def _split3(x):
    hi = x.astype(BF16)
    r1 = x - hi.astype(F32)
    mid = r1.astype(BF16)
    lo = (r1 - mid.astype(F32)).astype(BF16)
    return hi, mid, lo


def _cumsum_rows(tri, x, carry):
    acc = carry
    for piece in _split3(x):
        acc = acc + jnp.dot(tri, piece, preferred_element_type=F32)
    return acc


def _fox_aux(c_col):
    hi, mid, lo = (p.astype(F32) for p in _split3(c_col))
    lane = lax.broadcasted_iota(jnp.int32, (1, HEAD_ROW - FOX_HEAD_DIM), 1)
    q_aux = jnp.where(lane == 0, hi, jnp.where(lane == 1, mid, jnp.where(
        lane == 2, lo, jnp.where(lane < 6, 1.0, 0.0))))
    k_aux = jnp.where(lane < 3, 1.0, jnp.where(lane == 3, -hi, jnp.where(
        lane == 4, -mid, jnp.where(lane == 5, -lo, 0.0))))
    return q_aux, k_aux


def _write_fox_kv(kh_ref, vh_ref, k, v, c_slab):
    for h in range(FOX_HEADS):
        sl = slice(h * FOX_HEAD_DIM, (h + 1) * FOX_HEAD_DIM)
        _, k_aux = _fox_aux(c_slab[:, h:h + 1])
        kh_ref[0, h, :, 0:FOX_HEAD_DIM] = k[:, sl].astype(BF16)
        kh_ref[0, h, :, FOX_HEAD_DIM:HEAD_ROW] = k_aux.astype(BF16)
        vh_ref[0, h] = v[:, sl].astype(BF16)


def _write_mla_kv(kh_ref, vh_ref, kv, krope):
    rows = kv.shape[0]
    k_rope = krope.astype(BF16)
    pad = jnp.zeros((rows, HEAD_ROW - MLA_NOPE_DIM - MLA_ROPE_DIM), BF16)
    for h in range(MLA_HEADS):
        sl = slice(h * MLA_NOPE_DIM, (h + 1) * MLA_NOPE_DIM)
        kh_ref[0, FOX_HEADS + h, :, 0:MLA_NOPE_DIM] = kv[:, sl].astype(BF16)
        kh_ref[0, FOX_HEADS + h, :, MLA_NOPE_DIM:MLA_NOPE_DIM + MLA_ROPE_DIM] = k_rope
        kh_ref[0, FOX_HEADS + h, :, MLA_NOPE_DIM + MLA_ROPE_DIM:HEAD_ROW] = pad
        vsl = slice(MLA_HEADS * MLA_NOPE_DIM + h * MLA_V_DIM, MLA_HEADS * MLA_NOPE_DIM + (h + 1) * MLA_V_DIM)
        vh_ref[0, FOX_HEADS + h] = kv[:, vsl].astype(BF16)


def _proj_body(x_ref, c0_ref, tri_ref, cosq_ref, sinq_ref, cosk_ref, sink_ref, gmix_ref, win_ref, bf_ref,
               gq_ref, wuq_ref, gkv_ref, wukv_ref,
               fk_ref, fv_ref, logf_ref, ckv_ref, kr_ref, qh_ref, kh_ref, vh_ref, carry_ref):
    @pl.when(pl.program_id(1) == 0)
    def _():
        carry_ref[...] = c0_ref[0]

    rows = x_ref.shape[1]
    h = _rms(x_ref[0], gmix_ref[...]).astype(BF16)
    z = jnp.dot(h, win_ref[...], preferred_element_type=F32)

    fk = z[:, Z_FK:Z_FV]
    fv = z[:, Z_FV:Z_CQ]
    fk_ref[0] = fk
    fv_ref[0] = fv

    tail = z[:, Z_TAIL:Z_WIDTH]
    logf = _log_sigmoid(tail + bf_ref[...])
    logf_ref[0] = logf[:, 0:FOX_HEADS]
    c_slab = _cumsum_rows(tri_ref[...], logf, carry_ref[...])
    carry_ref[...] = c_slab[rows - 1:rows, :]

    krope = (tail[:, TAIL_KR:TAIL_KR + MLA_ROPE_DIM] * cosk_ref[...]
             + tail[:, TAIL_KRR:TAIL_KRR + MLA_ROPE_DIM] * sink_ref[...])
    kr_ref[0] = krope

    cqn = _rms(z[:, Z_CQ:Z_CKV], gq_ref[...]).astype(BF16)
    q = jnp.dot(cqn, wuq_ref[...], preferred_element_type=F32)
    nope_w = MLA_HEADS * MLA_NOPE_DIM
    rope_w = MLA_HEADS * MLA_ROPE_DIM
    qrope = q[:, nope_w:nope_w + rope_w] * cosq_ref[...] + q[:, nope_w + rope_w:nope_w + 2 * rope_w] * sinq_ref[...]

    ckv = _rms(z[:, Z_CKV:Z_TAIL], gkv_ref[...])
    ckv_ref[0] = ckv
    kv = jnp.dot(ckv.astype(BF16), wukv_ref[...], preferred_element_type=F32)

    fox_scale = FOX_HEAD_DIM ** -0.5
    mla_scale = (MLA_NOPE_DIM + MLA_ROPE_DIM) ** -0.5
    qpad = jnp.zeros((rows, HEAD_ROW - MLA_NOPE_DIM - MLA_ROPE_DIM), BF16)
    for hd in range(FOX_HEADS):
        sl = slice(hd * FOX_HEAD_DIM, (hd + 1) * FOX_HEAD_DIM)
        q_aux, _ = _fox_aux(c_slab[:, hd:hd + 1])
        qh_ref[0, hd, :, 0:FOX_HEAD_DIM] = (z[:, sl] * fox_scale).astype(BF16)
        qh_ref[0, hd, :, FOX_HEAD_DIM:HEAD_ROW] = q_aux.astype(BF16)
    for hd in range(MLA_HEADS):
        sl = slice(hd * MLA_NOPE_DIM, (hd + 1) * MLA_NOPE_DIM)
        rsl = slice(hd * MLA_ROPE_DIM, (hd + 1) * MLA_ROPE_DIM)
        qh_ref[0, FOX_HEADS + hd, :, 0:MLA_NOPE_DIM] = (q[:, sl] * mla_scale).astype(BF16)
        qh_ref[0, FOX_HEADS + hd, :, MLA_NOPE_DIM:MLA_NOPE_DIM + MLA_ROPE_DIM] = (qrope[:, rsl] * mla_scale).astype(BF16)
        qh_ref[0, FOX_HEADS + hd, :, MLA_NOPE_DIM + MLA_ROPE_DIM:HEAD_ROW] = qpad
    _write_fox_kv(kh_ref, vh_ref, fk, fv, c_slab)
    _write_mla_kv(kh_ref, vh_ref, kv, krope)


def _proj(x, c0, tables, w, tm):
    B, S, D = x.shape
    cosq, sinq, cosk, sink = tables
    tri = jnp.tril(jnp.ones((tm, tm), F32)).astype(BF16)
    const = lambda shape: pl.BlockSpec(shape, lambda b, t: (0,) * len(shape))
    tab = lambda width: pl.BlockSpec((tm, width), lambda b, t: (t, 0))
    rope_w = MLA_HEADS * MLA_ROPE_DIM
    in_specs = [
        pl.BlockSpec((1, tm, D), lambda b, t: (b, t, 0)),
        pl.BlockSpec((1, 1, 128), lambda b, t: (b, 0, 0)),
        const((tm, tm)),
        tab(rope_w), tab(rope_w), tab(MLA_ROPE_DIM), tab(MLA_ROPE_DIM),
        const((1, D)), const((D, Z_WIDTH)), const((1, 128)),
        const((1, MLA_Q_LORA)), const(w["wuq"].shape), const((1, MLA_KV_LORA)), const(w["wukv"].shape),
    ]
    tok = lambda width: pl.BlockSpec((1, tm, width), lambda b, t: (b, t, 0))
    head = lambda width: pl.BlockSpec((1, ATTN_HEADS, tm, width), lambda b, t: (b, 0, t, 0))
    out_specs = [tok(FOX_WIDTH), tok(FOX_WIDTH), tok(FOX_HEADS), tok(MLA_KV_LORA), tok(MLA_ROPE_DIM),
                 head(HEAD_ROW), head(HEAD_ROW), head(V_DIM)]
    sds = jax.ShapeDtypeStruct
    out_shape = [sds((B, S, FOX_WIDTH), F32), sds((B, S, FOX_WIDTH), F32), sds((B, S, FOX_HEADS), F32),
                 sds((B, S, MLA_KV_LORA), F32), sds((B, S, MLA_ROPE_DIM), F32),
                 sds((B, ATTN_HEADS, S, HEAD_ROW), BF16), sds((B, ATTN_HEADS, S, HEAD_ROW), BF16),
                 sds((B, ATTN_HEADS, S, V_DIM), BF16)]
    return pl.pallas_call(
        _proj_body, grid=(B, S // tm), in_specs=in_specs, out_specs=out_specs, out_shape=out_shape,
        scratch_shapes=[pltpu.VMEM((1, 128), F32)],
        compiler_params=_params(("arbitrary", "arbitrary")), name="proj",
    )(x, c0, tri, cosq, sinq, cosk, sink, w["gmix"], w["win"], w["bf"], w["gq"], w["wuq"], w["gkv"], w["wukv"])


def _cache_body(ck_ref, cv_ref, clogf_ref, cckv_ref, ckr_ref, tri_ref, wukv_ref, kh_ref, vh_ref, clast_ref):
    rows = ck_ref.shape[1]
    c_slab = _cumsum_rows(tri_ref[...], clogf_ref[0], jnp.zeros((1, 128), F32))
    clast_ref[0] = c_slab[rows - 1:rows, :]
    _write_fox_kv(kh_ref, vh_ref, ck_ref[0], cv_ref[0], c_slab)
    kv = jnp.dot(cckv_ref[0].astype(BF16), wukv_ref[...], preferred_element_type=F32)
    _write_mla_kv(kh_ref, vh_ref, kv, ckr_ref[0])


def _cache_rows(ck, cv, clogf_slab, cckv, ckr, wukv):
    B, P, _ = ck.shape
    tri = jnp.tril(jnp.ones((P, P), F32)).astype(BF16)
    per_b = lambda width: pl.BlockSpec((1, P, width), lambda b: (b, 0, 0))
    const = lambda shape: pl.BlockSpec(shape, lambda b: (0,) * len(shape))
    head = lambda width: pl.BlockSpec((1, ATTN_HEADS, P, width), lambda b: (b, 0, 0, 0))
    sds = jax.ShapeDtypeStruct
    return pl.pallas_call(
        _cache_body, grid=(B,),
        in_specs=[per_b(FOX_WIDTH), per_b(FOX_WIDTH), per_b(128), per_b(MLA_KV_LORA), per_b(MLA_ROPE_DIM),
                  const((P, P)), const(wukv.shape)],
        out_specs=[head(HEAD_ROW), head(V_DIM), pl.BlockSpec((1, 1, 128), lambda b: (b, 0, 0))],
        out_shape=[sds((B, ATTN_HEADS, P, HEAD_ROW), BF16), sds((B, ATTN_HEADS, P, V_DIM), BF16),
                   sds((B, 1, 128), F32)],
        compiler_params=_params(("arbitrary",)), name="cache_rows",
    )(ck, cv, clogf_slab, cckv, ckr, tri, wukv)


def _reduce_rows(pair_op, final_op, x):
    while x.shape[0] > 8 and x.shape[0] % 16 == 0:
        half = x.shape[0] // 2
        x = pair_op(x[:half], x[half:])
    return final_op(x, axis=0, keepdims=True)


def _attn_body(q_ref, k_ref, v_ref, o_ref, *, hg, tq, tk, nkv_total, q_off, kv_len):
    grp = pl.program_id(1)
    qi = pl.program_id(2)
    q_lo = q_off + qi * tq
    qpos = q_lo + lax.broadcasted_iota(jnp.int32, (1, tq), 1)
    lim = jnp.where(grp >= FOX_HEADS // hg, qpos | (CHUNK - 1), qpos)
    lim = jnp.minimum(lim, kv_len - 1)
    n_full = jnp.minimum(lax.div(q_lo + 1, tk), nkv_total)
    n_all = jnp.minimum(lax.div((q_lo + tq - 1) | (CHUNK - 1), tk) + 1, nkv_total)

    def step(j, carry, masked):
        off = pl.multiple_of(j * tk, tk)
        if masked:
            visible = off + lax.broadcasted_iota(jnp.int32, (tk, 1), 0) <= lim
        out = []
        scores = [lax.dot_general(k_ref[0, hh, pl.ds(off, tk), :], q_ref[0, hh], (((1,), (1,)), ((), ())),
                                  preferred_element_type=F32) for hh in range(hg)]
        for hh in range(hg):
            m, l, acc = carry[3 * hh:3 * hh + 3]
            vb = v_ref[0, hh, pl.ds(off, tk), :]
            s = scores[hh]
            if masked:
                s = jnp.where(visible, s, NEG)
            m_new = jnp.maximum(m, _reduce_rows(jnp.maximum, jnp.max, s))
            alpha = jnp.exp(m - m_new)
            p = jnp.exp(s - m_new)
            l = alpha * l + _reduce_rows(jnp.add, jnp.sum, p)
            pv = lax.dot_general(vb, p.astype(BF16), (((0,), (0,)), ((), ())), preferred_element_type=F32)
            out += [m_new, l, alpha * acc + pv]
        return tuple(out)

    init = (jnp.full((1, tq), NEG, F32), jnp.zeros((1, tq), F32), jnp.zeros((V_DIM, tq), F32)) * hg
    carry = lax.fori_loop(0, n_full, functools.partial(step, masked=False), init)
    carry = lax.fori_loop(n_full, n_all, functools.partial(step, masked=True), carry)
    o_ref[0] = jnp.concatenate([carry[3 * hh + 2] / carry[3 * hh + 1] for hh in range(hg)], axis=0).astype(BF16)


def _attention(qh, kh, vh, hg, tq, tk, q_off, kv_len):
    B, _, Tq, _ = qh.shape
    Tk = kh.shape[2]
    body = functools.partial(_attn_body, hg=hg, tq=tq, tk=tk, nkv_total=Tk // tk, q_off=q_off, kv_len=kv_len)
    return pl.pallas_call(
        body, grid=(B, ATTN_HEADS // hg, Tq // tq),
        in_specs=[pl.BlockSpec((1, hg, tq, HEAD_ROW), lambda b, g, i: (b, g, i, 0)),
                  pl.BlockSpec((1, hg, Tk, HEAD_ROW), lambda b, g, i: (b, g, 0, 0)),
                  pl.BlockSpec((1, hg, Tk, V_DIM), lambda b, g, i: (b, g, 0, 0))],
        out_specs=pl.BlockSpec((1, hg * V_DIM, tq), lambda b, g, i: (b, g, i)),
        out_shape=jax.ShapeDtypeStruct((B, ATTN_HEADS * V_DIM, Tq), BF16),
        compiler_params=_params(("arbitrary", "arbitrary", "arbitrary")), name="attn",
    )(qh, kh, vh)


def _sorting_network(n):
    size = 1 << (n - 1).bit_length()

    def merge(lo, hi, r):
        step = r * 2
        if step < hi - lo:
            yield from merge(lo, hi, step)
            yield from merge(lo + r, hi, step)
            yield from ((i, i + r) for i in range(lo + r, hi - r, step))
        else:
            yield (lo, lo + r)

    def sort(lo, hi):
        if hi - lo >= 1:
            mid = lo + (hi - lo) // 2
            yield from sort(lo, mid)
            yield from sort(mid + 1, hi)
            yield from merge(lo, hi, 1)

    return tuple((i, j) for i, j in sort(0, size - 1) if j < n)


def _largest(groups, n_out):
    g = list(groups)
    for i, j in _sorting_network(len(g)):
        g[i], g[j] = jnp.maximum(g[i], g[j]), jnp.minimum(g[i], g[j])
    below = jnp.full(g[0].shape, -jnp.inf, F32)
    out = []
    for r in range(n_out):
        m = jnp.max(g[0], axis=0, keepdims=True)
        out.append(m)
        hit = g[0] == m
        for k in range(min(n_out - 1 - r, len(g))):
            g[k] = jnp.where(hit, g[k + 1] if k + 1 < len(g) else below, g[k])
    return out


def _top_ranks(work, rows_ref, want_rank):
    rows_ref[...] = jnp.full(rows_ref.shape, -jnp.inf, F32)
    for r, m in enumerate(_largest([work[8 * k:8 * (k + 1), :] for k in range(work.shape[0] // 8)], N_RANK)):
        rows_ref[r:r + 1, :] = m
    if not want_rank:
        return None
    rank = jnp.zeros(work.shape, F32)
    for r in range(N_RANK):
        rank = rank + jnp.where(rows_ref[r:r + 1, :] > work, 1.0, 0.0)
    return rank


def _route_body(x_ref, o_ref, wo_ref, gffn_ref, wpq_ref, k1_ref, k2_ref,
                x1_ref, h2_ref, rank_ref, e2_ref, cnt_ref, w1_ref, qp_ref, a0_ref, b0_ref, a1_ref, b1_ref):
    x1 = x_ref[...] + lax.dot_general(o_ref[0], wo_ref[...], (((0,), (0,)), ((), ())), preferred_element_type=F32)
    x1_ref[...] = x1
    h2 = _rms(x1, gffn_ref[...]).astype(BF16)
    h2_ref[...] = h2
    qp_ref[...] = jnp.dot(h2, wpq_ref[...], preferred_element_type=F32)
    key_dim = 2 * PEER_HALF

    def head(h, a_ref, b_ref):
        qh = qp_ref[:, pl.ds(pl.multiple_of(h * key_dim, key_dim), key_dim)]
        qn = qh * lax.rsqrt(jnp.mean(qh * qh, axis=-1, keepdims=True) + EPS)
        nt = (((1,), (1,)), ((), ()))
        s1 = lax.dot_general(k1_ref[h], qn[:, :PEER_HALF].astype(BF16), nt, preferred_element_type=F32)
        s2 = lax.dot_general(k2_ref[h], qn[:, PEER_HALF:].astype(BF16), nt, preferred_element_type=F32)
        _top_ranks(s1, a_ref, False)
        rank = _top_ranks(s2, b_ref, True)
        groups = [a_ref[0:1, :] + b_ref[8 * k:8 * (k + 1), :] for k in range(RANK_ROWS // 8)]
        groups += [a_ref[r:r + 1, :] + b_ref[0:8, :] for r in range(1, 8)]
        groups += [a_ref[8 * k:8 * (k + 1), :] + b_ref[0:1, :] for k in range(1, RANK_ROWS // 8)]
        pair_scores = _largest(groups, N_RANK)
        top = pair_scores[0]
        zsum = jnp.zeros_like(top)
        for v in pair_scores[:PEER_TOPK]:
            zsum = zsum + jnp.exp(v - top)
        tau = 0.5 * (pair_scores[PEER_TOPK - 1] + pair_scores[PEER_TOPK])
        thr = tau - s1
        cnt = jnp.zeros_like(s1)
        for c in range(N_RANK):
            cnt = cnt + jnp.where(b_ref[c:c + 1, :] >= thr, 1.0, 0.0)
        rank_ref[0, h] = rank.astype(BF16)
        e2_ref[0, h] = jnp.exp(s2 - b_ref[0:1, :]).astype(BF16)
        cnt_ref[0, h] = cnt
        w1_ref[0, h] = jnp.exp(s1 - a_ref[0:1, :]) / zsum

    def head_pair(p, carry):
        head(2 * p, a0_ref, b0_ref)
        head(2 * p + 1, a1_ref, b1_ref)
        return carry

    lax.fori_loop(0, PEER_HEADS // 2, head_pair, 0)


def _route(x2d, o_t, w, tm, tt):
    T, D = x2d.shape
    per_seq = o_t.shape[2] // tm
    r = tt // tm
    kq = w["wpq"].shape[1]
    const = lambda shape: pl.BlockSpec(shape, lambda i: (0,) * len(shape))
    lane_out = pl.BlockSpec((1, PEER_HEADS, PEER_N_KEYS, tm), lambda i: (i // r, 0, 0, i % r))
    sds = jax.ShapeDtypeStruct
    lane_shape = lambda dtype: sds((T // tt, PEER_HEADS, PEER_N_KEYS, tt), dtype)
    return pl.pallas_call(
        _route_body, grid=(T // tm,),
        in_specs=[pl.BlockSpec((tm, D), lambda i: (i, 0)),
                  pl.BlockSpec((1, o_t.shape[1], tm), lambda i: (i // per_seq, 0, i % per_seq)),
                  const((D, D)), const((1, D)), const((D, kq)),
                  const((PEER_HEADS, PEER_N_KEYS, PEER_HALF)), const((PEER_HEADS, PEER_N_KEYS, PEER_HALF))],
        out_specs=[pl.BlockSpec((tm, D), lambda i: (i, 0)), pl.BlockSpec((tm, D), lambda i: (i, 0)),
                   lane_out, lane_out, lane_out, lane_out],
        out_shape=[sds((T, D), F32), sds((T, D), BF16),
                   lane_shape(BF16), lane_shape(BF16), lane_shape(F32), lane_shape(F32)],
        scratch_shapes=[pltpu.VMEM((tm, kq), F32)] + [pltpu.VMEM((RANK_ROWS, tm), F32)] * 4,
        compiler_params=_params(("arbitrary",)), name="route",
    )(x2d, o_t, w["wo"], w["gffn"], w["wpq"], w["k1"], w["k2"])


GATE_ROWS = 32
GATE_LANES = 256
CHUNK_I = 2


def _row_tile(ref, h, i, ls):
    tile = jnp.broadcast_to(ref[0, h, i:i + 1, ls], (8, GATE_LANES)).astype(BF16)
    return jnp.concatenate([tile] * (GATE_ROWS // 8), axis=0)


def _peer_body(x1_ref, h2_ref, rank_ref, e2_ref, cnt_ref, w1_ref, u_ref, v_ref, gfin_ref, y_ref,
               acc_ref, *chunk_refs, tt, rows_per_step):
    j = pl.program_id(1)

    @pl.when(j == 0)
    def _():
        acc_ref[...] = jnp.zeros_like(acc_ref)

    n_chunks = len(chunk_refs) // 2
    a_refs, act_refs = chunk_refs[:n_chunks], chunk_refs[n_chunks:]
    inv_sqrt2 = 1.0 / math.sqrt(2.0)
    rows = CHUNK_I * PEER_N_KEYS

    def score_matmul(c):
        cs = slice(c * rows, (c + 1) * rows)
        a_refs[c][...] = lax.dot_general(u_ref[cs, :], h2_ref[...], (((1,), (1,)), ((), ())),
                                         preferred_element_type=F32)

    score_matmul(0)
    score_matmul(1)
    for c, act_ref in enumerate(act_refs):
        if c + 2 < n_chunks:
            score_matmul(c + 2)
        cs = slice(c * rows, (c + 1) * rows)
        a = a_refs[c]
        for k in range(CHUNK_I):
            i = c * CHUNK_I + k
            for tc in range(tt // GATE_LANES):
                ls = slice(tc * GATE_LANES, (tc + 1) * GATE_LANES)
                cnt = [_row_tile(cnt_ref, h, i, ls) for h in range(PEER_HEADS)]
                w1 = [_row_tile(w1_ref, h, i, ls) for h in range(PEER_HEADS)]
                for jb in range(PEER_N_KEYS // GATE_ROWS):
                    js = slice(jb * GATE_ROWS, (jb + 1) * GATE_ROWS)
                    gate = jnp.zeros((GATE_ROWS, GATE_LANES), BF16)
                    for h in range(PEER_HEADS):
                        sel = rank_ref[0, h, js, ls] < cnt[h]
                        gate = gate + jnp.where(sel, e2_ref[0, h, js, ls] * w1[h], 0.0)
                    rs = slice(k * PEER_N_KEYS + jb * GATE_ROWS, k * PEER_N_KEYS + (jb + 1) * GATE_ROWS)
                    av = a[rs, ls]
                    gelu = 0.5 * av * (1.0 + lax.erf(av * inv_sqrt2))
                    act_ref[rs, ls] = gelu.astype(BF16) * gate
        acc_ref[...] += lax.dot_general(act_ref[...], v_ref[cs, :], (((0,), (0,)), ((), ())),
                                        preferred_element_type=F32)

    @pl.when(j == pl.num_programs(1) - 1)
    def _():
        y_ref[...] = _rms(x1_ref[...] + acc_ref[...], gfin_ref[...])


def _peer(x1, h2, rank, e2, cnt, w1, w, tt, rows_per_step):
    T, D = x1.shape
    n_exp = w["v"].shape[0]
    eb = rows_per_step * PEER_N_KEYS
    lane_in = pl.BlockSpec((1, PEER_HEADS, PEER_N_KEYS, tt), lambda t, j: (t, 0, 0, 0))
    row_in = pl.BlockSpec((1, PEER_HEADS, rows_per_step, tt), lambda t, j: (t, 0, j, 0))
    body = functools.partial(_peer_body, tt=tt, rows_per_step=rows_per_step)
    n_chunks = rows_per_step // CHUNK_I
    a_chunks = [pltpu.VMEM((CHUNK_I * PEER_N_KEYS, tt), F32)] * n_chunks
    act_chunks = [pltpu.VMEM((CHUNK_I * PEER_N_KEYS, tt), BF16)] * n_chunks
    return pl.pallas_call(
        body, grid=(T // tt, n_exp // eb),
        in_specs=[pl.BlockSpec((tt, D), lambda t, j: (t, 0)), pl.BlockSpec((tt, D), lambda t, j: (t, 0)),
                  lane_in, lane_in, row_in, row_in,
                  pl.BlockSpec((eb, D), lambda t, j: (j, 0)), pl.BlockSpec((eb, D), lambda t, j: (j, 0)),
                  pl.BlockSpec((1, D), lambda t, j: (0, 0))],
        out_specs=pl.BlockSpec((tt, D), lambda t, j: (t, 0)),
        out_shape=jax.ShapeDtypeStruct((T, D), F32),
        scratch_shapes=[pltpu.VMEM((tt, D), F32)] + a_chunks + act_chunks,
        compiler_params=_params(("arbitrary", "arbitrary")), name="peer",
    )(x1, h2, rank, e2, cnt, w1, w["u"], w["v"], w["gfin"])


def _rope_tables(pos):
    half = MLA_ROPE_DIM // 2
    inv = ROPE_BASE ** (-jnp.arange(half, dtype=F32) / half)
    ang = pos.astype(F32)[:, None] * inv[None, :]
    cos = jnp.concatenate([jnp.cos(ang)] * 2, axis=-1)
    sin = jnp.concatenate([jnp.sin(ang)] * 2, axis=-1)
    return jnp.tile(cos, (1, MLA_HEADS)), jnp.tile(sin, (1, MLA_HEADS)), cos, sin


def _rot_cols(wcols):
    half = wcols.shape[-1] // 2
    return jnp.concatenate([-wcols[..., half:], wcols[..., :half]], axis=-1)


def _prep_weights(g_mix, w_in, b_f, g_q, w_uq, g_kv, w_ukv, w_o, g_ffn, w_pq, sub_keys, peer_u, peer_v, g_final):
    D = w_in.shape[0]
    off_ff = 3 * FOX_WIDTH
    off_cq = off_ff + FOX_HEADS
    off_ckv = off_cq + MLA_Q_LORA
    off_kr = off_ckv + MLA_KV_LORA
    kr_cols = w_in[:, off_kr:off_kr + MLA_ROPE_DIM]
    tail_pad = jnp.zeros((D, 128 - FOX_HEADS - 2 * MLA_ROPE_DIM), F32)
    win = jnp.concatenate([w_in[:, :off_ff], w_in[:, off_cq:off_kr], w_in[:, off_ff:off_cq],
                           kr_cols, _rot_cols(kr_cols), tail_pad], axis=1).astype(BF16)
    bf = jnp.pad(b_f, (0, 128 - FOX_HEADS)).reshape(1, 128)
    per_q = MLA_NOPE_DIM + MLA_ROPE_DIM
    wq = w_uq.reshape(MLA_Q_LORA, MLA_HEADS, per_q)
    q_rope = wq[:, :, MLA_NOPE_DIM:]
    wuq = jnp.concatenate([wq[:, :, :MLA_NOPE_DIM].reshape(MLA_Q_LORA, -1), q_rope.reshape(MLA_Q_LORA, -1),
                           _rot_cols(q_rope).reshape(MLA_Q_LORA, -1)], axis=1).astype(BF16)
    wkv = w_ukv.reshape(MLA_KV_LORA, MLA_HEADS, MLA_NOPE_DIM + MLA_V_DIM)
    wukv = jnp.concatenate([wkv[:, :, :MLA_NOPE_DIM].reshape(MLA_KV_LORA, -1),
                            wkv[:, :, MLA_NOPE_DIM:].reshape(MLA_KV_LORA, -1)], axis=1).astype(BF16)
    return dict(
        gmix=g_mix.reshape(1, -1), win=win, bf=bf, gq=g_q.reshape(1, -1), wuq=wuq, gkv=g_kv.reshape(1, -1),
        wukv=wukv, wo=w_o.astype(BF16), gffn=g_ffn.reshape(1, -1), wpq=w_pq.astype(BF16),
        k1=sub_keys[0].astype(BF16), k2=sub_keys[1].astype(BF16),
        u=peer_u.astype(BF16), v=peer_v.astype(BF16), gfin=g_final.reshape(1, -1))


def _ffn(x, o, w):
    B, S, D = x.shape
    T = B * S
    tm = min(256, T)
    tt = min(512, T)
    if S % tm:
        o = jnp.transpose(o, (1, 0, 2)).reshape(1, o.shape[1], T)
    x1, h2, rank, e2, cnt, w1 = _route(x.reshape(T, D), o, w, tm, tt)
    return _peer(x1, h2, rank, e2, cnt, w1, w, tt, 8).reshape(B, S, D)


def kernel(x_prompt, x_sample, cache_fox_k, cache_fox_v, cache_fox_logf, cache_mla_ckv, cache_mla_krope, g_mix, w_in, b_f, g_q, w_uq, g_kv, w_ukv, w_o, g_ffn, w_pq, peer_sub_keys, peer_u, peer_v, g_final):
    assert g_mix.shape[0] == 1, "single-layer model"
    w = _prep_weights(g_mix[0], w_in[0], b_f[0], g_q[0], w_uq[0], g_kv[0], w_ukv[0], w_o[0], g_ffn[0], w_pq[0],
                      peer_sub_keys[0], peer_u[0], peer_v[0], g_final)
    B, S, D = x_prompt.shape
    Bs, Ss, _ = x_sample.shape
    P = cache_fox_k.shape[2]

    tables_p = _rope_tables(jnp.arange(S, dtype=jnp.int32))
    tm = min(512, S)
    fk, fv, logf, ckv, kr, qh, kh, vh = _proj(x_prompt, jnp.zeros((B, 1, 128), F32), tables_p, w, tm)
    tq = min(256, S)
    o_p = _attention(qh, kh, vh, ATTN_HEAD_GROUP, tq, tq, 0, S)
    y_p = _ffn(x_prompt, o_p, w)

    kh_c, vh_c, c_last = _cache_rows(
        cache_fox_k[0].reshape(Bs, P, FOX_WIDTH), cache_fox_v[0].reshape(Bs, P, FOX_WIDTH),
        jnp.pad(cache_fox_logf[0], ((0, 0), (0, 0), (0, 128 - FOX_HEADS))),
        cache_mla_ckv[0], cache_mla_krope[0], w["wukv"])
    tables_s = _rope_tables(P + jnp.arange(Ss, dtype=jnp.int32))
    sfk, sfv, slogf, sckv, skr, sqh, skh, svh = _proj(x_sample, c_last, tables_s, w, Ss)
    tk = 256
    L = P + Ss
    pad = (-L) % tk
    kh_all = jnp.pad(jnp.concatenate([kh_c, skh], axis=2), ((0, 0), (0, 0), (0, pad), (0, 0)))
    vh_all = jnp.pad(jnp.concatenate([vh_c, svh], axis=2), ((0, 0), (0, 0), (0, pad), (0, 0)))
    o_s = _attention(sqh, kh_all, vh_all, ATTN_HEAD_GROUP, Ss, tk, P, L)
    y_s = _ffn(x_sample, o_s, w)

    return (y_p, y_s,
            fk.reshape(1, B, S, FOX_HEADS, FOX_HEAD_DIM), fv.reshape(1, B, S, FOX_HEADS, FOX_HEAD_DIM),
            logf[None], ckv[None], kr[None],
            sfk.reshape(1, Bs, Ss, FOX_HEADS, FOX_HEAD_DIM), sfv.reshape(1, Bs, Ss, FOX_HEADS, FOX_HEAD_DIM),
            slogf[None], sckv[None], skr[None])
```

```python
import functools
import math

import numpy as np
import jax
import jax.numpy as jnp
from jax import lax
from jax.experimental import pallas as pl
from jax.experimental.pallas import tpu as pltpu

F32 = jnp.float32
BF16 = jnp.bfloat16

EPS = 1e-6
CHUNK = 64
ROPE_BASE = 10000.0

FOX_HEADS = 8
FOX_HEAD_DIM = 64
FOX_WIDTH = FOX_HEADS * FOX_HEAD_DIM
MLA_HEADS = 8
MLA_NOPE_DIM = 64
MLA_ROPE_DIM = 32
MLA_V_DIM = 64
MLA_Q_LORA = 256
MLA_KV_LORA = 128
ATTN_HEADS = FOX_HEADS + MLA_HEADS
HEAD_ROW = 128
V_DIM = 64

PEER_HEADS = 8
PEER_N_KEYS = 128
PEER_HALF = 128
PEER_TOPK = 16
N_RANK = PEER_TOPK + 1
RANK_ROWS = 24

Z_FQ = 0
Z_FK = Z_FQ + FOX_WIDTH
Z_FV = Z_FK + FOX_WIDTH
Z_CQ = Z_FV + FOX_WIDTH
Z_CKV = Z_CQ + MLA_Q_LORA
Z_TAIL = Z_CKV + MLA_KV_LORA
Z_WIDTH = Z_TAIL + 128
TAIL_KR = FOX_HEADS
TAIL_KRR = TAIL_KR + MLA_ROPE_DIM

NEG = -1e30
LOG2E = math.log2(math.e)
ATTN_HEAD_GROUP = 8
V7X_VMEM_LIMIT_BYTES = 56 * 1024 * 1024


def _params(sem, flags=None):
    return pltpu.CompilerParams(dimension_semantics=sem, vmem_limit_bytes=V7X_VMEM_LIMIT_BYTES, flags=flags)


def _rms(x, g):
    return x * lax.rsqrt(jnp.mean(x * x, axis=-1, keepdims=True) + EPS) * g


def _log_sigmoid(x):
    return jnp.minimum(x, 0.0) - jnp.log1p(jnp.exp(-jnp.abs(x)))


def _split3(x):
    hi = x.astype(BF16)
    r1 = x - hi.astype(F32)
    mid = r1.astype(BF16)
    lo = (r1 - mid.astype(F32)).astype(BF16)
    return hi, mid, lo


def _cumsum_rows(tri, x, carry):
    acc = carry
    for piece in _split3(x):
        acc = acc + jnp.dot(tri, piece, preferred_element_type=F32)
    return acc


def _fox_aux(c_col):
    hi, mid, lo = (p.astype(F32) for p in _split3(c_col * LOG2E))
    lane = lax.broadcasted_iota(jnp.int32, (1, HEAD_ROW - FOX_HEAD_DIM), 1)
    q_aux = jnp.where(lane == 0, hi, jnp.where(lane == 1, mid, jnp.where(
        lane == 2, lo, jnp.where(lane < 6, 1.0, 0.0))))
    k_aux = jnp.where(lane < 3, 1.0, jnp.where(lane == 3, -hi, jnp.where(
        lane == 4, -mid, jnp.where(lane == 5, -lo, 0.0))))
    return q_aux, k_aux


def _write_fox_kv(kh_ref, vh_ref, k, v, c_slab):
    for h in range(FOX_HEADS):
        sl = slice(h * FOX_HEAD_DIM, (h + 1) * FOX_HEAD_DIM)
        _, k_aux = _fox_aux(c_slab[:, h:h + 1])
        kh_ref[0, h, :, 0:FOX_HEAD_DIM] = k[:, sl].astype(BF16)
        kh_ref[0, h, :, FOX_HEAD_DIM:HEAD_ROW] = k_aux.astype(BF16)
        vh_ref[0, h] = v[:, sl].astype(BF16)


def _write_mla_kv(kh_ref, vh_ref, kv, krope):
    rows = kv.shape[0]
    k_rope = krope.astype(BF16)
    pad = jnp.zeros((rows, HEAD_ROW - MLA_NOPE_DIM - MLA_ROPE_DIM), BF16)
    for h in range(MLA_HEADS):
        sl = slice(h * MLA_NOPE_DIM, (h + 1) * MLA_NOPE_DIM)
        kh_ref[0, FOX_HEADS + h, :, 0:MLA_NOPE_DIM] = kv[:, sl].astype(BF16)
        kh_ref[0, FOX_HEADS + h, :, MLA_NOPE_DIM:MLA_NOPE_DIM + MLA_ROPE_DIM] = k_rope
        kh_ref[0, FOX_HEADS + h, :, MLA_NOPE_DIM + MLA_ROPE_DIM:HEAD_ROW] = pad
        vsl = slice(MLA_HEADS * MLA_NOPE_DIM + h * MLA_V_DIM, MLA_HEADS * MLA_NOPE_DIM + (h + 1) * MLA_V_DIM)
        vh_ref[0, FOX_HEADS + h] = kv[:, vsl].astype(BF16)


def _proj_body(x_ref, c0_ref, tri_ref, cosq_ref, sinq_ref, cosk_ref, sink_ref, gmix_ref, win_ref, bf_ref,
               gq_ref, wuq_ref, gkv_ref, wukv_ref,
               fk_ref, fv_ref, logf_ref, ckv_ref, kr_ref, qh_ref, kh_ref, vh_ref, carry_ref):
    @pl.when(pl.program_id(1) == 0)
    def _():
        carry_ref[...] = c0_ref[0]

    rows = x_ref.shape[1]
    h = _rms(x_ref[0], gmix_ref[...]).astype(BF16)
    z = jnp.dot(h, win_ref[...], preferred_element_type=F32)

    fk = z[:, Z_FK:Z_FV]
    fv = z[:, Z_FV:Z_CQ]
    fk_ref[0] = fk
    fv_ref[0] = fv

    tail = z[:, Z_TAIL:Z_WIDTH]
    logf = _log_sigmoid(tail + bf_ref[...])
    logf_ref[0] = logf[:, 0:FOX_HEADS]
    c_slab = _cumsum_rows(tri_ref[...], logf, carry_ref[...])
    carry_ref[...] = c_slab[rows - 1:rows, :]

    krope = (tail[:, TAIL_KR:TAIL_KR + MLA_ROPE_DIM] * cosk_ref[...]
             + tail[:, TAIL_KRR:TAIL_KRR + MLA_ROPE_DIM] * sink_ref[...])
    kr_ref[0] = krope

    cqn = _rms(z[:, Z_CQ:Z_CKV], gq_ref[...]).astype(BF16)
    q = jnp.dot(cqn, wuq_ref[...], preferred_element_type=F32)
    nope_w = MLA_HEADS * MLA_NOPE_DIM
    rope_w = MLA_HEADS * MLA_ROPE_DIM
    qrope = q[:, nope_w:nope_w + rope_w] * cosq_ref[...] + q[:, nope_w + rope_w:nope_w + 2 * rope_w] * sinq_ref[...]

    ckv = _rms(z[:, Z_CKV:Z_TAIL], gkv_ref[...])
    ckv_ref[0] = ckv
    kv = jnp.dot(ckv.astype(BF16), wukv_ref[...], preferred_element_type=F32)

    fox_scale = FOX_HEAD_DIM ** -0.5 * LOG2E
    mla_scale = (MLA_NOPE_DIM + MLA_ROPE_DIM) ** -0.5 * LOG2E
    qpad = jnp.zeros((rows, HEAD_ROW - MLA_NOPE_DIM - MLA_ROPE_DIM), BF16)
    for hd in range(FOX_HEADS):
        sl = slice(hd * FOX_HEAD_DIM, (hd + 1) * FOX_HEAD_DIM)
        q_aux, _ = _fox_aux(c_slab[:, hd:hd + 1])
        qh_ref[0, hd, :, 0:FOX_HEAD_DIM] = (z[:, sl] * fox_scale).astype(BF16)
        qh_ref[0, hd, :, FOX_HEAD_DIM:HEAD_ROW] = q_aux.astype(BF16)
    for hd in range(MLA_HEADS):
        sl = slice(hd * MLA_NOPE_DIM, (hd + 1) * MLA_NOPE_DIM)
        rsl = slice(hd * MLA_ROPE_DIM, (hd + 1) * MLA_ROPE_DIM)
        qh_ref[0, FOX_HEADS + hd, :, 0:MLA_NOPE_DIM] = (q[:, sl] * mla_scale).astype(BF16)
        qh_ref[0, FOX_HEADS + hd, :, MLA_NOPE_DIM:MLA_NOPE_DIM + MLA_ROPE_DIM] = (qrope[:, rsl] * mla_scale).astype(BF16)
        qh_ref[0, FOX_HEADS + hd, :, MLA_NOPE_DIM + MLA_ROPE_DIM:HEAD_ROW] = qpad
    _write_fox_kv(kh_ref, vh_ref, fk, fv, c_slab)
    _write_mla_kv(kh_ref, vh_ref, kv, krope)


def _proj(x, c0, tables, w, tm):
    B, S, D = x.shape
    cosq, sinq, cosk, sink = tables
    tri = jnp.tril(jnp.ones((tm, tm), F32)).astype(BF16)
    const = lambda shape: pl.BlockSpec(shape, lambda b, t: (0,) * len(shape))
    tab = lambda width: pl.BlockSpec((tm, width), lambda b, t: (t, 0))
    rope_w = MLA_HEADS * MLA_ROPE_DIM
    in_specs = [
        pl.BlockSpec((1, tm, D), lambda b, t: (b, t, 0)),
        pl.BlockSpec((1, 1, 128), lambda b, t: (b, 0, 0)),
        const((tm, tm)),
        tab(rope_w), tab(rope_w), tab(MLA_ROPE_DIM), tab(MLA_ROPE_DIM),
        const((1, D)), const((D, Z_WIDTH)), const((1, 128)),
        const((1, MLA_Q_LORA)), const(w["wuq"].shape), const((1, MLA_KV_LORA)), const(w["wukv"].shape),
    ]
    tok = lambda width: pl.BlockSpec((1, tm, width), lambda b, t: (b, t, 0))
    head = lambda width: pl.BlockSpec((1, ATTN_HEADS, tm, width), lambda b, t: (b, 0, t, 0))
    out_specs = [tok(FOX_WIDTH), tok(FOX_WIDTH), tok(FOX_HEADS), tok(MLA_KV_LORA), tok(MLA_ROPE_DIM),
                 head(HEAD_ROW), head(HEAD_ROW), head(V_DIM)]
    sds = jax.ShapeDtypeStruct
    out_shape = [sds((B, S, FOX_WIDTH), F32), sds((B, S, FOX_WIDTH), F32), sds((B, S, FOX_HEADS), F32),
                 sds((B, S, MLA_KV_LORA), F32), sds((B, S, MLA_ROPE_DIM), F32),
                 sds((B, ATTN_HEADS, S, HEAD_ROW), BF16), sds((B, ATTN_HEADS, S, HEAD_ROW), BF16),
                 sds((B, ATTN_HEADS, S, V_DIM), BF16)]
    return pl.pallas_call(
        _proj_body, grid=(B, S // tm), in_specs=in_specs, out_specs=out_specs, out_shape=out_shape,
        scratch_shapes=[pltpu.VMEM((1, 128), F32)],
        compiler_params=_params(("arbitrary", "arbitrary")), name="proj",
    )(x, c0, tri, cosq, sinq, cosk, sink, w["gmix"], w["win"], w["bf"], w["gq"], w["wuq"], w["gkv"], w["wukv"])


def _cache_body(ck_ref, cv_ref, clogf_ref, cckv_ref, ckr_ref, tri_ref, wukv_ref, kh_ref, vh_ref, clast_ref):
    rows = ck_ref.shape[1]
    c_slab = _cumsum_rows(tri_ref[...], clogf_ref[0], jnp.zeros((1, 128), F32))
    clast_ref[0] = c_slab[rows - 1:rows, :]
    _write_fox_kv(kh_ref, vh_ref, ck_ref[0], cv_ref[0], c_slab)
    kv = jnp.dot(cckv_ref[0].astype(BF16), wukv_ref[...], preferred_element_type=F32)
    _write_mla_kv(kh_ref, vh_ref, kv, ckr_ref[0])


def _cache_rows(ck, cv, clogf_slab, cckv, ckr, wukv):
    B, P, _ = ck.shape
    tri = jnp.tril(jnp.ones((P, P), F32)).astype(BF16)
    per_b = lambda width: pl.BlockSpec((1, P, width), lambda b: (b, 0, 0))
    const = lambda shape: pl.BlockSpec(shape, lambda b: (0,) * len(shape))
    head = lambda width: pl.BlockSpec((1, ATTN_HEADS, P, width), lambda b: (b, 0, 0, 0))
    sds = jax.ShapeDtypeStruct
    return pl.pallas_call(
        _cache_body, grid=(B,),
        in_specs=[per_b(FOX_WIDTH), per_b(FOX_WIDTH), per_b(128), per_b(MLA_KV_LORA), per_b(MLA_ROPE_DIM),
                  const((P, P)), const(wukv.shape)],
        out_specs=[head(HEAD_ROW), head(V_DIM), pl.BlockSpec((1, 1, 128), lambda b: (b, 0, 0))],
        out_shape=[sds((B, ATTN_HEADS, P, HEAD_ROW), BF16), sds((B, ATTN_HEADS, P, V_DIM), BF16),
                   sds((B, 1, 128), F32)],
        compiler_params=_params(("arbitrary",)), name="cache_rows",
    )(ck, cv, clogf_slab, cckv, ckr, tri, wukv)


def _reduce_rows(pair_op, final_op, x):
    while x.shape[0] > 8 and x.shape[0] % 16 == 0:
        half = x.shape[0] // 2
        x = pair_op(x[:half], x[half:])
    return final_op(x, axis=0, keepdims=True)


def _attn_body(q_ref, k_ref, v_ref, o_ref, *, hg, tq, tk, nkv_total, q_off, kv_len):
    grp = pl.program_id(1)
    qi = pl.program_id(2)
    q_lo = q_off + qi * tq
    qpos = q_lo + lax.broadcasted_iota(jnp.int32, (1, tq), 1)
    lim = jnp.where(grp >= FOX_HEADS // hg, qpos | (CHUNK - 1), qpos)
    lim = jnp.minimum(lim, kv_len - 1)
    n_full = jnp.minimum(lax.div(q_lo + 1, tk), nkv_total)
    n_all = jnp.minimum(lax.div((q_lo + tq - 1) | (CHUNK - 1), tk) + 1, nkv_total)

    def step(j, carry, masked):
        off = pl.multiple_of(j * tk, tk)
        if masked:
            visible = off + lax.broadcasted_iota(jnp.int32, (tk, 1), 0) <= lim
        out = []
        scores = [lax.dot_general(k_ref[0, hh, pl.ds(off, tk), :], q_ref[0, hh], (((1,), (1,)), ((), ())),
                                  preferred_element_type=F32) for hh in range(hg)]
        for hh in range(hg):
            m, l, acc = carry[3 * hh:3 * hh + 3]
            vb = v_ref[0, hh, pl.ds(off, tk), :]
            s = scores[hh]
            if masked:
                s = jnp.where(visible, s, NEG)
            m_new = jnp.maximum(m, _reduce_rows(jnp.maximum, jnp.max, s))
            alpha = jnp.exp2(m - m_new)
            p = jnp.exp2(s - m_new)
            l = alpha * l + _reduce_rows(jnp.add, jnp.sum, p)
            pv = lax.dot_general(vb, p.astype(BF16), (((0,), (0,)), ((), ())), preferred_element_type=F32)
            out += [m_new, l, alpha * acc + pv]
        return tuple(out)

    init = (jnp.full((1, tq), NEG, F32), jnp.zeros((1, tq), F32), jnp.zeros((V_DIM, tq), F32)) * hg
    carry = lax.fori_loop(0, n_full, functools.partial(step, masked=False), init)
    carry = lax.fori_loop(n_full, n_all, functools.partial(step, masked=True), carry)
    o_ref[0] = jnp.concatenate([carry[3 * hh + 2] / carry[3 * hh + 1] for hh in range(hg)], axis=0).astype(BF16)


def _attention(qh, kh, vh, hg, tq, tk, q_off, kv_len):
    B, _, Tq, _ = qh.shape
    Tk = kh.shape[2]
    body = functools.partial(_attn_body, hg=hg, tq=tq, tk=tk, nkv_total=Tk // tk, q_off=q_off, kv_len=kv_len)
    return pl.pallas_call(
        body, grid=(B, ATTN_HEADS // hg, Tq // tq),
        in_specs=[pl.BlockSpec((1, hg, tq, HEAD_ROW), lambda b, g, i: (b, g, i, 0)),
                  pl.BlockSpec((1, hg, Tk, HEAD_ROW), lambda b, g, i: (b, g, 0, 0)),
                  pl.BlockSpec((1, hg, Tk, V_DIM), lambda b, g, i: (b, g, 0, 0))],
        out_specs=pl.BlockSpec((1, hg * V_DIM, tq), lambda b, g, i: (b, g, i)),
        out_shape=jax.ShapeDtypeStruct((B, ATTN_HEADS * V_DIM, Tq), BF16),
        compiler_params=_params(("arbitrary", "arbitrary", "arbitrary")), name="attn",
    )(qh, kh, vh)


def _sorting_network(n):
    size = 1 << (n - 1).bit_length()

    def merge(lo, hi, r):
        step = r * 2
        if step < hi - lo:
            yield from merge(lo, hi, step)
            yield from merge(lo + r, hi, step)
            yield from ((i, i + r) for i in range(lo + r, hi - r, step))
        else:
            yield (lo, lo + r)

    def sort(lo, hi):
        if hi - lo >= 1:
            mid = lo + (hi - lo) // 2
            yield from sort(lo, mid)
            yield from sort(mid + 1, hi)
            yield from merge(lo, hi, 1)

    return tuple((i, j) for i, j in sort(0, size - 1) if j < n)


def _largest(groups, n_out):
    g = list(groups)
    for i, j in _sorting_network(len(g)):
        g[i], g[j] = jnp.maximum(g[i], g[j]), jnp.minimum(g[i], g[j])
    below = jnp.full(g[0].shape, -jnp.inf, F32)
    out = []
    for r in range(n_out):
        m = jnp.max(g[0], axis=0, keepdims=True)
        out.append(m)
        hit = g[0] == m
        for k in range(min(n_out - 1 - r, len(g))):
            g[k] = jnp.where(hit, g[k + 1] if k + 1 < len(g) else below, g[k])
    return out


def _top_ranks(work, rows_ref, want_rank):
    rows_ref[...] = jnp.full(rows_ref.shape, -jnp.inf, F32)
    for r, m in enumerate(_largest([work[8 * k:8 * (k + 1), :] for k in range(work.shape[0] // 8)], N_RANK)):
        rows_ref[r:r + 1, :] = m
    if not want_rank:
        return None
    rank = jnp.zeros(work.shape, F32)
    for r in range(N_RANK):
        rank = rank + jnp.where(rows_ref[r:r + 1, :] > work, 1.0, 0.0)
    return rank


def _route_body(x_ref, o_ref, wo_ref, gffn_ref, wpq_ref, k1_ref, k2_ref,
                x1_ref, h2_ref, rank_ref, e2_ref, cnt_ref, w1_ref, qp_ref, a0_ref, b0_ref, a1_ref, b1_ref):
    x1 = x_ref[...] + lax.dot_general(o_ref[0], wo_ref[...], (((0,), (0,)), ((), ())), preferred_element_type=F32)
    x1_ref[...] = x1
    h2 = _rms(x1, gffn_ref[...]).astype(BF16)
    h2_ref[...] = h2
    qp_ref[...] = jnp.dot(h2, wpq_ref[...], preferred_element_type=F32)
    key_dim = 2 * PEER_HALF

    def head(h, a_ref, b_ref):
        qh = qp_ref[:, pl.ds(pl.multiple_of(h * key_dim, key_dim), key_dim)]
        qn = qh * lax.rsqrt(jnp.mean(qh * qh, axis=-1, keepdims=True) + EPS)
        nt = (((1,), (1,)), ((), ()))
        s1 = lax.dot_general(k1_ref[h], qn[:, :PEER_HALF].astype(BF16), nt, preferred_element_type=F32)
        s2 = lax.dot_general(k2_ref[h], qn[:, PEER_HALF:].astype(BF16), nt, preferred_element_type=F32)
        _top_ranks(s1, a_ref, False)
        rank = _top_ranks(s2, b_ref, True)
        groups = [a_ref[0:1, :] + b_ref[8 * k:8 * (k + 1), :] for k in range(RANK_ROWS // 8)]
        groups += [a_ref[r:r + 1, :] + b_ref[0:8, :] for r in range(1, 8)]
        groups += [a_ref[8 * k:8 * (k + 1), :] + b_ref[0:1, :] for k in range(1, RANK_ROWS // 8)]
        pair_scores = _largest(groups, N_RANK)
        top = pair_scores[0]
        zsum = jnp.zeros_like(top)
        for v in pair_scores[:PEER_TOPK]:
            zsum = zsum + jnp.exp(v - top)
        tau = 0.5 * (pair_scores[PEER_TOPK - 1] + pair_scores[PEER_TOPK])
        thr = tau - s1
        cnt = jnp.zeros_like(s1)
        for c in range(N_RANK):
            cnt = cnt + jnp.where(b_ref[c:c + 1, :] >= thr, 1.0, 0.0)
        rank_ref[0, h] = rank.astype(BF16)
        e2_ref[0, h] = jnp.exp(s2 - b_ref[0:1, :]).astype(BF16)
        cnt_ref[0, h] = cnt
        w1_ref[0, h] = jnp.exp(s1 - a_ref[0:1, :]) / zsum

    def head_pair(p, carry):
        head(2 * p, a0_ref, b0_ref)
        head(2 * p + 1, a1_ref, b1_ref)
        return carry

    lax.fori_loop(0, PEER_HEADS // 2, head_pair, 0)


def _route(x2d, o_t, w, tm, tt):
    T, D = x2d.shape
    per_seq = o_t.shape[2] // tm
    r = tt // tm
    kq = w["wpq"].shape[1]
    const = lambda shape: pl.BlockSpec(shape, lambda i: (0,) * len(shape))
    lane_out = pl.BlockSpec((1, PEER_HEADS, PEER_N_KEYS, tm), lambda i: (i // r, 0, 0, i % r))
    sds = jax.ShapeDtypeStruct
    lane_shape = lambda dtype: sds((T // tt, PEER_HEADS, PEER_N_KEYS, tt), dtype)
    return pl.pallas_call(
        _route_body, grid=(T // tm,),
        in_specs=[pl.BlockSpec((tm, D), lambda i: (i, 0)),
                  pl.BlockSpec((1, o_t.shape[1], tm), lambda i: (i // per_seq, 0, i % per_seq)),
                  const((D, D)), const((1, D)), const((D, kq)),
                  const((PEER_HEADS, PEER_N_KEYS, PEER_HALF)), const((PEER_HEADS, PEER_N_KEYS, PEER_HALF))],
        out_specs=[pl.BlockSpec((tm, D), lambda i: (i, 0)), pl.BlockSpec((tm, D), lambda i: (i, 0)),
                   lane_out, lane_out, lane_out, lane_out],
        out_shape=[sds((T, D), F32), sds((T, D), BF16),
                   lane_shape(BF16), lane_shape(BF16), lane_shape(F32), lane_shape(F32)],
        scratch_shapes=[pltpu.VMEM((tm, kq), F32)] + [pltpu.VMEM((RANK_ROWS, tm), F32)] * 4,
        compiler_params=_params(("arbitrary",)), name="route",
    )(x2d, o_t, w["wo"], w["gffn"], w["wpq"], w["k1"], w["k2"])


GATE_ROWS = 32
GATE_LANES = 256
CHUNK_I = 2


def _row_tile(ref, h, i, ls):
    tile = jnp.broadcast_to(ref[0, h, i:i + 1, ls], (8, GATE_LANES)).astype(BF16)
    return jnp.concatenate([tile] * (GATE_ROWS // 8), axis=0)


def _peer_body(x1_ref, h2_ref, rank_ref, e2_ref, cnt_ref, w1_ref, u_ref, v_ref, gfin_ref, y_ref,
               acc_ref, *chunk_refs, tt, rows_per_step):
    j = pl.program_id(1)

    @pl.when(j == 0)
    def _():
        acc_ref[...] = jnp.zeros_like(acc_ref)

    n_chunks = len(chunk_refs) // 2
    a_refs, act_refs = chunk_refs[:n_chunks], chunk_refs[n_chunks:]
    inv_sqrt2 = 1.0 / math.sqrt(2.0)
    rows = CHUNK_I * PEER_N_KEYS

    def score_matmul(c):
        cs = slice(c * rows, (c + 1) * rows)
        a_refs[c][...] = lax.dot_general(u_ref[cs, :], h2_ref[...], (((1,), (1,)), ((), ())),
                                         preferred_element_type=F32)

    score_matmul(0)
    score_matmul(1)
    for c, act_ref in enumerate(act_refs):
        if c + 2 < n_chunks:
            score_matmul(c + 2)
        cs = slice(c * rows, (c + 1) * rows)
        a = a_refs[c]
        for k in range(CHUNK_I):
            i = c * CHUNK_I + k
            for tc in range(tt // GATE_LANES):
                ls = slice(tc * GATE_LANES, (tc + 1) * GATE_LANES)
                cnt = [_row_tile(cnt_ref, h, i, ls) for h in range(PEER_HEADS)]
                w1 = [_row_tile(w1_ref, h, i, ls) for h in range(PEER_HEADS)]
                for jb in range(PEER_N_KEYS // GATE_ROWS):
                    js = slice(jb * GATE_ROWS, (jb + 1) * GATE_ROWS)
                    gate = jnp.zeros((GATE_ROWS, GATE_LANES), BF16)
                    for h in range(PEER_HEADS):
                        sel = rank_ref[0, h, js, ls] < cnt[h]
                        gate = gate + jnp.where(sel, e2_ref[0, h, js, ls] * w1[h], 0.0)
                    rs = slice(k * PEER_N_KEYS + jb * GATE_ROWS, k * PEER_N_KEYS + (jb + 1) * GATE_ROWS)
                    av = a[rs, ls]
                    gelu = 0.5 * av * (1.0 + lax.erf(av * inv_sqrt2))
                    act_ref[rs, ls] = gelu.astype(BF16) * gate
        acc_ref[...] += lax.dot_general(act_ref[...], v_ref[cs, :], (((0,), (0,)), ((), ())),
                                        preferred_element_type=F32)

    @pl.when(j == pl.num_programs(1) - 1)
    def _():
        y_ref[...] = _rms(x1_ref[...] + acc_ref[...], gfin_ref[...])


def _peer(x1, h2, rank, e2, cnt, w1, w, tt, rows_per_step):
    T, D = x1.shape
    n_exp = w["v"].shape[0]
    eb = rows_per_step * PEER_N_KEYS
    lane_in = pl.BlockSpec((1, PEER_HEADS, PEER_N_KEYS, tt), lambda t, j: (t, 0, 0, 0))
    row_in = pl.BlockSpec((1, PEER_HEADS, rows_per_step, tt), lambda t, j: (t, 0, j, 0))
    body = functools.partial(_peer_body, tt=tt, rows_per_step=rows_per_step)
    n_chunks = rows_per_step // CHUNK_I
    a_chunks = [pltpu.VMEM((CHUNK_I * PEER_N_KEYS, tt), F32)] * n_chunks
    act_chunks = [pltpu.VMEM((CHUNK_I * PEER_N_KEYS, tt), BF16)] * n_chunks
    return pl.pallas_call(
        body, grid=(T // tt, n_exp // eb),
        in_specs=[pl.BlockSpec((tt, D), lambda t, j: (t, 0)), pl.BlockSpec((tt, D), lambda t, j: (t, 0)),
                  lane_in, lane_in, row_in, row_in,
                  pl.BlockSpec((eb, D), lambda t, j: (j, 0)), pl.BlockSpec((eb, D), lambda t, j: (j, 0)),
                  pl.BlockSpec((1, D), lambda t, j: (0, 0))],
        out_specs=pl.BlockSpec((tt, D), lambda t, j: (t, 0)),
        out_shape=jax.ShapeDtypeStruct((T, D), F32),
        scratch_shapes=[pltpu.VMEM((tt, D), F32)] + a_chunks + act_chunks,
        compiler_params=_params(("arbitrary", "arbitrary")), name="peer",
    )(x1, h2, rank, e2, cnt, w1, w["u"], w["v"], w["gfin"])


def _rope_tables(pos):
    half = MLA_ROPE_DIM // 2
    inv = ROPE_BASE ** (-jnp.arange(half, dtype=F32) / half)
    ang = pos.astype(F32)[:, None] * inv[None, :]
    cos = jnp.concatenate([jnp.cos(ang)] * 2, axis=-1)
    sin = jnp.concatenate([jnp.sin(ang)] * 2, axis=-1)
    return jnp.tile(cos, (1, MLA_HEADS)), jnp.tile(sin, (1, MLA_HEADS)), cos, sin


def _rot_cols(wcols):
    half = wcols.shape[-1] // 2
    return jnp.concatenate([-wcols[..., half:], wcols[..., :half]], axis=-1)


def _prep_weights(g_mix, w_in, b_f, g_q, w_uq, g_kv, w_ukv, w_o, g_ffn, w_pq, sub_keys, peer_u, peer_v, g_final):
    D = w_in.shape[0]
    off_ff = 3 * FOX_WIDTH
    off_cq = off_ff + FOX_HEADS
    off_ckv = off_cq + MLA_Q_LORA
    off_kr = off_ckv + MLA_KV_LORA
    kr_cols = w_in[:, off_kr:off_kr + MLA_ROPE_DIM]
    tail_pad = jnp.zeros((D, 128 - FOX_HEADS - 2 * MLA_ROPE_DIM), F32)
    win = jnp.concatenate([w_in[:, :off_ff], w_in[:, off_cq:off_kr], w_in[:, off_ff:off_cq],
                           kr_cols, _rot_cols(kr_cols), tail_pad], axis=1).astype(BF16)
    bf = jnp.pad(b_f, (0, 128 - FOX_HEADS)).reshape(1, 128)
    per_q = MLA_NOPE_DIM + MLA_ROPE_DIM
    wq = w_uq.reshape(MLA_Q_LORA, MLA_HEADS, per_q)
    q_rope = wq[:, :, MLA_NOPE_DIM:]
    wuq = jnp.concatenate([wq[:, :, :MLA_NOPE_DIM].reshape(MLA_Q_LORA, -1), q_rope.reshape(MLA_Q_LORA, -1),
                           _rot_cols(q_rope).reshape(MLA_Q_LORA, -1)], axis=1).astype(BF16)
    wkv = w_ukv.reshape(MLA_KV_LORA, MLA_HEADS, MLA_NOPE_DIM + MLA_V_DIM)
    wukv = jnp.concatenate([wkv[:, :, :MLA_NOPE_DIM].reshape(MLA_KV_LORA, -1),
                            wkv[:, :, MLA_NOPE_DIM:].reshape(MLA_KV_LORA, -1)], axis=1).astype(BF16)
    return dict(
        gmix=g_mix.reshape(1, -1), win=win, bf=bf, gq=g_q.reshape(1, -1), wuq=wuq, gkv=g_kv.reshape(1, -1),
        wukv=wukv, wo=w_o.astype(BF16), gffn=g_ffn.reshape(1, -1), wpq=w_pq.astype(BF16),
        k1=sub_keys[0].astype(BF16), k2=sub_keys[1].astype(BF16),
        u=peer_u.astype(BF16), v=peer_v.astype(BF16), gfin=g_final.reshape(1, -1))


def _ffn(x, o, w):
    B, S, D = x.shape
    T = B * S
    tm = min(256, T)
    tt = min(512, T)
    if S % tm:
        o = jnp.transpose(o, (1, 0, 2)).reshape(1, o.shape[1], T)
    x1, h2, rank, e2, cnt, w1 = _route(x.reshape(T, D), o, w, tm, tt)
    return _peer(x1, h2, rank, e2, cnt, w1, w, tt, 32).reshape(B, S, D)


def kernel(x_prompt, x_sample, cache_fox_k, cache_fox_v, cache_fox_logf, cache_mla_ckv, cache_mla_krope, g_mix, w_in, b_f, g_q, w_uq, g_kv, w_ukv, w_o, g_ffn, w_pq, peer_sub_keys, peer_u, peer_v, g_final):
    assert g_mix.shape[0] == 1, "single-layer model"
    w = _prep_weights(g_mix[0], w_in[0], b_f[0], g_q[0], w_uq[0], g_kv[0], w_ukv[0], w_o[0], g_ffn[0], w_pq[0],
                      peer_sub_keys[0], peer_u[0], peer_v[0], g_final)
    B, S, D = x_prompt.shape
    Bs, Ss, _ = x_sample.shape
    P = cache_fox_k.shape[2]

    tables_p = _rope_tables(jnp.arange(S, dtype=jnp.int32))
    tm = min(512, S)
    fk, fv, logf, ckv, kr, qh, kh, vh = _proj(x_prompt, jnp.zeros((B, 1, 128), F32), tables_p, w, tm)
    tq = min(256, S)
    o_p = _attention(qh, kh, vh, ATTN_HEAD_GROUP, tq, tq, 0, S)
    y_p = _ffn(x_prompt, o_p, w)

    kh_c, vh_c, c_last = _cache_rows(
        cache_fox_k[0].reshape(Bs, P, FOX_WIDTH), cache_fox_v[0].reshape(Bs, P, FOX_WIDTH),
        jnp.pad(cache_fox_logf[0], ((0, 0), (0, 0), (0, 128 - FOX_HEADS))),
        cache_mla_ckv[0], cache_mla_krope[0], w["wukv"])
    tables_s = _rope_tables(P + jnp.arange(Ss, dtype=jnp.int32))
    sfk, sfv, slogf, sckv, skr, sqh, skh, svh = _proj(x_sample, c_last, tables_s, w, Ss)
    tk = 256
    L = P + Ss
    pad = (-L) % tk
    kh_all = jnp.pad(jnp.concatenate([kh_c, skh], axis=2), ((0, 0), (0, 0), (0, pad), (0, 0)))
    vh_all = jnp.pad(jnp.concatenate([vh_c, svh], axis=2), ((0, 0), (0, 0), (0, pad), (0, 0)))
    o_s = _attention(sqh, kh_all, vh_all, ATTN_HEAD_GROUP, Ss, tk, P, L)
    y_s = _ffn(x_sample, o_s, w)

    return (y_p, y_s,
            fk.reshape(1, B, S, FOX_HEADS, FOX_HEAD_DIM), fv.reshape(1, B, S, FOX_HEADS, FOX_HEAD_DIM),
            logf[None], ckv[None], kr[None],
            sfk.reshape(1, Bs, Ss, FOX_HEADS, FOX_HEAD_DIM), sfv.reshape(1, Bs, Ss, FOX_HEADS, FOX_HEAD_DIM),
            slogf[None], sckv[None], skr[None])
```

```python
import functools
import math

import numpy as np
import jax
import jax.numpy as jnp
from jax import lax
from jax.experimental import pallas as pl
from jax.experimental.pallas import tpu as pltpu

F32 = jnp.float32
BF16 = jnp.bfloat16

EPS = 1e-6
CHUNK = 64
ROPE_BASE = 10000.0

FOX_HEADS = 8
FOX_HEAD_DIM = 64
FOX_WIDTH = FOX_HEADS * FOX_HEAD_DIM
MLA_HEADS = 8
MLA_NOPE_DIM = 64
MLA_ROPE_DIM = 32
MLA_V_DIM = 64
MLA_Q_LORA = 256
MLA_KV_LORA = 128
ATTN_HEADS = FOX_HEADS + MLA_HEADS
HEAD_ROW = 128
V_DIM = 64

PEER_HEADS = 8
PEER_N_KEYS = 128
PEER_HALF = 128
PEER_TOPK = 16
N_RANK = PEER_TOPK + 1
RANK_ROWS = 24
ROUTE_HEAD_GROUP = 4

Z_FQ = 0
Z_FK = Z_FQ + FOX_WIDTH
Z_FV = Z_FK + FOX_WIDTH
Z_CQ = Z_FV + FOX_WIDTH
Z_CKV = Z_CQ + MLA_Q_LORA
Z_TAIL = Z_CKV + MLA_KV_LORA
Z_TAILB = Z_TAIL + 128
Z_WIDTH = Z_TAILB + 128
ROPE_LANE = 64

NEG = -1e30
LOG2E = math.log2(math.e)
ATTN_HEAD_GROUP = 8
V7X_VMEM_LIMIT_BYTES = 56 * 1024 * 1024


def _params(sem, flags=None):
    return pltpu.CompilerParams(dimension_semantics=sem, vmem_limit_bytes=V7X_VMEM_LIMIT_BYTES, flags=flags)


def _rms(x, g):
    return x * lax.rsqrt(jnp.mean(x * x, axis=-1, keepdims=True) + EPS) * g


def _log_sigmoid(x):
    return jnp.minimum(x, 0.0) - jnp.log1p(jnp.exp(-jnp.abs(x)))


def _split3(x):
    hi = x.astype(BF16)
    r1 = x - hi.astype(F32)
    mid = r1.astype(BF16)
    lo = (r1 - mid.astype(F32)).astype(BF16)
    return hi, mid, lo


def _cumsum_rows(tri, x, carry):
    acc = carry
    for piece in _split3(x):
        acc = acc + jnp.dot(tri, piece, preferred_element_type=F32)
    return acc


def _fox_aux(c_col):
    hi, mid, lo = (p.astype(F32) for p in _split3(c_col * LOG2E))
    lane = lax.broadcasted_iota(jnp.int32, (1, HEAD_ROW - FOX_HEAD_DIM), 1)
    q_aux = jnp.where(lane == 0, hi, jnp.where(lane == 1, mid, jnp.where(
        lane == 2, lo, jnp.where(lane < 6, 1.0, 0.0))))
    k_aux = jnp.where(lane < 3, 1.0, jnp.where(lane == 3, -hi, jnp.where(
        lane == 4, -mid, jnp.where(lane == 5, -lo, 0.0))))
    return q_aux, k_aux


def _write_fox_kv(kh_ref, vh_ref, k, v, c_slab):
    for h in range(FOX_HEADS):
        sl = slice(h * FOX_HEAD_DIM, (h + 1) * FOX_HEAD_DIM)
        _, k_aux = _fox_aux(c_slab[:, h:h + 1])
        kh_ref[0, h, :, 0:FOX_HEAD_DIM] = k[:, sl].astype(BF16)
        kh_ref[0, h, :, FOX_HEAD_DIM:HEAD_ROW] = k_aux.astype(BF16)
        vh_ref[0, h] = v[:, sl].astype(BF16)


def _write_mla_kv(kh_ref, vh_ref, kv, krope):
    rows = kv.shape[0]
    k_rope = krope.astype(BF16)
    pad = jnp.zeros((rows, HEAD_ROW - MLA_NOPE_DIM - MLA_ROPE_DIM), BF16)
    for h in range(MLA_HEADS):
        sl = slice(h * MLA_NOPE_DIM, (h + 1) * MLA_NOPE_DIM)
        kh_ref[0, FOX_HEADS + h, :, 0:MLA_NOPE_DIM] = kv[:, sl].astype(BF16)
        kh_ref[0, FOX_HEADS + h, :, MLA_NOPE_DIM:MLA_NOPE_DIM + MLA_ROPE_DIM] = k_rope
        kh_ref[0, FOX_HEADS + h, :, MLA_NOPE_DIM + MLA_ROPE_DIM:HEAD_ROW] = pad
        vsl = slice(MLA_HEADS * MLA_NOPE_DIM + h * MLA_V_DIM, MLA_HEADS * MLA_NOPE_DIM + (h + 1) * MLA_V_DIM)
        vh_ref[0, FOX_HEADS + h] = kv[:, vsl].astype(BF16)


def _proj_body(x_ref, c0_ref, tri_ref, cos_ref, sin_ref, gmix_ref, win_ref, bf_ref, gq_ref, wuq_ref, gkv_ref,
               wukv_ref, sel_ref,
               fk_ref, fv_ref, logf_ref, ckv_ref, kr_ref, qh_ref, kh_ref, vh_ref, carry_ref):
    @pl.when(pl.program_id(1) == 0)
    def _():
        carry_ref[...] = c0_ref[0]

    rows = x_ref.shape[1]
    h = _rms(x_ref[0], gmix_ref[...]).astype(BF16)
    z = jnp.dot(h, win_ref[...], preferred_element_type=F32)

    fq = z[:, Z_FQ:Z_FK]
    fk = z[:, Z_FK:Z_FV]
    fv = z[:, Z_FV:Z_CQ]
    fk_ref[0] = fk
    fv_ref[0] = fv

    tail_a = z[:, Z_TAIL:Z_TAILB]
    tail_b = z[:, Z_TAILB:Z_WIDTH]
    logf = _log_sigmoid(tail_a + bf_ref[...])
    logf_ref[0] = logf[:, 0:FOX_HEADS]
    c_slab = _cumsum_rows(tri_ref[...], logf, carry_ref[...])
    carry_ref[...] = c_slab[rows - 1:rows, :]

    lane = lax.broadcasted_iota(jnp.int32, (1, 128), 1)
    hi, mid, lo = (p.astype(F32) for p in _split3(c_slab * LOG2E))
    pieces = jnp.where(lane < FOX_HEADS, hi, jnp.where(lane < 2 * FOX_HEADS, pltpu.roll(mid, FOX_HEADS, 1), jnp.where(
        lane < 3 * FOX_HEADS, pltpu.roll(lo, 2 * FOX_HEADS, 1), jnp.where(lane == 3 * FOX_HEADS, 1.0, 0.0))))
    aux = jnp.dot(pieces.astype(BF16), sel_ref[...], preferred_element_type=F32)

    cos = cos_ref[...]
    sin = sin_ref[...]
    krope = tail_a * cos + tail_b * sin
    kr_ref[0] = krope[:, ROPE_LANE:ROPE_LANE + MLA_ROPE_DIM]

    cqn = _rms(z[:, Z_CQ:Z_CKV], gq_ref[...]).astype(BF16)
    q = jnp.dot(cqn, wuq_ref[...], preferred_element_type=F32)
    ckv = _rms(z[:, Z_CKV:Z_TAIL], gkv_ref[...])
    ckv_ref[0] = ckv
    kv = jnp.dot(ckv.astype(BF16), wukv_ref[...], preferred_element_type=F32)

    fox_scale = FOX_HEAD_DIM ** -0.5 * LOG2E
    mla_scale = (MLA_NOPE_DIM + MLA_ROPE_DIM) ** -0.5 * LOG2E
    low = lane < FOX_HEAD_DIM
    nope_w = MLA_HEADS * MLA_NOPE_DIM
    row_w = MLA_HEADS * HEAD_ROW

    def head_src(block, hd):
        slab = block[:, (hd // 2) * 128:(hd // 2 + 1) * 128]
        return slab if hd % 2 == 0 else pltpu.roll(slab, 64, 1)

    for hd in range(FOX_HEADS):
        rs = slice(hd * HEAD_ROW, (hd + 1) * HEAD_ROW)
        ks = slice(row_w + hd * HEAD_ROW, row_w + (hd + 1) * HEAD_ROW)
        qh_ref[0, hd] = jnp.where(low, head_src(fq, hd) * fox_scale, aux[:, rs]).astype(BF16)
        kh_ref[0, hd] = jnp.where(low, head_src(fk, hd), aux[:, ks]).astype(BF16)
        vh_ref[0, hd] = fv[:, hd * V_DIM:(hd + 1) * V_DIM].astype(BF16)
    for hd in range(MLA_HEADS):
        rope = (q[:, nope_w + hd * HEAD_ROW:nope_w + (hd + 1) * HEAD_ROW] * cos
                + q[:, nope_w + row_w + hd * HEAD_ROW:nope_w + row_w + (hd + 1) * HEAD_ROW] * sin)
        qh_ref[0, FOX_HEADS + hd] = (jnp.where(low, head_src(q, hd), rope) * mla_scale).astype(BF16)
        kh_ref[0, FOX_HEADS + hd] = jnp.where(low, head_src(kv, hd), krope).astype(BF16)
        vh_ref[0, FOX_HEADS + hd] = kv[:, nope_w + hd * V_DIM:nope_w + (hd + 1) * V_DIM].astype(BF16)


def _proj(x, c0, tables, w, tm):
    B, S, D = x.shape
    cos, sin = tables
    tri = jnp.tril(jnp.ones((tm, tm), F32)).astype(BF16)
    const = lambda shape: pl.BlockSpec(shape, lambda b, t: (0,) * len(shape))
    tab = pl.BlockSpec((tm, 128), lambda b, t: (t, 0))
    in_specs = [
        pl.BlockSpec((1, tm, D), lambda b, t: (b, t, 0)),
        pl.BlockSpec((1, 1, 128), lambda b, t: (b, 0, 0)),
        const((tm, tm)), tab, tab,
        const((1, D)), const((D, Z_WIDTH)), const((1, 128)),
        const((1, MLA_Q_LORA)), const(w["wuq"].shape), const((1, MLA_KV_LORA)), const(w["wukv"].shape),
        const(w["sel"].shape),
    ]
    tok = lambda width: pl.BlockSpec((1, tm, width), lambda b, t: (b, t, 0))
    head = lambda width: pl.BlockSpec((1, ATTN_HEADS, tm, width), lambda b, t: (b, 0, t, 0))
    out_specs = [tok(FOX_WIDTH), tok(FOX_WIDTH), tok(FOX_HEADS), tok(MLA_KV_LORA), tok(MLA_ROPE_DIM),
                 head(HEAD_ROW), head(HEAD_ROW), head(V_DIM)]
    sds = jax.ShapeDtypeStruct
    out_shape = [sds((B, S, FOX_WIDTH), F32), sds((B, S, FOX_WIDTH), F32), sds((B, S, FOX_HEADS), F32),
                 sds((B, S, MLA_KV_LORA), F32), sds((B, S, MLA_ROPE_DIM), F32),
                 sds((B, ATTN_HEADS, S, HEAD_ROW), BF16), sds((B, ATTN_HEADS, S, HEAD_ROW), BF16),
                 sds((B, ATTN_HEADS, S, V_DIM), BF16)]
    return pl.pallas_call(
        _proj_body, grid=(B, S // tm), in_specs=in_specs, out_specs=out_specs, out_shape=out_shape,
        scratch_shapes=[pltpu.VMEM((1, 128), F32)],
        compiler_params=_params(("arbitrary", "arbitrary")), name="proj",
    )(x, c0, tri, cos, sin, w["gmix"], w["win"], w["bf"], w["gq"], w["wuq"], w["gkv"], w["wukv"], w["sel"])


def _cache_body(ck_ref, cv_ref, clogf_ref, cckv_ref, ckr_ref, tri_ref, wukv_ref, kh_ref, vh_ref, clast_ref):
    rows = ck_ref.shape[1]
    c_slab = _cumsum_rows(tri_ref[...], clogf_ref[0], jnp.zeros((1, 128), F32))
    clast_ref[0] = c_slab[rows - 1:rows, :]
    _write_fox_kv(kh_ref, vh_ref, ck_ref[0], cv_ref[0], c_slab)
    kv = jnp.dot(cckv_ref[0].astype(BF16), wukv_ref[...], preferred_element_type=F32)
    _write_mla_kv(kh_ref, vh_ref, kv, ckr_ref[0])


def _cache_rows(ck, cv, clogf_slab, cckv, ckr, wukv):
    B, P, _ = ck.shape
    tri = jnp.tril(jnp.ones((P, P), F32)).astype(BF16)
    per_b = lambda width: pl.BlockSpec((1, P, width), lambda b: (b, 0, 0))
    const = lambda shape: pl.BlockSpec(shape, lambda b: (0,) * len(shape))
    head = lambda width: pl.BlockSpec((1, ATTN_HEADS, P, width), lambda b: (b, 0, 0, 0))
    sds = jax.ShapeDtypeStruct
    return pl.pallas_call(
        _cache_body, grid=(B,),
        in_specs=[per_b(FOX_WIDTH), per_b(FOX_WIDTH), per_b(128), per_b(MLA_KV_LORA), per_b(MLA_ROPE_DIM),
                  const((P, P)), const(wukv.shape)],
        out_specs=[head(HEAD_ROW), head(V_DIM), pl.BlockSpec((1, 1, 128), lambda b: (b, 0, 0))],
        out_shape=[sds((B, ATTN_HEADS, P, HEAD_ROW), BF16), sds((B, ATTN_HEADS, P, V_DIM), BF16),
                   sds((B, 1, 128), F32)],
        compiler_params=_params(("arbitrary",)), name="cache_rows",
    )(ck, cv, clogf_slab, cckv, ckr, tri, wukv)


def _reduce_rows(pair_op, final_op, x):
    while x.shape[0] > 8 and x.shape[0] % 16 == 0:
        half = x.shape[0] // 2
        x = pair_op(x[:half], x[half:])
    return final_op(x, axis=0, keepdims=True)


def _attn_body(q_ref, k_ref, v_ref, o_ref, *, hg, tq, tk, nkv_total, q_off, kv_len):
    grp = pl.program_id(1)
    qi = pl.program_id(2)
    q_lo = q_off + qi * tq
    qpos = q_lo + lax.broadcasted_iota(jnp.int32, (1, tq), 1)
    lim = jnp.where(grp >= FOX_HEADS // hg, qpos | (CHUNK - 1), qpos)
    lim = jnp.minimum(lim, kv_len - 1)
    n_full = jnp.minimum(lax.div(q_lo + 1, tk), nkv_total)
    n_all = jnp.minimum(lax.div((q_lo + tq - 1) | (CHUNK - 1), tk) + 1, nkv_total)

    def step(j, carry, masked):
        off = pl.multiple_of(j * tk, tk)
        if masked:
            visible = off + lax.broadcasted_iota(jnp.int32, (tk, 1), 0) <= lim
        out = []
        scores = [lax.dot_general(k_ref[0, hh, pl.ds(off, tk), :], q_ref[0, hh], (((1,), (1,)), ((), ())),
                                  preferred_element_type=F32) for hh in range(hg)]
        for hh in range(hg):
            m, l, acc = carry[3 * hh:3 * hh + 3]
            vb = v_ref[0, hh, pl.ds(off, tk), :]
            s = scores[hh]
            if masked:
                s = jnp.where(visible, s, NEG)
            m_new = jnp.maximum(m, _reduce_rows(jnp.maximum, jnp.max, s))
            alpha = jnp.exp2(m - m_new)
            p = jnp.exp2(s - m_new)
            l = alpha * l + _reduce_rows(jnp.add, jnp.sum, p)
            pv = lax.dot_general(vb, p.astype(BF16), (((0,), (0,)), ((), ())), preferred_element_type=F32)
            out += [m_new, l, alpha * acc + pv]
        return tuple(out)

    init = (jnp.full((1, tq), NEG, F32), jnp.zeros((1, tq), F32), jnp.zeros((V_DIM, tq), F32)) * hg
    carry = lax.fori_loop(0, n_full, functools.partial(step, masked=False), init)
    carry = lax.fori_loop(n_full, n_all, functools.partial(step, masked=True), carry)
    o_ref[0] = jnp.concatenate([carry[3 * hh + 2] / carry[3 * hh + 1] for hh in range(hg)], axis=0).astype(BF16)


def _attention(qh, kh, vh, hg, tq, tk, q_off, kv_len):
    B, _, Tq, _ = qh.shape
    Tk = kh.shape[2]
    body = functools.partial(_attn_body, hg=hg, tq=tq, tk=tk, nkv_total=Tk // tk, q_off=q_off, kv_len=kv_len)
    return pl.pallas_call(
        body, grid=(B, ATTN_HEADS // hg, Tq // tq),
        in_specs=[pl.BlockSpec((1, hg, tq, HEAD_ROW), lambda b, g, i: (b, g, i, 0)),
                  pl.BlockSpec((1, hg, Tk, HEAD_ROW), lambda b, g, i: (b, g, 0, 0)),
                  pl.BlockSpec((1, hg, Tk, V_DIM), lambda b, g, i: (b, g, 0, 0))],
        out_specs=pl.BlockSpec((1, hg * V_DIM, tq), lambda b, g, i: (b, g, i)),
        out_shape=jax.ShapeDtypeStruct((B, ATTN_HEADS * V_DIM, Tq), BF16),
        compiler_params=_params(("arbitrary", "arbitrary", "arbitrary")), name="attn",
    )(qh, kh, vh)


def _sorting_network(n):
    size = 1 << (n - 1).bit_length()

    def merge(lo, hi, r):
        step = r * 2
        if step < hi - lo:
            yield from merge(lo, hi, step)
            yield from merge(lo + r, hi, step)
            yield from ((i, i + r) for i in range(lo + r, hi - r, step))
        else:
            yield (lo, lo + r)

    def sort(lo, hi):
        if hi - lo >= 1:
            mid = lo + (hi - lo) // 2
            yield from sort(lo, mid)
            yield from sort(mid + 1, hi)
            yield from merge(lo, hi, 1)

    return tuple((i, j) for i, j in sort(0, size - 1) if j < n)


def _largest(groups, n_out):
    g = list(groups)
    for i, j in _sorting_network(len(g)):
        g[i], g[j] = jnp.maximum(g[i], g[j]), jnp.minimum(g[i], g[j])
    below = jnp.full(g[0].shape, -jnp.inf, F32)
    out = []
    for r in range(n_out):
        m = jnp.max(g[0], axis=0, keepdims=True)
        out.append(m)
        hit = g[0] == m
        for k in range(min(n_out - 1 - r, len(g))):
            g[k] = jnp.where(hit, g[k + 1] if k + 1 < len(g) else below, g[k])
    return out


def _top_ranks(work, rows_ref, want_rank):
    rows_ref[...] = jnp.full(rows_ref.shape, -jnp.inf, F32)
    for r, m in enumerate(_largest([work[8 * k:8 * (k + 1), :] for k in range(work.shape[0] // 8)], N_RANK)):
        rows_ref[r:r + 1, :] = m
    if not want_rank:
        return None
    rank = jnp.zeros(work.shape, F32)
    for r in range(N_RANK):
        rank = rank + jnp.where(rows_ref[r:r + 1, :] > work, 1.0, 0.0)
    return rank


def _route_body(x_ref, o_ref, wo_ref, gffn_ref, wpq_ref, k1_ref, k2_ref,
                x1_ref, h2_ref, rank_ref, e2_ref, cnt_ref, w1_ref, qp_ref, *rank_rows):
    x1 = x_ref[...] + lax.dot_general(o_ref[0], wo_ref[...], (((0,), (0,)), ((), ())), preferred_element_type=F32)
    x1_ref[...] = x1
    h2 = _rms(x1, gffn_ref[...]).astype(BF16)
    h2_ref[...] = h2
    qp_ref[...] = jnp.dot(h2, wpq_ref[...], preferred_element_type=F32)
    key_dim = 2 * PEER_HALF

    def head(h, a_ref, b_ref):
        qh = qp_ref[:, pl.ds(pl.multiple_of(h * key_dim, key_dim), key_dim)]
        qn = qh * lax.rsqrt(jnp.mean(qh * qh, axis=-1, keepdims=True) + EPS)
        nt = (((1,), (1,)), ((), ()))
        s1 = lax.dot_general(k1_ref[h], qn[:, :PEER_HALF].astype(BF16), nt, preferred_element_type=F32)
        s2 = lax.dot_general(k2_ref[h], qn[:, PEER_HALF:].astype(BF16), nt, preferred_element_type=F32)
        _top_ranks(s1, a_ref, False)
        rank = _top_ranks(s2, b_ref, True)
        groups = [a_ref[0:1, :] + b_ref[8 * k:8 * (k + 1), :] for k in range(RANK_ROWS // 8)]
        groups += [a_ref[r:r + 1, :] + b_ref[0:8, :] for r in range(1, 8)]
        groups += [a_ref[8 * k:8 * (k + 1), :] + b_ref[0:1, :] for k in range(1, RANK_ROWS // 8)]
        pair_scores = _largest(groups, N_RANK)
        top = pair_scores[0]
        zsum = jnp.zeros_like(top)
        for v in pair_scores[:PEER_TOPK]:
            zsum = zsum + jnp.exp(v - top)
        tau = 0.5 * (pair_scores[PEER_TOPK - 1] + pair_scores[PEER_TOPK])
        thr = tau - s1
        cnt = jnp.zeros_like(s1)
        for c in range(N_RANK):
            cnt = cnt + jnp.where(b_ref[c:c + 1, :] >= thr, 1.0, 0.0)
        rank_ref[0, h] = rank.astype(BF16)
        e2_ref[0, h] = jnp.exp(s2 - b_ref[0:1, :]).astype(BF16)
        cnt_ref[0, h] = cnt
        w1_ref[0, h] = jnp.exp(s1 - a_ref[0:1, :]) / zsum

    group = len(rank_rows) // 2

    def head_group(p, carry):
        for k in range(group):
            head(group * p + k, rank_rows[2 * k], rank_rows[2 * k + 1])
        return carry

    lax.fori_loop(0, PEER_HEADS // group, head_group, 0)


def _route(x2d, o_t, w, tm, tt):
    T, D = x2d.shape
    per_seq = o_t.shape[2] // tm
    r = tt // tm
    kq = w["wpq"].shape[1]
    const = lambda shape: pl.BlockSpec(shape, lambda i: (0,) * len(shape))
    lane_out = pl.BlockSpec((1, PEER_HEADS, PEER_N_KEYS, tm), lambda i: (i // r, 0, 0, i % r))
    sds = jax.ShapeDtypeStruct
    lane_shape = lambda dtype: sds((T // tt, PEER_HEADS, PEER_N_KEYS, tt), dtype)
    return pl.pallas_call(
        _route_body, grid=(T // tm,),
        in_specs=[pl.BlockSpec((tm, D), lambda i: (i, 0)),
                  pl.BlockSpec((1, o_t.shape[1], tm), lambda i: (i // per_seq, 0, i % per_seq)),
                  const((D, D)), const((1, D)), const((D, kq)),
                  const((PEER_HEADS, PEER_N_KEYS, PEER_HALF)), const((PEER_HEADS, PEER_N_KEYS, PEER_HALF))],
        out_specs=[pl.BlockSpec((tm, D), lambda i: (i, 0)), pl.BlockSpec((tm, D), lambda i: (i, 0)),
                   lane_out, lane_out, lane_out, lane_out],
        out_shape=[sds((T, D), F32), sds((T, D), BF16),
                   lane_shape(BF16), lane_shape(BF16), lane_shape(F32), lane_shape(F32)],
        scratch_shapes=[pltpu.VMEM((tm, kq), F32)] + [pltpu.VMEM((RANK_ROWS, tm), F32)] * (2 * ROUTE_HEAD_GROUP),
        compiler_params=_params(("arbitrary",)), name="route",
    )(x2d, o_t, w["wo"], w["gffn"], w["wpq"], w["k1"], w["k2"])


GATE_ROWS = 32
GATE_LANES = 256
CHUNK_I = 2
OUT_CHUNKS = 1


def _row_tile(ref, h, i, ls):
    tile = jnp.broadcast_to(ref[0, h, i:i + 1, ls], (8, GATE_LANES)).astype(BF16)
    return jnp.concatenate([tile] * (GATE_ROWS // 8), axis=0)


def _peer_body(x1_ref, h2_ref, rank_ref, e2_ref, cnt_ref, w1_ref, u_ref, v_ref, gfin_ref, y_ref,
               acc_ref, *chunk_refs, tt, rows_per_step):
    j = pl.program_id(1)

    @pl.when(j == 0)
    def _():
        acc_ref[...] = jnp.zeros_like(acc_ref)

    n_chunks = len(chunk_refs) * OUT_CHUNKS // (OUT_CHUNKS + 1)
    a_refs, act_refs = chunk_refs[:n_chunks], chunk_refs[n_chunks:]
    inv_sqrt2 = 1.0 / math.sqrt(2.0)
    rows = CHUNK_I * PEER_N_KEYS

    def score_matmul(c):
        cs = slice(c * rows, (c + 1) * rows)
        a_refs[c][...] = lax.dot_general(u_ref[cs, :], h2_ref[...], (((1,), (1,)), ((), ())),
                                         preferred_element_type=F32)

    score_matmul(0)
    score_matmul(1)
    for c in range(n_chunks):
        if c + 2 < n_chunks:
            score_matmul(c + 2)
        act_ref = act_refs[c // OUT_CHUNKS]
        act_row = (c % OUT_CHUNKS) * rows
        a = a_refs[c]
        for k in range(CHUNK_I):
            i = c * CHUNK_I + k
            for tc in range(tt // GATE_LANES):
                ls = slice(tc * GATE_LANES, (tc + 1) * GATE_LANES)
                cnt = [_row_tile(cnt_ref, h, i, ls) for h in range(PEER_HEADS)]
                w1 = [_row_tile(w1_ref, h, i, ls) for h in range(PEER_HEADS)]
                for jb in range(PEER_N_KEYS // GATE_ROWS):
                    js = slice(jb * GATE_ROWS, (jb + 1) * GATE_ROWS)
                    gate = jnp.zeros((GATE_ROWS, GATE_LANES), BF16)
                    for h in range(PEER_HEADS):
                        sel = rank_ref[0, h, js, ls] < cnt[h]
                        gate = gate + jnp.where(sel, e2_ref[0, h, js, ls] * w1[h], 0.0)
                    r0 = k * PEER_N_KEYS + jb * GATE_ROWS
                    av = a[r0:r0 + GATE_ROWS, ls]
                    gelu = 0.5 * av * (1.0 + lax.erf(av * inv_sqrt2))
                    act_ref[act_row + r0:act_row + r0 + GATE_ROWS, ls] = gelu.astype(BF16) * gate
        if c % OUT_CHUNKS == OUT_CHUNKS - 1:
            vs = slice((c + 1 - OUT_CHUNKS) * rows, (c + 1) * rows)
            acc_ref[...] += lax.dot_general(act_ref[...], v_ref[vs, :], (((0,), (0,)), ((), ())),
                                            preferred_element_type=F32)

    @pl.when(j == pl.num_programs(1) - 1)
    def _():
        y_ref[...] = _rms(x1_ref[...] + acc_ref[...], gfin_ref[...])


def _peer(x1, h2, rank, e2, cnt, w1, w, tt, rows_per_step):
    T, D = x1.shape
    n_exp = w["v"].shape[0]
    eb = rows_per_step * PEER_N_KEYS
    lane_in = pl.BlockSpec((1, PEER_HEADS, PEER_N_KEYS, tt), lambda t, j: (t, 0, 0, 0))
    row_in = pl.BlockSpec((1, PEER_HEADS, rows_per_step, tt), lambda t, j: (t, 0, j, 0))
    body = functools.partial(_peer_body, tt=tt, rows_per_step=rows_per_step)
    n_chunks = rows_per_step // CHUNK_I
    a_chunks = [pltpu.VMEM((CHUNK_I * PEER_N_KEYS, tt), F32)] * n_chunks
    act_chunks = [pltpu.VMEM((OUT_CHUNKS * CHUNK_I * PEER_N_KEYS, tt), BF16)] * (n_chunks // OUT_CHUNKS)
    return pl.pallas_call(
        body, grid=(T // tt, n_exp // eb),
        in_specs=[pl.BlockSpec((tt, D), lambda t, j: (t, 0)), pl.BlockSpec((tt, D), lambda t, j: (t, 0)),
                  lane_in, lane_in, row_in, row_in,
                  pl.BlockSpec((eb, D), lambda t, j: (j, 0)), pl.BlockSpec((eb, D), lambda t, j: (j, 0)),
                  pl.BlockSpec((1, D), lambda t, j: (0, 0))],
        out_specs=pl.BlockSpec((tt, D), lambda t, j: (t, 0)),
        out_shape=jax.ShapeDtypeStruct((T, D), F32),
        scratch_shapes=[pltpu.VMEM((tt, D), F32)] + a_chunks + act_chunks,
        compiler_params=_params(("arbitrary", "arbitrary")), name="peer",
    )(x1, h2, rank, e2, cnt, w1, w["u"], w["v"], w["gfin"])


def _rope_tables(pos):
    half = MLA_ROPE_DIM // 2
    inv = ROPE_BASE ** (-jnp.arange(half, dtype=F32) / half)
    ang = pos.astype(F32)[:, None] * inv[None, :]
    pad = ((0, 0), (ROPE_LANE, HEAD_ROW - ROPE_LANE - MLA_ROPE_DIM))
    return (jnp.pad(jnp.concatenate([jnp.cos(ang)] * 2, axis=-1), pad),
            jnp.pad(jnp.concatenate([jnp.sin(ang)] * 2, axis=-1), pad))


def _rot_cols(wcols):
    half = wcols.shape[-1] // 2
    return jnp.concatenate([-wcols[..., half:], wcols[..., :half]], axis=-1)


def _prep_weights(g_mix, w_in, b_f, g_q, w_uq, g_kv, w_ukv, w_o, g_ffn, w_pq, sub_keys, peer_u, peer_v, g_final):
    D = w_in.shape[0]
    off_ff = 3 * FOX_WIDTH
    off_cq = off_ff + FOX_HEADS
    off_ckv = off_cq + MLA_Q_LORA
    off_kr = off_ckv + MLA_KV_LORA
    kr_cols = w_in[:, off_kr:off_kr + MLA_ROPE_DIM]
    zeros = lambda n: jnp.zeros((D, n), F32)
    rope_tail = zeros(HEAD_ROW - ROPE_LANE - MLA_ROPE_DIM)
    tail_a = [w_in[:, off_ff:off_cq], zeros(ROPE_LANE - FOX_HEADS), kr_cols, rope_tail]
    tail_b = [zeros(ROPE_LANE), _rot_cols(kr_cols), rope_tail]
    win = jnp.concatenate([w_in[:, :off_ff], w_in[:, off_cq:off_kr]] + tail_a + tail_b, axis=1).astype(BF16)
    bf = jnp.pad(b_f, (0, 128 - FOX_HEADS)).reshape(1, 128)
    per_q = MLA_NOPE_DIM + MLA_ROPE_DIM
    wq = w_uq.reshape(MLA_Q_LORA, MLA_HEADS, per_q)
    q_rope = wq[:, :, MLA_NOPE_DIM:]
    in_row = lambda cols: jnp.pad(cols, ((0, 0), (0, 0), (ROPE_LANE, HEAD_ROW - ROPE_LANE - MLA_ROPE_DIM)))
    wuq = jnp.concatenate([wq[:, :, :MLA_NOPE_DIM].reshape(MLA_Q_LORA, -1), in_row(q_rope).reshape(MLA_Q_LORA, -1),
                           in_row(_rot_cols(q_rope)).reshape(MLA_Q_LORA, -1)], axis=1).astype(BF16)
    sel = np.zeros((128, 2 * FOX_HEADS * HEAD_ROW), np.float32)
    for hd in range(FOX_HEADS):
        q0, k0 = hd * HEAD_ROW + FOX_HEAD_DIM, (FOX_HEADS + hd) * HEAD_ROW + FOX_HEAD_DIM
        for k in range(3):
            sel[k * FOX_HEADS + hd, q0 + k] = 1.0
            sel[3 * FOX_HEADS, q0 + 3 + k] = 1.0
            sel[3 * FOX_HEADS, k0 + k] = 1.0
            sel[k * FOX_HEADS + hd, k0 + 3 + k] = -1.0
    wkv = w_ukv.reshape(MLA_KV_LORA, MLA_HEADS, MLA_NOPE_DIM + MLA_V_DIM)
    wukv = jnp.concatenate([wkv[:, :, :MLA_NOPE_DIM].reshape(MLA_KV_LORA, -1),
                            wkv[:, :, MLA_NOPE_DIM:].reshape(MLA_KV_LORA, -1)], axis=1).astype(BF16)
    return dict(
        gmix=g_mix.reshape(1, -1), win=win, bf=bf, sel=jnp.asarray(sel, BF16), gq=g_q.reshape(1, -1), wuq=wuq, gkv=g_kv.reshape(1, -1),
        wukv=wukv, wo=w_o.astype(BF16), gffn=g_ffn.reshape(1, -1), wpq=w_pq.astype(BF16),
        k1=sub_keys[0].astype(BF16), k2=sub_keys[1].astype(BF16),
        u=peer_u.astype(BF16), v=peer_v.astype(BF16), gfin=g_final.reshape(1, -1))


def _ffn(x, o, w):
    B, S, D = x.shape
    T = B * S
    tm = min(256, T)
    tt = min(512, T)
    if S % tm:
        o = jnp.transpose(o, (1, 0, 2)).reshape(1, o.shape[1], T)
    x1, h2, rank, e2, cnt, w1 = _route(x.reshape(T, D), o, w, tm, tt)
    return _peer(x1, h2, rank, e2, cnt, w1, w, tt, 32).reshape(B, S, D)


def kernel(x_prompt, x_sample, cache_fox_k, cache_fox_v, cache_fox_logf, cache_mla_ckv, cache_mla_krope, g_mix, w_in, b_f, g_q, w_uq, g_kv, w_ukv, w_o, g_ffn, w_pq, peer_sub_keys, peer_u, peer_v, g_final):
    assert g_mix.shape[0] == 1, "single-layer model"
    w = _prep_weights(g_mix[0], w_in[0], b_f[0], g_q[0], w_uq[0], g_kv[0], w_ukv[0], w_o[0], g_ffn[0], w_pq[0],
                      peer_sub_keys[0], peer_u[0], peer_v[0], g_final)
    B, S, D = x_prompt.shape
    Bs, Ss, _ = x_sample.shape
    P = cache_fox_k.shape[2]

    tables_p = _rope_tables(jnp.arange(S, dtype=jnp.int32))
    tm = min(512, S)
    fk, fv, logf, ckv, kr, qh, kh, vh = _proj(x_prompt, jnp.zeros((B, 1, 128), F32), tables_p, w, tm)
    tq = min(256, S)
    o_p = _attention(qh, kh, vh, ATTN_HEAD_GROUP, tq, tq, 0, S)
    y_p = _ffn(x_prompt, o_p, w)

    kh_c, vh_c, c_last = _cache_rows(
        cache_fox_k[0].reshape(Bs, P, FOX_WIDTH), cache_fox_v[0].reshape(Bs, P, FOX_WIDTH),
        jnp.pad(cache_fox_logf[0], ((0, 0), (0, 0), (0, 128 - FOX_HEADS))),
        cache_mla_ckv[0], cache_mla_krope[0], w["wukv"])
    tables_s = _rope_tables(P + jnp.arange(Ss, dtype=jnp.int32))
    sfk, sfv, slogf, sckv, skr, sqh, skh, svh = _proj(x_sample, c_last, tables_s, w, Ss)
    tk = 256
    L = P + Ss
    pad = (-L) % tk
    kh_all = jnp.pad(jnp.concatenate([kh_c, skh], axis=2), ((0, 0), (0, 0), (0, pad), (0, 0)))
    vh_all = jnp.pad(jnp.concatenate([vh_c, svh], axis=2), ((0, 0), (0, 0), (0, pad), (0, 0)))
    o_s = _attention(sqh, kh_all, vh_all, ATTN_HEAD_GROUP, Ss, tk, P, L)
    y_s = _ffn(x_sample, o_s, w)

    return (y_p, y_s,
            fk.reshape(1, B, S, FOX_HEADS, FOX_HEAD_DIM), fv.reshape(1, B, S, FOX_HEADS, FOX_HEAD_DIM),
            logf[None], ckv[None], kr[None],
            sfk.reshape(1, Bs, Ss, FOX_HEADS, FOX_HEAD_DIM), sfv.reshape(1, Bs, Ss, FOX_HEADS, FOX_HEAD_DIM),
            slogf[None], sckv[None], skr[None])
```

```python
import functools
import math

import numpy as np
import jax
import jax.numpy as jnp
from jax import lax
from jax.experimental import pallas as pl
from jax.experimental.pallas import tpu as pltpu

F32 = jnp.float32
BF16 = jnp.bfloat16

EPS = 1e-6
CHUNK = 64
ROPE_BASE = 10000.0

FOX_HEADS = 8
FOX_HEAD_DIM = 64
FOX_WIDTH = FOX_HEADS * FOX_HEAD_DIM
MLA_HEADS = 8
MLA_NOPE_DIM = 64
MLA_ROPE_DIM = 32
MLA_V_DIM = 64
MLA_Q_LORA = 256
MLA_KV_LORA = 128
ATTN_HEADS = FOX_HEADS + MLA_HEADS
HEAD_ROW = 128
V_DIM = 64

PEER_HEADS = 8
PEER_N_KEYS = 128
PEER_HALF = 128
PEER_TOPK = 16
N_RANK = PEER_TOPK + 1
RANK_ROWS = 24
ROUTE_HEAD_GROUP = 4

Z_FQ = 0
Z_FK = Z_FQ + FOX_WIDTH
Z_FV = Z_FK + FOX_WIDTH
Z_CQ = Z_FV + FOX_WIDTH
Z_CKV = Z_CQ + MLA_Q_LORA
Z_TAIL = Z_CKV + MLA_KV_LORA
Z_TAILB = Z_TAIL + 128
Z_WIDTH = Z_TAILB + 128
ROPE_LANE = 64

NEG = -1e30
LOG2E = math.log2(math.e)
ATTN_HEAD_GROUP = 8
V7X_VMEM_LIMIT_BYTES = 56 * 1024 * 1024


def _params(sem, flags=None):
    return pltpu.CompilerParams(dimension_semantics=sem, vmem_limit_bytes=V7X_VMEM_LIMIT_BYTES, flags=flags)


def _rms(x, g):
    return x * lax.rsqrt(jnp.mean(x * x, axis=-1, keepdims=True) + EPS) * g


def _log_sigmoid(x):
    return jnp.minimum(x, 0.0) - jnp.log1p(jnp.exp(-jnp.abs(x)))


def _split3(x):
    hi = x.astype(BF16)
    r1 = x - hi.astype(F32)
    mid = r1.astype(BF16)
    lo = (r1 - mid.astype(F32)).astype(BF16)
    return hi, mid, lo


def _cumsum_rows(tri, x, carry):
    acc = carry
    for piece in _split3(x):
        acc = acc + jnp.dot(tri, piece, preferred_element_type=F32)
    return acc


def _fox_aux(c_col):
    hi, mid, lo = (p.astype(F32) for p in _split3(c_col * LOG2E))
    lane = lax.broadcasted_iota(jnp.int32, (1, HEAD_ROW - FOX_HEAD_DIM), 1)
    q_aux = jnp.where(lane == 0, hi, jnp.where(lane == 1, mid, jnp.where(
        lane == 2, lo, jnp.where(lane < 6, 1.0, 0.0))))
    k_aux = jnp.where(lane < 3, 1.0, jnp.where(lane == 3, -hi, jnp.where(
        lane == 4, -mid, jnp.where(lane == 5, -lo, 0.0))))
    return q_aux, k_aux


def _write_fox_kv(kh_ref, vh_ref, k, v, c_slab):
    rows = slice(0, k.shape[0])
    for h in range(FOX_HEADS):
        sl = slice(h * FOX_HEAD_DIM, (h + 1) * FOX_HEAD_DIM)
        _, k_aux = _fox_aux(c_slab[:, h:h + 1])
        kh_ref[0, h, rows, 0:FOX_HEAD_DIM] = k[:, sl].astype(BF16)
        kh_ref[0, h, rows, FOX_HEAD_DIM:HEAD_ROW] = k_aux.astype(BF16)
        vh_ref[0, h, rows, :] = v[:, sl].astype(BF16)


def _write_mla_kv(kh_ref, vh_ref, kv, krope):
    rows = slice(0, kv.shape[0])
    k_rope = krope.astype(BF16)
    pad = jnp.zeros((kv.shape[0], HEAD_ROW - MLA_NOPE_DIM - MLA_ROPE_DIM), BF16)
    for h in range(MLA_HEADS):
        sl = slice(h * MLA_NOPE_DIM, (h + 1) * MLA_NOPE_DIM)
        kh_ref[0, FOX_HEADS + h, rows, 0:MLA_NOPE_DIM] = kv[:, sl].astype(BF16)
        kh_ref[0, FOX_HEADS + h, rows, MLA_NOPE_DIM:MLA_NOPE_DIM + MLA_ROPE_DIM] = k_rope
        kh_ref[0, FOX_HEADS + h, rows, MLA_NOPE_DIM + MLA_ROPE_DIM:HEAD_ROW] = pad
        vsl = slice(MLA_HEADS * MLA_NOPE_DIM + h * MLA_V_DIM, MLA_HEADS * MLA_NOPE_DIM + (h + 1) * MLA_V_DIM)
        vh_ref[0, FOX_HEADS + h, rows, :] = kv[:, vsl].astype(BF16)


def _proj_body(x_ref, c0_ref, tri_ref, cos_ref, sin_ref, gmix_ref, win_ref, bf_ref, gq_ref, wuq_ref, gkv_ref,
               wukv_ref, sel_ref,
               fk_ref, fv_ref, logf_ref, ckv_ref, kr_ref, qh_ref, kh_ref, vh_ref, carry_ref):
    @pl.when(pl.program_id(1) == 0)
    def _():
        carry_ref[...] = c0_ref[0]

    rows = x_ref.shape[1]
    h = _rms(x_ref[0], gmix_ref[...]).astype(BF16)
    z = jnp.dot(h, win_ref[...], preferred_element_type=F32)

    fq = z[:, Z_FQ:Z_FK]
    fk = z[:, Z_FK:Z_FV]
    fv = z[:, Z_FV:Z_CQ]
    fk_ref[0] = fk
    fv_ref[0] = fv

    tail_a = z[:, Z_TAIL:Z_TAILB]
    tail_b = z[:, Z_TAILB:Z_WIDTH]
    logf = _log_sigmoid(tail_a + bf_ref[...])
    logf_ref[0] = logf[:, 0:FOX_HEADS]
    c_slab = _cumsum_rows(tri_ref[...], logf, carry_ref[...])
    carry_ref[...] = c_slab[rows - 1:rows, :]

    lane = lax.broadcasted_iota(jnp.int32, (1, 128), 1)
    hi, mid, lo = (p.astype(F32) for p in _split3(c_slab * LOG2E))
    pieces = jnp.where(lane < FOX_HEADS, hi, jnp.where(lane < 2 * FOX_HEADS, pltpu.roll(mid, FOX_HEADS, 1), jnp.where(
        lane < 3 * FOX_HEADS, pltpu.roll(lo, 2 * FOX_HEADS, 1), jnp.where(lane == 3 * FOX_HEADS, 1.0, 0.0))))
    aux = jnp.dot(pieces.astype(BF16), sel_ref[...], preferred_element_type=F32)

    cos = cos_ref[...]
    sin = sin_ref[...]
    krope = tail_a * cos + tail_b * sin
    kr_ref[0] = krope[:, ROPE_LANE:ROPE_LANE + MLA_ROPE_DIM]

    cqn = _rms(z[:, Z_CQ:Z_CKV], gq_ref[...]).astype(BF16)
    q = jnp.dot(cqn, wuq_ref[...], preferred_element_type=F32)
    ckv = _rms(z[:, Z_CKV:Z_TAIL], gkv_ref[...])
    ckv_ref[0] = ckv
    kv = jnp.dot(ckv.astype(BF16), wukv_ref[...], preferred_element_type=F32)

    fox_scale = FOX_HEAD_DIM ** -0.5 * LOG2E
    mla_scale = (MLA_NOPE_DIM + MLA_ROPE_DIM) ** -0.5 * LOG2E
    low = lane < FOX_HEAD_DIM
    nope_w = MLA_HEADS * MLA_NOPE_DIM
    row_w = MLA_HEADS * HEAD_ROW

    def head_src(block, hd):
        slab = block[:, (hd // 2) * 128:(hd // 2 + 1) * 128]
        return slab if hd % 2 == 0 else pltpu.roll(slab, 64, 1)

    for hd in range(FOX_HEADS):
        rs = slice(hd * HEAD_ROW, (hd + 1) * HEAD_ROW)
        ks = slice(row_w + hd * HEAD_ROW, row_w + (hd + 1) * HEAD_ROW)
        qh_ref[0, hd] = jnp.where(low, head_src(fq, hd) * fox_scale, aux[:, rs]).astype(BF16)
        kh_ref[0, hd] = jnp.where(low, head_src(fk, hd), aux[:, ks]).astype(BF16)
        vh_ref[0, hd] = fv[:, hd * V_DIM:(hd + 1) * V_DIM].astype(BF16)
    for hd in range(MLA_HEADS):
        rope = (q[:, nope_w + hd * HEAD_ROW:nope_w + (hd + 1) * HEAD_ROW] * cos
                + q[:, nope_w + row_w + hd * HEAD_ROW:nope_w + row_w + (hd + 1) * HEAD_ROW] * sin)
        qh_ref[0, FOX_HEADS + hd] = (jnp.where(low, head_src(q, hd), rope) * mla_scale).astype(BF16)
        kh_ref[0, FOX_HEADS + hd] = jnp.where(low, head_src(kv, hd), krope).astype(BF16)
        vh_ref[0, FOX_HEADS + hd] = kv[:, nope_w + hd * V_DIM:nope_w + (hd + 1) * V_DIM].astype(BF16)


def _proj(x, c0, tables, w, tm):
    B, S, D = x.shape
    cos, sin = tables
    tri = jnp.tril(jnp.ones((tm, tm), F32)).astype(BF16)
    const = lambda shape: pl.BlockSpec(shape, lambda b, t: (0,) * len(shape))
    tab = pl.BlockSpec((tm, 128), lambda b, t: (t, 0))
    in_specs = [
        pl.BlockSpec((1, tm, D), lambda b, t: (b, t, 0)),
        pl.BlockSpec((1, 1, 128), lambda b, t: (b, 0, 0)),
        const((tm, tm)), tab, tab,
        const((1, D)), const((D, Z_WIDTH)), const((1, 128)),
        const((1, MLA_Q_LORA)), const(w["wuq"].shape), const((1, MLA_KV_LORA)), const(w["wukv"].shape),
        const(w["sel"].shape),
    ]
    tok = lambda width: pl.BlockSpec((1, tm, width), lambda b, t: (b, t, 0))
    head = lambda width: pl.BlockSpec((1, ATTN_HEADS, tm, width), lambda b, t: (b, 0, t, 0))
    out_specs = [tok(FOX_WIDTH), tok(FOX_WIDTH), tok(FOX_HEADS), tok(MLA_KV_LORA), tok(MLA_ROPE_DIM),
                 head(HEAD_ROW), head(HEAD_ROW), head(V_DIM)]
    sds = jax.ShapeDtypeStruct
    out_shape = [sds((B, S, FOX_WIDTH), F32), sds((B, S, FOX_WIDTH), F32), sds((B, S, FOX_HEADS), F32),
                 sds((B, S, MLA_KV_LORA), F32), sds((B, S, MLA_ROPE_DIM), F32),
                 sds((B, ATTN_HEADS, S, HEAD_ROW), BF16), sds((B, ATTN_HEADS, S, HEAD_ROW), BF16),
                 sds((B, ATTN_HEADS, S, V_DIM), BF16)]
    return pl.pallas_call(
        _proj_body, grid=(B, S // tm), in_specs=in_specs, out_specs=out_specs, out_shape=out_shape,
        scratch_shapes=[pltpu.VMEM((1, 128), F32)],
        compiler_params=_params(("arbitrary", "arbitrary")), name="proj",
    )(x, c0, tri, cos, sin, w["gmix"], w["win"], w["bf"], w["gq"], w["wuq"], w["gkv"], w["wukv"], w["sel"])


def _cache_body(ck_ref, cv_ref, clogf_ref, cckv_ref, ckr_ref, tri_ref, wukv_ref, kh_ref, vh_ref, clast_ref):
    rows = ck_ref.shape[1]
    c_slab = _cumsum_rows(tri_ref[...], clogf_ref[0], jnp.zeros((1, 128), F32))
    clast_ref[0] = c_slab[rows - 1:rows, :]
    _write_fox_kv(kh_ref, vh_ref, ck_ref[0], cv_ref[0], c_slab)
    kv = jnp.dot(cckv_ref[0].astype(BF16), wukv_ref[...], preferred_element_type=F32)
    _write_mla_kv(kh_ref, vh_ref, kv, ckr_ref[0])
    kh_ref[0, :, rows:, :] = jnp.zeros((ATTN_HEADS, kh_ref.shape[2] - rows, HEAD_ROW), BF16)
    vh_ref[0, :, rows:, :] = jnp.zeros((ATTN_HEADS, vh_ref.shape[2] - rows, V_DIM), BF16)


def _cache_rows(ck, cv, clogf_slab, cckv, ckr, wukv, total_rows):
    B, P, _ = ck.shape
    tri = jnp.tril(jnp.ones((P, P), F32)).astype(BF16)
    per_b = lambda width: pl.BlockSpec((1, P, width), lambda b: (b, 0, 0))
    const = lambda shape: pl.BlockSpec(shape, lambda b: (0,) * len(shape))
    head = lambda width: pl.BlockSpec((1, ATTN_HEADS, total_rows, width), lambda b: (b, 0, 0, 0))
    sds = jax.ShapeDtypeStruct
    return pl.pallas_call(
        _cache_body, grid=(B,),
        in_specs=[per_b(FOX_WIDTH), per_b(FOX_WIDTH), per_b(128), per_b(MLA_KV_LORA), per_b(MLA_ROPE_DIM),
                  const((P, P)), const(wukv.shape)],
        out_specs=[head(HEAD_ROW), head(V_DIM), pl.BlockSpec((1, 1, 128), lambda b: (b, 0, 0))],
        out_shape=[sds((B, ATTN_HEADS, total_rows, HEAD_ROW), BF16), sds((B, ATTN_HEADS, total_rows, V_DIM), BF16),
                   sds((B, 1, 128), F32)],
        compiler_params=_params(("arbitrary",)), name="cache_rows",
    )(ck, cv, clogf_slab, cckv, ckr, tri, wukv)


def _reduce_rows(pair_op, final_op, x):
    while x.shape[0] > 8 and x.shape[0] % 16 == 0:
        half = x.shape[0] // 2
        x = pair_op(x[:half], x[half:])
    return final_op(x, axis=0, keepdims=True)


def _attn_body(q_ref, k_ref, v_ref, o_ref, *, hg, tq, tk, nkv_total, q_off, kv_len):
    grp = pl.program_id(1)
    qi = pl.program_id(2)
    q_lo = q_off + qi * tq
    qpos = q_lo + lax.broadcasted_iota(jnp.int32, (1, tq), 1)
    lim = jnp.where(grp >= FOX_HEADS // hg, qpos | (CHUNK - 1), qpos)
    lim = jnp.minimum(lim, kv_len - 1)
    n_full = jnp.minimum(lax.div(q_lo + 1, tk), nkv_total)
    n_all = jnp.minimum(lax.div((q_lo + tq - 1) | (CHUNK - 1), tk) + 1, nkv_total)

    def step(j, carry, masked):
        off = pl.multiple_of(j * tk, tk)
        if masked:
            visible = off + lax.broadcasted_iota(jnp.int32, (tk, 1), 0) <= lim
        out = []
        scores = [lax.dot_general(k_ref[0, hh, pl.ds(off, tk), :], q_ref[0, hh], (((1,), (1,)), ((), ())),
                                  preferred_element_type=F32) for hh in range(hg)]
        for hh in range(hg):
            m, l, acc = carry[3 * hh:3 * hh + 3]
            vb = v_ref[0, hh, pl.ds(off, tk), :]
            s = scores[hh]
            if masked:
                s = jnp.where(visible, s, NEG)
            m_new = jnp.maximum(m, _reduce_rows(jnp.maximum, jnp.max, s))
            alpha = jnp.exp2(m - m_new)
            p = jnp.exp2(s - m_new)
            l = alpha * l + _reduce_rows(jnp.add, jnp.sum, p)
            pv = lax.dot_general(vb, p.astype(BF16), (((0,), (0,)), ((), ())), preferred_element_type=F32)
            out += [m_new, l, alpha * acc + pv]
        return tuple(out)

    init = (jnp.full((1, tq), NEG, F32), jnp.zeros((1, tq), F32), jnp.zeros((V_DIM, tq), F32)) * hg
    carry = lax.fori_loop(0, n_full, functools.partial(step, masked=False), init)
    carry = lax.fori_loop(n_full, n_all, functools.partial(step, masked=True), carry)
    o_ref[0] = jnp.concatenate([carry[3 * hh + 2] / carry[3 * hh + 1] for hh in range(hg)], axis=0).astype(BF16)


def _attention(qh, kh, vh, hg, tq, tk, q_off, kv_len):
    B, _, Tq, _ = qh.shape
    Tk = kh.shape[2]
    body = functools.partial(_attn_body, hg=hg, tq=tq, tk=tk, nkv_total=Tk // tk, q_off=q_off, kv_len=kv_len)
    return pl.pallas_call(
        body, grid=(B, ATTN_HEADS // hg, Tq // tq),
        in_specs=[pl.BlockSpec((1, hg, tq, HEAD_ROW), lambda b, g, i: (b, g, i, 0)),
                  pl.BlockSpec((1, hg, Tk, HEAD_ROW), lambda b, g, i: (b, g, 0, 0)),
                  pl.BlockSpec((1, hg, Tk, V_DIM), lambda b, g, i: (b, g, 0, 0))],
        out_specs=pl.BlockSpec((1, hg * V_DIM, tq), lambda b, g, i: (b, g, i)),
        out_shape=jax.ShapeDtypeStruct((B, ATTN_HEADS * V_DIM, Tq), BF16),
        compiler_params=_params(("arbitrary", "arbitrary", "arbitrary")), name="attn",
    )(qh, kh, vh)


def _sorting_network(n):
    size = 1 << (n - 1).bit_length()

    def merge(lo, hi, r):
        step = r * 2
        if step < hi - lo:
            yield from merge(lo, hi, step)
            yield from merge(lo + r, hi, step)
            yield from ((i, i + r) for i in range(lo + r, hi - r, step))
        else:
            yield (lo, lo + r)

    def sort(lo, hi):
        if hi - lo >= 1:
            mid = lo + (hi - lo) // 2
            yield from sort(lo, mid)
            yield from sort(mid + 1, hi)
            yield from merge(lo, hi, 1)

    return tuple((i, j) for i, j in sort(0, size - 1) if j < n)


def _largest(groups, n_out):
    g = list(groups)
    for i, j in _sorting_network(len(g)):
        g[i], g[j] = jnp.maximum(g[i], g[j]), jnp.minimum(g[i], g[j])
    below = jnp.full(g[0].shape, -jnp.inf, F32)
    out = []
    for r in range(n_out):
        m = jnp.max(g[0], axis=0, keepdims=True)
        out.append(m)
        hit = g[0] == m
        for k in range(min(n_out - 1 - r, len(g))):
            g[k] = jnp.where(hit, g[k + 1] if k + 1 < len(g) else below, g[k])
    return out


def _top_ranks(work, rows_ref, want_rank):
    rows_ref[...] = jnp.full(rows_ref.shape, -jnp.inf, F32)
    for r, m in enumerate(_largest([work[8 * k:8 * (k + 1), :] for k in range(work.shape[0] // 8)], N_RANK)):
        rows_ref[r:r + 1, :] = m
    if not want_rank:
        return None
    rank = jnp.zeros(work.shape, F32)
    for r in range(N_RANK):
        rank = rank + jnp.where(rows_ref[r:r + 1, :] > work, 1.0, 0.0)
    return rank


def _route_body(x_ref, o_ref, wo_ref, gffn_ref, wpq_ref, k1_ref, k2_ref,
                x1_ref, h2_ref, rank_ref, e2_ref, cnt_ref, w1_ref, qp_ref, *rank_rows):
    x1 = x_ref[...] + lax.dot_general(o_ref[0], wo_ref[...], (((0,), (0,)), ((), ())), preferred_element_type=F32)
    x1_ref[...] = x1
    h2 = _rms(x1, gffn_ref[...]).astype(BF16)
    h2_ref[...] = h2
    qp_ref[...] = jnp.dot(h2, wpq_ref[...], preferred_element_type=F32)
    key_dim = 2 * PEER_HALF

    def head(h, a_ref, b_ref):
        qh = qp_ref[:, pl.ds(pl.multiple_of(h * key_dim, key_dim), key_dim)]
        qn = qh * lax.rsqrt(jnp.mean(qh * qh, axis=-1, keepdims=True) + EPS)
        nt = (((1,), (1,)), ((), ()))
        s1 = lax.dot_general(k1_ref[h], qn[:, :PEER_HALF].astype(BF16), nt, preferred_element_type=F32)
        s2 = lax.dot_general(k2_ref[h], qn[:, PEER_HALF:].astype(BF16), nt, preferred_element_type=F32)
        _top_ranks(s1, a_ref, False)
        rank = _top_ranks(s2, b_ref, True)
        groups = [a_ref[0:1, :] + b_ref[8 * k:8 * (k + 1), :] for k in range(RANK_ROWS // 8)]
        groups += [a_ref[r:r + 1, :] + b_ref[0:8, :] for r in range(1, 8)]
        groups += [a_ref[8 * k:8 * (k + 1), :] + b_ref[0:1, :] for k in range(1, RANK_ROWS // 8)]
        pair_scores = _largest(groups, N_RANK)
        top = pair_scores[0]
        zsum = jnp.zeros_like(top)
        for v in pair_scores[:PEER_TOPK]:
            zsum = zsum + jnp.exp(v - top)
        tau = 0.5 * (pair_scores[PEER_TOPK - 1] + pair_scores[PEER_TOPK])
        thr = tau - s1
        cnt = jnp.zeros_like(s1)
        for c in range(N_RANK):
            cnt = cnt + jnp.where(b_ref[c:c + 1, :] >= thr, 1.0, 0.0)
        rank_ref[0, h] = rank.astype(BF16)
        e2_ref[0, h] = jnp.exp(s2 - b_ref[0:1, :]).astype(BF16)
        cnt_ref[0, h] = cnt
        w1_ref[0, h] = (0.5 * jnp.exp(s1 - a_ref[0:1, :])) / zsum

    group = len(rank_rows) // 2

    def head_group(p, carry):
        for k in range(group):
            head(group * p + k, rank_rows[2 * k], rank_rows[2 * k + 1])
        return carry

    lax.fori_loop(0, PEER_HEADS // group, head_group, 0)


def _route(x2d, o_t, w, tm, tt):
    T, D = x2d.shape
    per_seq = o_t.shape[2] // tm
    r = tt // tm
    kq = w["wpq"].shape[1]
    const = lambda shape: pl.BlockSpec(shape, lambda i: (0,) * len(shape))
    lane_out = pl.BlockSpec((1, PEER_HEADS, PEER_N_KEYS, tm), lambda i: (i // r, 0, 0, i % r))
    sds = jax.ShapeDtypeStruct
    lane_shape = lambda dtype: sds((T // tt, PEER_HEADS, PEER_N_KEYS, tt), dtype)
    return pl.pallas_call(
        _route_body, grid=(T // tm,),
        in_specs=[pl.BlockSpec((tm, D), lambda i: (i, 0)),
                  pl.BlockSpec((1, o_t.shape[1], tm), lambda i: (i // per_seq, 0, i % per_seq)),
                  const((D, D)), const((1, D)), const((D, kq)),
                  const((PEER_HEADS, PEER_N_KEYS, PEER_HALF)), const((PEER_HEADS, PEER_N_KEYS, PEER_HALF))],
        out_specs=[pl.BlockSpec((tm, D), lambda i: (i, 0)), pl.BlockSpec((tm, D), lambda i: (i, 0)),
                   lane_out, lane_out, lane_out, lane_out],
        out_shape=[sds((T, D), F32), sds((T, D), BF16),
                   lane_shape(BF16), lane_shape(BF16), lane_shape(F32), lane_shape(F32)],
        scratch_shapes=[pltpu.VMEM((tm, kq), F32)] + [pltpu.VMEM((RANK_ROWS, tm), F32)] * (2 * ROUTE_HEAD_GROUP),
        compiler_params=_params(("arbitrary",)), name="route",
    )(x2d, o_t, w["wo"], w["gffn"], w["wpq"], w["k1"], w["k2"])


GATE_ROWS = 32
GATE_LANES = 256
CHUNK_I = 2
OUT_CHUNKS = 1


def _row_tile(ref, h, i, ls):
    tile = jnp.broadcast_to(ref[0, h, i:i + 1, ls], (8, GATE_LANES)).astype(BF16)
    return jnp.concatenate([tile] * (GATE_ROWS // 8), axis=0)


def _peer_body(x1_ref, h2_ref, rank_ref, e2_ref, cnt_ref, w1_ref, u_ref, v_ref, gfin_ref, y_ref,
               acc_ref, *chunk_refs, tt, rows_per_step):
    j = pl.program_id(1)

    @pl.when(j == 0)
    def _():
        acc_ref[...] = jnp.zeros_like(acc_ref)

    n_chunks = len(chunk_refs) * OUT_CHUNKS // (OUT_CHUNKS + 1)
    a_refs, act_refs = chunk_refs[:n_chunks], chunk_refs[n_chunks:]
    inv_sqrt2 = 1.0 / math.sqrt(2.0)
    rows = CHUNK_I * PEER_N_KEYS

    def score_matmul(c):
        cs = slice(c * rows, (c + 1) * rows)
        a_refs[c][...] = lax.dot_general(u_ref[cs, :], h2_ref[...], (((1,), (1,)), ((), ())),
                                         preferred_element_type=F32)

    score_matmul(0)
    score_matmul(1)
    for c in range(n_chunks):
        if c + 2 < n_chunks:
            score_matmul(c + 2)
        act_ref = act_refs[c // OUT_CHUNKS]
        act_row = (c % OUT_CHUNKS) * rows
        a = a_refs[c]
        for k in range(CHUNK_I):
            i = c * CHUNK_I + k
            for tc in range(tt // GATE_LANES):
                ls = slice(tc * GATE_LANES, (tc + 1) * GATE_LANES)
                cnt = [_row_tile(cnt_ref, h, i, ls) for h in range(PEER_HEADS)]
                w1 = [_row_tile(w1_ref, h, i, ls) for h in range(PEER_HEADS)]
                for jb in range(PEER_N_KEYS // GATE_ROWS):
                    js = slice(jb * GATE_ROWS, (jb + 1) * GATE_ROWS)
                    gate = jnp.zeros((GATE_ROWS, GATE_LANES), BF16)
                    for h in range(PEER_HEADS):
                        sel = rank_ref[0, h, js, ls] < cnt[h]
                        gate = gate + jnp.where(sel, e2_ref[0, h, js, ls] * w1[h], 0.0)
                    r0 = k * PEER_N_KEYS + jb * GATE_ROWS
                    av = a[r0:r0 + GATE_ROWS, ls]
                    gelu2 = av * (1.0 + lax.erf(av * inv_sqrt2))
                    act_ref[act_row + r0:act_row + r0 + GATE_ROWS, ls] = gelu2.astype(BF16) * gate
        if c % OUT_CHUNKS == OUT_CHUNKS - 1:
            vs = slice((c + 1 - OUT_CHUNKS) * rows, (c + 1) * rows)
            acc_ref[...] += lax.dot_general(act_ref[...], v_ref[vs, :], (((0,), (0,)), ((), ())),
                                            preferred_element_type=F32)

    @pl.when(j == pl.num_programs(1) - 1)
    def _():
        y_ref[...] = _rms(x1_ref[...] + acc_ref[...], gfin_ref[...])


def _peer(x1, h2, rank, e2, cnt, w1, w, tt, rows_per_step):
    T, D = x1.shape
    n_exp = w["v"].shape[0]
    eb = rows_per_step * PEER_N_KEYS
    lane_in = pl.BlockSpec((1, PEER_HEADS, PEER_N_KEYS, tt), lambda t, j: (t, 0, 0, 0))
    row_in = pl.BlockSpec((1, PEER_HEADS, rows_per_step, tt), lambda t, j: (t, 0, j, 0))
    body = functools.partial(_peer_body, tt=tt, rows_per_step=rows_per_step)
    n_chunks = rows_per_step // CHUNK_I
    a_chunks = [pltpu.VMEM((CHUNK_I * PEER_N_KEYS, tt), F32)] * n_chunks
    act_chunks = [pltpu.VMEM((OUT_CHUNKS * CHUNK_I * PEER_N_KEYS, tt), BF16)] * (n_chunks // OUT_CHUNKS)
    return pl.pallas_call(
        body, grid=(T // tt, n_exp // eb),
        in_specs=[pl.BlockSpec((tt, D), lambda t, j: (t, 0)), pl.BlockSpec((tt, D), lambda t, j: (t, 0)),
                  lane_in, lane_in, row_in, row_in,
                  pl.BlockSpec((eb, D), lambda t, j: (j, 0)), pl.BlockSpec((eb, D), lambda t, j: (j, 0)),
                  pl.BlockSpec((1, D), lambda t, j: (0, 0))],
        out_specs=pl.BlockSpec((tt, D), lambda t, j: (t, 0)),
        out_shape=jax.ShapeDtypeStruct((T, D), F32),
        scratch_shapes=[pltpu.VMEM((tt, D), F32)] + a_chunks + act_chunks,
        compiler_params=_params(("arbitrary", "arbitrary")), name="peer",
    )(x1, h2, rank, e2, cnt, w1, w["u"], w["v"], w["gfin"])


def _rope_tables(pos):
    half = MLA_ROPE_DIM // 2
    inv = ROPE_BASE ** (-jnp.arange(half, dtype=F32) / half)
    ang = pos.astype(F32)[:, None] * inv[None, :]
    pad = ((0, 0), (ROPE_LANE, HEAD_ROW - ROPE_LANE - MLA_ROPE_DIM))
    return (jnp.pad(jnp.concatenate([jnp.cos(ang)] * 2, axis=-1), pad),
            jnp.pad(jnp.concatenate([jnp.sin(ang)] * 2, axis=-1), pad))


def _rot_cols(wcols):
    half = wcols.shape[-1] // 2
    return jnp.concatenate([-wcols[..., half:], wcols[..., :half]], axis=-1)


def _prep_weights(g_mix, w_in, b_f, g_q, w_uq, g_kv, w_ukv, w_o, g_ffn, w_pq, sub_keys, peer_u, peer_v, g_final):
    D = w_in.shape[0]
    off_ff = 3 * FOX_WIDTH
    off_cq = off_ff + FOX_HEADS
    off_ckv = off_cq + MLA_Q_LORA
    off_kr = off_ckv + MLA_KV_LORA
    kr_cols = w_in[:, off_kr:off_kr + MLA_ROPE_DIM]
    zeros = lambda n: jnp.zeros((D, n), F32)
    rope_tail = zeros(HEAD_ROW - ROPE_LANE - MLA_ROPE_DIM)
    tail_a = [w_in[:, off_ff:off_cq], zeros(ROPE_LANE - FOX_HEADS), kr_cols, rope_tail]
    tail_b = [zeros(ROPE_LANE), _rot_cols(kr_cols), rope_tail]
    win = jnp.concatenate([w_in[:, :off_ff], w_in[:, off_cq:off_kr]] + tail_a + tail_b, axis=1).astype(BF16)
    bf = jnp.pad(b_f, (0, 128 - FOX_HEADS)).reshape(1, 128)
    per_q = MLA_NOPE_DIM + MLA_ROPE_DIM
    wq = w_uq.reshape(MLA_Q_LORA, MLA_HEADS, per_q)
    q_rope = wq[:, :, MLA_NOPE_DIM:]
    in_row = lambda cols: jnp.pad(cols, ((0, 0), (0, 0), (ROPE_LANE, HEAD_ROW - ROPE_LANE - MLA_ROPE_DIM)))
    wuq = jnp.concatenate([wq[:, :, :MLA_NOPE_DIM].reshape(MLA_Q_LORA, -1), in_row(q_rope).reshape(MLA_Q_LORA, -1),
                           in_row(_rot_cols(q_rope)).reshape(MLA_Q_LORA, -1)], axis=1).astype(BF16)
    sel = np.zeros((128, 2 * FOX_HEADS * HEAD_ROW), np.float32)
    for hd in range(FOX_HEADS):
        q0, k0 = hd * HEAD_ROW + FOX_HEAD_DIM, (FOX_HEADS + hd) * HEAD_ROW + FOX_HEAD_DIM
        for k in range(3):
            sel[k * FOX_HEADS + hd, q0 + k] = 1.0
            sel[3 * FOX_HEADS, q0 + 3 + k] = 1.0
            sel[3 * FOX_HEADS, k0 + k] = 1.0
            sel[k * FOX_HEADS + hd, k0 + 3 + k] = -1.0
    wkv = w_ukv.reshape(MLA_KV_LORA, MLA_HEADS, MLA_NOPE_DIM + MLA_V_DIM)
    wukv = jnp.concatenate([wkv[:, :, :MLA_NOPE_DIM].reshape(MLA_KV_LORA, -1),
                            wkv[:, :, MLA_NOPE_DIM:].reshape(MLA_KV_LORA, -1)], axis=1).astype(BF16)
    return dict(
        gmix=g_mix.reshape(1, -1), win=win, bf=bf, sel=jnp.asarray(sel, BF16), gq=g_q.reshape(1, -1), wuq=wuq, gkv=g_kv.reshape(1, -1),
        wukv=wukv, wo=w_o.astype(BF16), gffn=g_ffn.reshape(1, -1), wpq=w_pq.astype(BF16),
        k1=sub_keys[0].astype(BF16), k2=sub_keys[1].astype(BF16),
        u=peer_u.astype(BF16), v=peer_v.astype(BF16), gfin=g_final.reshape(1, -1))


def _ffn(x, o, w):
    B, S, D = x.shape
    T = B * S
    tm = min(256, T)
    tt = min(512, T)
    if S % tm:
        o = jnp.transpose(o, (1, 0, 2)).reshape(1, o.shape[1], T)
    x1, h2, rank, e2, cnt, w1 = _route(x.reshape(T, D), o, w, tm, tt)
    return _peer(x1, h2, rank, e2, cnt, w1, w, tt, 32).reshape(B, S, D)


def kernel(x_prompt, x_sample, cache_fox_k, cache_fox_v, cache_fox_logf, cache_mla_ckv, cache_mla_krope, g_mix, w_in, b_f, g_q, w_uq, g_kv, w_ukv, w_o, g_ffn, w_pq, peer_sub_keys, peer_u, peer_v, g_final):
    assert g_mix.shape[0] == 1, "single-layer model"
    w = _prep_weights(g_mix[0], w_in[0], b_f[0], g_q[0], w_uq[0], g_kv[0], w_ukv[0], w_o[0], g_ffn[0], w_pq[0],
                      peer_sub_keys[0], peer_u[0], peer_v[0], g_final)
    B, S, D = x_prompt.shape
    Bs, Ss, _ = x_sample.shape
    P = cache_fox_k.shape[2]

    tables_p = _rope_tables(jnp.arange(S, dtype=jnp.int32))
    tm = min(512, S)
    fk, fv, logf, ckv, kr, qh, kh, vh = _proj(x_prompt, jnp.zeros((B, 1, 128), F32), tables_p, w, tm)
    tq = min(256, S)
    o_p = _attention(qh, kh, vh, ATTN_HEAD_GROUP, tq, tq, 0, S)
    y_p = _ffn(x_prompt, o_p, w)

    tk = 256
    L = P + Ss
    kh_c, vh_c, c_last = _cache_rows(
        cache_fox_k[0].reshape(Bs, P, FOX_WIDTH), cache_fox_v[0].reshape(Bs, P, FOX_WIDTH),
        jnp.pad(cache_fox_logf[0], ((0, 0), (0, 0), (0, 128 - FOX_HEADS))),
        cache_mla_ckv[0], cache_mla_krope[0], w["wukv"], L + (-L) % tk)
    tables_s = _rope_tables(P + jnp.arange(Ss, dtype=jnp.int32))
    sfk, sfv, slogf, sckv, skr, sqh, skh, svh = _proj(x_sample, c_last, tables_s, w, Ss)
    kh_all = lax.dynamic_update_slice(kh_c, skh, (0, 0, P, 0))
    vh_all = lax.dynamic_update_slice(vh_c, svh, (0, 0, P, 0))
    o_s = _attention(sqh, kh_all, vh_all, ATTN_HEAD_GROUP, Ss, tk, P, L)
    y_s = _ffn(x_sample, o_s, w)

    return (y_p, y_s,
            fk.reshape(1, B, S, FOX_HEADS, FOX_HEAD_DIM), fv.reshape(1, B, S, FOX_HEADS, FOX_HEAD_DIM),
            logf[None], ckv[None], kr[None],
            sfk.reshape(1, Bs, Ss, FOX_HEADS, FOX_HEAD_DIM), sfv.reshape(1, Bs, Ss, FOX_HEADS, FOX_HEAD_DIM),
            slogf[None], sckv[None], skr[None])
```

```python
import functools
import math

import numpy as np
import jax
import jax.numpy as jnp
from jax import lax
from jax.experimental import pallas as pl
from jax.experimental.pallas import tpu as pltpu

F32 = jnp.float32
BF16 = jnp.bfloat16

EPS = 1e-6
CHUNK = 64
ROPE_BASE = 10000.0

FOX_HEADS = 8
FOX_HEAD_DIM = 64
FOX_WIDTH = FOX_HEADS * FOX_HEAD_DIM
MLA_HEADS = 8
MLA_NOPE_DIM = 64
MLA_ROPE_DIM = 32
MLA_V_DIM = 64
MLA_Q_LORA = 256
MLA_KV_LORA = 128
ATTN_HEADS = FOX_HEADS + MLA_HEADS
HEAD_ROW = 128
V_DIM = 64

PEER_HEADS = 8
PEER_N_KEYS = 128
PEER_HALF = 128
PEER_TOPK = 16
N_RANK = PEER_TOPK + 1
RANK_ROWS = 24
ROUTE_HEAD_GROUP = 4

Z_FQ = 0
Z_FK = Z_FQ + FOX_WIDTH
Z_FV = Z_FK + FOX_WIDTH
Z_CQ = Z_FV + FOX_WIDTH
Z_CKV = Z_CQ + MLA_Q_LORA
Z_TAIL = Z_CKV + MLA_KV_LORA
Z_TAILB = Z_TAIL + 128
Z_WIDTH = Z_TAILB + 128
ROPE_LANE = 64

NEG = -1e30
LOG2E = math.log2(math.e)
ATTN_HEAD_GROUP = 8
V7X_VMEM_LIMIT_BYTES = 56 * 1024 * 1024


def _params(sem, flags=None):
    return pltpu.CompilerParams(dimension_semantics=sem, vmem_limit_bytes=V7X_VMEM_LIMIT_BYTES, flags=flags)


def _rms(x, g):
    return x * lax.rsqrt(jnp.mean(x * x, axis=-1, keepdims=True) + EPS) * g


def _log_sigmoid(x):
    return jnp.minimum(x, 0.0) - jnp.log1p(jnp.exp(-jnp.abs(x)))


def _split3(x):
    hi = x.astype(BF16)
    r1 = x - hi.astype(F32)
    mid = r1.astype(BF16)
    lo = (r1 - mid.astype(F32)).astype(BF16)
    return hi, mid, lo


def _cumsum_rows(tri, x, carry):
    acc = carry
    for piece in _split3(x):
        acc = acc + jnp.dot(tri, piece, preferred_element_type=F32)
    return acc


def _fox_aux(c_col):
    hi, mid, lo = (p.astype(F32) for p in _split3(c_col * LOG2E))
    lane = lax.broadcasted_iota(jnp.int32, (1, HEAD_ROW - FOX_HEAD_DIM), 1)
    q_aux = jnp.where(lane == 0, hi, jnp.where(lane == 1, mid, jnp.where(
        lane == 2, lo, jnp.where(lane < 6, 1.0, 0.0))))
    k_aux = jnp.where(lane < 3, 1.0, jnp.where(lane == 3, -hi, jnp.where(
        lane == 4, -mid, jnp.where(lane == 5, -lo, 0.0))))
    return q_aux, k_aux


def _write_fox_kv(kh_ref, vh_ref, k, v, c_slab):
    rows = slice(0, k.shape[0])
    for h in range(FOX_HEADS):
        sl = slice(h * FOX_HEAD_DIM, (h + 1) * FOX_HEAD_DIM)
        _, k_aux = _fox_aux(c_slab[:, h:h + 1])
        kh_ref[0, h, rows, 0:FOX_HEAD_DIM] = k[:, sl].astype(BF16)
        kh_ref[0, h, rows, FOX_HEAD_DIM:HEAD_ROW] = k_aux.astype(BF16)
        vh_ref[0, h, rows, :] = v[:, sl].astype(BF16)


def _write_mla_kv(kh_ref, vh_ref, kv, krope):
    rows = slice(0, kv.shape[0])
    k_rope = krope.astype(BF16)
    pad = jnp.zeros((kv.shape[0], HEAD_ROW - MLA_NOPE_DIM - MLA_ROPE_DIM), BF16)
    for h in range(MLA_HEADS):
        sl = slice(h * MLA_NOPE_DIM, (h + 1) * MLA_NOPE_DIM)
        kh_ref[0, FOX_HEADS + h, rows, 0:MLA_NOPE_DIM] = kv[:, sl].astype(BF16)
        kh_ref[0, FOX_HEADS + h, rows, MLA_NOPE_DIM:MLA_NOPE_DIM + MLA_ROPE_DIM] = k_rope
        kh_ref[0, FOX_HEADS + h, rows, MLA_NOPE_DIM + MLA_ROPE_DIM:HEAD_ROW] = pad
        vsl = slice(MLA_HEADS * MLA_NOPE_DIM + h * MLA_V_DIM, MLA_HEADS * MLA_NOPE_DIM + (h + 1) * MLA_V_DIM)
        vh_ref[0, FOX_HEADS + h, rows, :] = kv[:, vsl].astype(BF16)


def _proj_body(x_ref, c0_ref, tri_ref, cos_ref, sin_ref, gmix_ref, win_ref, bf_ref, gq_ref, wuq_ref, gkv_ref,
               wukv_ref, sel_ref,
               fk_ref, fv_ref, logf_ref, ckv_ref, kr_ref, qh_ref, kh_ref, vh_ref, carry_ref):
    @pl.when(pl.program_id(1) == 0)
    def _():
        carry_ref[...] = c0_ref[0]

    rows = x_ref.shape[1]
    h = _rms(x_ref[0], gmix_ref[...]).astype(BF16)
    z = jnp.dot(h, win_ref[...], preferred_element_type=F32)

    fq = z[:, Z_FQ:Z_FK]
    fk = z[:, Z_FK:Z_FV]
    fv = z[:, Z_FV:Z_CQ]
    fk_ref[0] = fk
    fv_ref[0] = fv

    tail_a = z[:, Z_TAIL:Z_TAILB]
    tail_b = z[:, Z_TAILB:Z_WIDTH]
    logf = _log_sigmoid(tail_a + bf_ref[...])
    logf_ref[0] = logf[:, 0:FOX_HEADS]
    c_slab = _cumsum_rows(tri_ref[...], logf, carry_ref[...])
    carry_ref[...] = c_slab[rows - 1:rows, :]

    lane = lax.broadcasted_iota(jnp.int32, (1, 128), 1)
    hi, mid, lo = (p.astype(F32) for p in _split3(c_slab * LOG2E))
    pieces = jnp.where(lane < FOX_HEADS, hi, jnp.where(lane < 2 * FOX_HEADS, pltpu.roll(mid, FOX_HEADS, 1), jnp.where(
        lane < 3 * FOX_HEADS, pltpu.roll(lo, 2 * FOX_HEADS, 1), jnp.where(lane == 3 * FOX_HEADS, 1.0, 0.0))))
    aux = jnp.dot(pieces.astype(BF16), sel_ref[...], preferred_element_type=F32)

    cos = cos_ref[...]
    sin = sin_ref[...]
    krope = tail_a * cos + tail_b * sin
    kr_ref[0] = krope[:, ROPE_LANE:ROPE_LANE + MLA_ROPE_DIM]

    cqn = _rms(z[:, Z_CQ:Z_CKV], gq_ref[...]).astype(BF16)
    q = jnp.dot(cqn, wuq_ref[...], preferred_element_type=F32)
    ckv = _rms(z[:, Z_CKV:Z_TAIL], gkv_ref[...])
    ckv_ref[0] = ckv
    kv = jnp.dot(ckv.astype(BF16), wukv_ref[...], preferred_element_type=F32)

    fox_scale = FOX_HEAD_DIM ** -0.5 * LOG2E
    mla_scale = (MLA_NOPE_DIM + MLA_ROPE_DIM) ** -0.5 * LOG2E
    low = lane < FOX_HEAD_DIM
    nope_w = MLA_HEADS * MLA_NOPE_DIM
    row_w = MLA_HEADS * HEAD_ROW

    def head_src(block, hd):
        slab = block[:, (hd // 2) * 128:(hd // 2 + 1) * 128]
        return slab if hd % 2 == 0 else pltpu.roll(slab, 64, 1)

    for hd in range(FOX_HEADS):
        rs = slice(hd * HEAD_ROW, (hd + 1) * HEAD_ROW)
        ks = slice(row_w + hd * HEAD_ROW, row_w + (hd + 1) * HEAD_ROW)
        qh_ref[0, hd] = jnp.where(low, head_src(fq, hd) * fox_scale, aux[:, rs]).astype(BF16)
        kh_ref[0, hd] = jnp.where(low, head_src(fk, hd), aux[:, ks]).astype(BF16)
        vh_ref[0, hd] = fv[:, hd * V_DIM:(hd + 1) * V_DIM].astype(BF16)
    for hd in range(MLA_HEADS):
        rope = (q[:, nope_w + hd * HEAD_ROW:nope_w + (hd + 1) * HEAD_ROW] * cos
                + q[:, nope_w + row_w + hd * HEAD_ROW:nope_w + row_w + (hd + 1) * HEAD_ROW] * sin)
        qh_ref[0, FOX_HEADS + hd] = (jnp.where(low, head_src(q, hd), rope) * mla_scale).astype(BF16)
        kh_ref[0, FOX_HEADS + hd] = jnp.where(low, head_src(kv, hd), krope).astype(BF16)
        vh_ref[0, FOX_HEADS + hd] = kv[:, nope_w + hd * V_DIM:nope_w + (hd + 1) * V_DIM].astype(BF16)


def _proj(x, c0, tables, w, tm):
    B, S, D = x.shape
    cos, sin = tables
    tri = jnp.tril(jnp.ones((tm, tm), F32)).astype(BF16)
    const = lambda shape: pl.BlockSpec(shape, lambda b, t: (0,) * len(shape))
    tab = pl.BlockSpec((tm, 128), lambda b, t: (t, 0))
    in_specs = [
        pl.BlockSpec((1, tm, D), lambda b, t: (b, t, 0)),
        pl.BlockSpec((1, 1, 128), lambda b, t: (b, 0, 0)),
        const((tm, tm)), tab, tab,
        const((1, D)), const((D, Z_WIDTH)), const((1, 128)),
        const((1, MLA_Q_LORA)), const(w["wuq"].shape), const((1, MLA_KV_LORA)), const(w["wukv"].shape),
        const(w["sel"].shape),
    ]
    tok = lambda width: pl.BlockSpec((1, tm, width), lambda b, t: (b, t, 0))
    head = lambda width: pl.BlockSpec((1, ATTN_HEADS, tm, width), lambda b, t: (b, 0, t, 0))
    out_specs = [tok(FOX_WIDTH), tok(FOX_WIDTH), tok(FOX_HEADS), tok(MLA_KV_LORA), tok(MLA_ROPE_DIM),
                 head(HEAD_ROW), head(HEAD_ROW), head(V_DIM)]
    sds = jax.ShapeDtypeStruct
    out_shape = [sds((B, S, FOX_WIDTH), F32), sds((B, S, FOX_WIDTH), F32), sds((B, S, FOX_HEADS), F32),
                 sds((B, S, MLA_KV_LORA), F32), sds((B, S, MLA_ROPE_DIM), F32),
                 sds((B, ATTN_HEADS, S, HEAD_ROW), BF16), sds((B, ATTN_HEADS, S, HEAD_ROW), BF16),
                 sds((B, ATTN_HEADS, S, V_DIM), BF16)]
    return pl.pallas_call(
        _proj_body, grid=(B, S // tm), in_specs=in_specs, out_specs=out_specs, out_shape=out_shape,
        scratch_shapes=[pltpu.VMEM((1, 128), F32)],
        compiler_params=_params(("arbitrary", "arbitrary")), name="proj",
    )(x, c0, tri, cos, sin, w["gmix"], w["win"], w["bf"], w["gq"], w["wuq"], w["gkv"], w["wukv"], w["sel"])


def _cache_body(ck_ref, cv_ref, clogf_ref, cckv_ref, ckr_ref, tri_ref, wukv_ref, kh_ref, vh_ref, clast_ref):
    rows = ck_ref.shape[1]
    c_slab = _cumsum_rows(tri_ref[...], clogf_ref[0], jnp.zeros((1, 128), F32))
    clast_ref[0] = c_slab[rows - 1:rows, :]
    _write_fox_kv(kh_ref, vh_ref, ck_ref[0], cv_ref[0], c_slab)
    kv = jnp.dot(cckv_ref[0].astype(BF16), wukv_ref[...], preferred_element_type=F32)
    _write_mla_kv(kh_ref, vh_ref, kv, ckr_ref[0])
    kh_ref[0, :, rows:, :] = jnp.zeros((ATTN_HEADS, kh_ref.shape[2] - rows, HEAD_ROW), BF16)
    vh_ref[0, :, rows:, :] = jnp.zeros((ATTN_HEADS, vh_ref.shape[2] - rows, V_DIM), BF16)


def _cache_rows(ck, cv, clogf_slab, cckv, ckr, wukv, total_rows):
    B, P, _ = ck.shape
    tri = jnp.tril(jnp.ones((P, P), F32)).astype(BF16)
    per_b = lambda width: pl.BlockSpec((1, P, width), lambda b: (b, 0, 0))
    const = lambda shape: pl.BlockSpec(shape, lambda b: (0,) * len(shape))
    head = lambda width: pl.BlockSpec((1, ATTN_HEADS, total_rows, width), lambda b: (b, 0, 0, 0))
    sds = jax.ShapeDtypeStruct
    return pl.pallas_call(
        _cache_body, grid=(B,),
        in_specs=[per_b(FOX_WIDTH), per_b(FOX_WIDTH), per_b(128), per_b(MLA_KV_LORA), per_b(MLA_ROPE_DIM),
                  const((P, P)), const(wukv.shape)],
        out_specs=[head(HEAD_ROW), head(V_DIM), pl.BlockSpec((1, 1, 128), lambda b: (b, 0, 0))],
        out_shape=[sds((B, ATTN_HEADS, total_rows, HEAD_ROW), BF16), sds((B, ATTN_HEADS, total_rows, V_DIM), BF16),
                   sds((B, 1, 128), F32)],
        compiler_params=_params(("arbitrary",)), name="cache_rows",
    )(ck, cv, clogf_slab, cckv, ckr, tri, wukv)


def _reduce_rows(pair_op, final_op, x):
    while x.shape[0] > 8 and x.shape[0] % 16 == 0:
        half = x.shape[0] // 2
        x = pair_op(x[:half], x[half:])
    return final_op(x, axis=0, keepdims=True)


def _attn_body(q_ref, k_ref, v_ref, o_ref, *, hg, tq, tk, nkv_total, q_off, kv_len):
    grp = pl.program_id(1)
    qi = pl.program_id(2)
    q_lo = q_off + qi * tq
    qpos = q_lo + lax.broadcasted_iota(jnp.int32, (1, tq), 1)
    lim = jnp.where(grp >= FOX_HEADS // hg, qpos | (CHUNK - 1), qpos)
    lim = jnp.minimum(lim, kv_len - 1)
    n_full = jnp.minimum(lax.div(q_lo + 1, tk), nkv_total)
    n_all = jnp.minimum(lax.div((q_lo + tq - 1) | (CHUNK - 1), tk) + 1, nkv_total)

    def step(j, carry, masked):
        off = pl.multiple_of(j * tk, tk)
        if masked:
            visible = off + lax.broadcasted_iota(jnp.int32, (tk, 1), 0) <= lim
        out = []
        scores = [lax.dot_general(k_ref[0, hh, pl.ds(off, tk), :], q_ref[0, hh], (((1,), (1,)), ((), ())),
                                  preferred_element_type=F32) for hh in range(hg)]
        for hh in range(hg):
            m, l, acc = carry[3 * hh:3 * hh + 3]
            vb = v_ref[0, hh, pl.ds(off, tk), :]
            s = scores[hh]
            if masked:
                s = jnp.where(visible, s, NEG)
            m_new = jnp.maximum(m, _reduce_rows(jnp.maximum, jnp.max, s))
            alpha = jnp.exp2(m - m_new)
            p = jnp.exp2(s - m_new)
            l = alpha * l + _reduce_rows(jnp.add, jnp.sum, p)
            pv = lax.dot_general(vb, p.astype(BF16), (((0,), (0,)), ((), ())), preferred_element_type=F32)
            out += [m_new, l, alpha * acc + pv]
        return tuple(out)

    init = (jnp.full((1, tq), NEG, F32), jnp.zeros((1, tq), F32), jnp.zeros((V_DIM, tq), F32)) * hg
    carry = lax.fori_loop(0, n_full, functools.partial(step, masked=False), init)
    carry = lax.fori_loop(n_full, n_all, functools.partial(step, masked=True), carry)
    o_ref[0] = jnp.concatenate([carry[3 * hh + 2] / carry[3 * hh + 1] for hh in range(hg)], axis=0).astype(BF16)


def _attention(qh, kh, vh, hg, tq, tk, q_off, kv_len):
    B, _, Tq, _ = qh.shape
    Tk = kh.shape[2]
    body = functools.partial(_attn_body, hg=hg, tq=tq, tk=tk, nkv_total=Tk // tk, q_off=q_off, kv_len=kv_len)
    return pl.pallas_call(
        body, grid=(B, ATTN_HEADS // hg, Tq // tq),
        in_specs=[pl.BlockSpec((1, hg, tq, HEAD_ROW), lambda b, g, i: (b, g, i, 0)),
                  pl.BlockSpec((1, hg, Tk, HEAD_ROW), lambda b, g, i: (b, g, 0, 0)),
                  pl.BlockSpec((1, hg, Tk, V_DIM), lambda b, g, i: (b, g, 0, 0))],
        out_specs=pl.BlockSpec((1, hg * V_DIM, tq), lambda b, g, i: (b, g, i)),
        out_shape=jax.ShapeDtypeStruct((B, ATTN_HEADS * V_DIM, Tq), BF16),
        compiler_params=_params(("arbitrary", "arbitrary", "arbitrary")), name="attn",
    )(qh, kh, vh)


def _sorting_network(n):
    size = 1 << (n - 1).bit_length()

    def merge(lo, hi, r):
        step = r * 2
        if step < hi - lo:
            yield from merge(lo, hi, step)
            yield from merge(lo + r, hi, step)
            yield from ((i, i + r) for i in range(lo + r, hi - r, step))
        else:
            yield (lo, lo + r)

    def sort(lo, hi):
        if hi - lo >= 1:
            mid = lo + (hi - lo) // 2
            yield from sort(lo, mid)
            yield from sort(mid + 1, hi)
            yield from merge(lo, hi, 1)

    return tuple((i, j) for i, j in sort(0, size - 1) if j < n)


def _largest(groups, n_out):
    g = list(groups)
    for i, j in _sorting_network(len(g)):
        g[i], g[j] = jnp.maximum(g[i], g[j]), jnp.minimum(g[i], g[j])
    below = jnp.full(g[0].shape, -jnp.inf, F32)
    out = []
    for r in range(n_out):
        m = jnp.max(g[0], axis=0, keepdims=True)
        out.append(m)
        hit = g[0] == m
        for k in range(min(n_out - 1 - r, len(g))):
            g[k] = jnp.where(hit, g[k + 1] if k + 1 < len(g) else below, g[k])
    return out


def _top_ranks(work, rows_ref, want_rank):
    rows_ref[...] = jnp.full(rows_ref.shape, -jnp.inf, F32)
    for r, m in enumerate(_largest([work[8 * k:8 * (k + 1), :] for k in range(work.shape[0] // 8)], N_RANK)):
        rows_ref[r:r + 1, :] = m
    if not want_rank:
        return None
    rank = jnp.zeros(work.shape, F32)
    for r in range(N_RANK):
        rank = rank + jnp.where(rows_ref[r:r + 1, :] > work, 1.0, 0.0)
    return rank


def _route_body(x_ref, o_ref, wo_ref, gffn_ref, wpq_ref, k1_ref, k2_ref,
                x1_ref, h2_ref, rank_ref, e2_ref, cnt_ref, w1_ref, qp_ref, *rank_rows):
    x1 = x_ref[...] + lax.dot_general(o_ref[0], wo_ref[...], (((0,), (0,)), ((), ())), preferred_element_type=F32)
    x1_ref[...] = x1
    h2 = _rms(x1, gffn_ref[...]).astype(BF16)
    h2_ref[...] = h2
    qp_ref[...] = jnp.dot(h2, wpq_ref[...], preferred_element_type=F32)
    key_dim = 2 * PEER_HALF

    def head(h, a_ref, b_ref):
        qh = qp_ref[:, pl.ds(pl.multiple_of(h * key_dim, key_dim), key_dim)]
        qn = qh * lax.rsqrt(jnp.mean(qh * qh, axis=-1, keepdims=True) + EPS)
        nt = (((1,), (1,)), ((), ()))
        s1 = lax.dot_general(k1_ref[h], qn[:, :PEER_HALF].astype(BF16), nt, preferred_element_type=F32)
        s2 = lax.dot_general(k2_ref[h], qn[:, PEER_HALF:].astype(BF16), nt, preferred_element_type=F32)
        _top_ranks(s1, a_ref, False)
        rank = _top_ranks(s2, b_ref, True)
        groups = [a_ref[0:1, :] + b_ref[8 * k:8 * (k + 1), :] for k in range(RANK_ROWS // 8)]
        groups += [a_ref[r:r + 1, :] + b_ref[0:8, :] for r in range(1, 8)]
        groups += [a_ref[8 * k:8 * (k + 1), :] + b_ref[0:1, :] for k in range(1, RANK_ROWS // 8)]
        pair_scores = _largest(groups, N_RANK)
        top = pair_scores[0]
        zsum = jnp.zeros_like(top)
        for v in pair_scores[:PEER_TOPK]:
            zsum = zsum + jnp.exp(v - top)
        tau = 0.5 * (pair_scores[PEER_TOPK - 1] + pair_scores[PEER_TOPK])
        thr = tau - s1
        cnt = jnp.zeros_like(s1)
        for c in range(N_RANK):
            cnt = cnt + jnp.where(b_ref[c:c + 1, :] >= thr, 1.0, 0.0)
        rank_ref[0, h] = rank.astype(BF16)
        e2_ref[0, h] = jnp.exp(s2 - b_ref[0:1, :]).astype(BF16)
        cnt_ref[0, h] = cnt
        w1_ref[0, h] = (0.5 * jnp.exp(s1 - a_ref[0:1, :])) / zsum

    group = len(rank_rows) // 2

    def head_group(p, carry):
        for k in range(group):
            head(group * p + k, rank_rows[2 * k], rank_rows[2 * k + 1])
        return carry

    lax.fori_loop(0, PEER_HEADS // group, head_group, 0)


def _route(x2d, o_t, w, tm, tt):
    T, D = x2d.shape
    per_seq = o_t.shape[2] // tm
    r = tt // tm
    kq = w["wpq"].shape[1]
    const = lambda shape: pl.BlockSpec(shape, lambda i: (0,) * len(shape))
    lane_out = pl.BlockSpec((1, PEER_HEADS, PEER_N_KEYS, tm), lambda i: (i // r, 0, 0, i % r))
    sds = jax.ShapeDtypeStruct
    lane_shape = lambda dtype: sds((T // tt, PEER_HEADS, PEER_N_KEYS, tt), dtype)
    return pl.pallas_call(
        _route_body, grid=(T // tm,),
        in_specs=[pl.BlockSpec((tm, D), lambda i: (i, 0)),
                  pl.BlockSpec((1, o_t.shape[1], tm), lambda i: (i // per_seq, 0, i % per_seq)),
                  const((D, D)), const((1, D)), const((D, kq)),
                  const((PEER_HEADS, PEER_N_KEYS, PEER_HALF)), const((PEER_HEADS, PEER_N_KEYS, PEER_HALF))],
        out_specs=[pl.BlockSpec((tm, D), lambda i: (i, 0)), pl.BlockSpec((tm, D), lambda i: (i, 0)),
                   lane_out, lane_out, lane_out, lane_out],
        out_shape=[sds((T, D), F32), sds((T, D), BF16),
                   lane_shape(BF16), lane_shape(BF16), lane_shape(F32), lane_shape(F32)],
        scratch_shapes=[pltpu.VMEM((tm, kq), F32)] + [pltpu.VMEM((RANK_ROWS, tm), F32)] * (2 * ROUTE_HEAD_GROUP),
        compiler_params=_params(("arbitrary",)), name="route",
    )(x2d, o_t, w["wo"], w["gffn"], w["wpq"], w["k1"], w["k2"])


GATE_ROWS = 32
GATE_LANES = 256
CHUNK_I = 2
OUT_CHUNKS = 1


def _row_tile(ref, h, i, ls):
    tile = jnp.broadcast_to(ref[0, h, i:i + 1, ls], (8, GATE_LANES)).astype(BF16)
    return jnp.concatenate([tile] * (GATE_ROWS // 8), axis=0)


def _peer_body(x1_ref, h2_ref, rank_ref, e2_ref, cnt_ref, w1_ref, u_ref, v_ref, gfin_ref, y_ref,
               acc_ref, *chunk_refs, tt, rows_per_step):
    j = pl.program_id(1)

    @pl.when(j == 0)
    def _():
        acc_ref[...] = jnp.zeros_like(acc_ref)

    n_chunks = len(chunk_refs) * OUT_CHUNKS // (OUT_CHUNKS + 1)
    a_refs, act_refs = chunk_refs[:n_chunks], chunk_refs[n_chunks:]
    inv_sqrt2 = 1.0 / math.sqrt(2.0)
    rows = CHUNK_I * PEER_N_KEYS

    def score_matmul(c):
        cs = slice(c * rows, (c + 1) * rows)
        a_refs[c][...] = lax.dot_general(u_ref[cs, :], h2_ref[...], (((1,), (1,)), ((), ())),
                                         preferred_element_type=F32)

    score_matmul(0)
    score_matmul(1)
    for c in range(n_chunks):
        if c + 2 < n_chunks:
            score_matmul(c + 2)
        act_ref = act_refs[c // OUT_CHUNKS]
        act_row = (c % OUT_CHUNKS) * rows
        a = a_refs[c]
        for k in range(CHUNK_I):
            i = c * CHUNK_I + k
            for tc in range(tt // GATE_LANES):
                ls = slice(tc * GATE_LANES, (tc + 1) * GATE_LANES)
                cnt = [_row_tile(cnt_ref, h, i, ls) for h in range(PEER_HEADS)]
                w1 = [_row_tile(w1_ref, h, i, ls) for h in range(PEER_HEADS)]
                for jb in range(PEER_N_KEYS // GATE_ROWS):
                    js = slice(jb * GATE_ROWS, (jb + 1) * GATE_ROWS)
                    gate = jnp.zeros((GATE_ROWS, GATE_LANES), BF16)
                    for h in range(PEER_HEADS):
                        sel = rank_ref[0, h, js, ls] < cnt[h]
                        gate = gate + jnp.where(sel, e2_ref[0, h, js, ls] * w1[h], 0.0)
                    r0 = k * PEER_N_KEYS + jb * GATE_ROWS
                    av = a[r0:r0 + GATE_ROWS, ls]
                    gelu2 = av * (1.0 + lax.erf(av * inv_sqrt2))
                    act_ref[act_row + r0:act_row + r0 + GATE_ROWS, ls] = gelu2.astype(BF16) * gate
        if c % OUT_CHUNKS == OUT_CHUNKS - 1:
            vs = slice((c + 1 - OUT_CHUNKS) * rows, (c + 1) * rows)
            acc_ref[...] += lax.dot_general(act_ref[...], v_ref[vs, :], (((0,), (0,)), ((), ())),
                                            preferred_element_type=F32)

    @pl.when(j == pl.num_programs(1) - 1)
    def _():
        y_ref[...] = _rms(x1_ref[...] + acc_ref[...], gfin_ref[...])


def _peer(x1, h2, rank, e2, cnt, w1, w, tt, rows_per_step):
    T, D = x1.shape
    n_exp = w["v"].shape[0]
    eb = rows_per_step * PEER_N_KEYS
    lane_in = pl.BlockSpec((1, PEER_HEADS, PEER_N_KEYS, tt), lambda t, j: (t, 0, 0, 0))
    row_in = pl.BlockSpec((1, PEER_HEADS, rows_per_step, tt), lambda t, j: (t, 0, j, 0))
    body = functools.partial(_peer_body, tt=tt, rows_per_step=rows_per_step)
    n_chunks = rows_per_step // CHUNK_I
    a_chunks = [pltpu.VMEM((CHUNK_I * PEER_N_KEYS, tt), F32)] * n_chunks
    act_chunks = [pltpu.VMEM((OUT_CHUNKS * CHUNK_I * PEER_N_KEYS, tt), BF16)] * (n_chunks // OUT_CHUNKS)
    return pl.pallas_call(
        body, grid=(T // tt, n_exp // eb),
        in_specs=[pl.BlockSpec((tt, D), lambda t, j: (t, 0)), pl.BlockSpec((tt, D), lambda t, j: (t, 0)),
                  lane_in, lane_in, row_in, row_in,
                  pl.BlockSpec((eb, D), lambda t, j: (j, 0)), pl.BlockSpec((eb, D), lambda t, j: (j, 0)),
                  pl.BlockSpec((1, D), lambda t, j: (0, 0))],
        out_specs=pl.BlockSpec((tt, D), lambda t, j: (t, 0)),
        out_shape=jax.ShapeDtypeStruct((T, D), F32),
        scratch_shapes=[pltpu.VMEM((tt, D), F32)] + a_chunks + act_chunks,
        compiler_params=_params(("arbitrary", "arbitrary")), name="peer",
    )(x1, h2, rank, e2, cnt, w1, w["u"], w["v"], w["gfin"])


def _rope_tables(pos):
    half = MLA_ROPE_DIM // 2
    inv = ROPE_BASE ** (-jnp.arange(half, dtype=F32) / half)
    ang = pos.astype(F32)[:, None] * inv[None, :]
    pad = ((0, 0), (ROPE_LANE, HEAD_ROW - ROPE_LANE - MLA_ROPE_DIM))
    return (jnp.pad(jnp.concatenate([jnp.cos(ang)] * 2, axis=-1), pad),
            jnp.pad(jnp.concatenate([jnp.sin(ang)] * 2, axis=-1), pad))


def _rot_cols(wcols):
    half = wcols.shape[-1] // 2
    return jnp.concatenate([-wcols[..., half:], wcols[..., :half]], axis=-1)


def _cast_body(x_ref, o_ref):
    o_ref[...] = x_ref[...].astype(o_ref.dtype)


def _to_bf16(x, rows=2048):
    n, d = x.shape
    return pl.pallas_call(
        _cast_body, grid=(n // rows,), in_specs=[pl.BlockSpec((rows, d), lambda i: (i, 0))],
        out_specs=pl.BlockSpec((rows, d), lambda i: (i, 0)), out_shape=jax.ShapeDtypeStruct((n, d), BF16),
        compiler_params=_params(("arbitrary",)), name="to_bf16",
    )(x)


def _prep_weights(g_mix, w_in, b_f, g_q, w_uq, g_kv, w_ukv, w_o, g_ffn, w_pq, sub_keys, peer_u, peer_v, g_final):
    D = w_in.shape[0]
    off_ff = 3 * FOX_WIDTH
    off_cq = off_ff + FOX_HEADS
    off_ckv = off_cq + MLA_Q_LORA
    off_kr = off_ckv + MLA_KV_LORA
    kr_cols = w_in[:, off_kr:off_kr + MLA_ROPE_DIM]
    zeros = lambda n: jnp.zeros((D, n), F32)
    rope_tail = zeros(HEAD_ROW - ROPE_LANE - MLA_ROPE_DIM)
    tail_a = [w_in[:, off_ff:off_cq], zeros(ROPE_LANE - FOX_HEADS), kr_cols, rope_tail]
    tail_b = [zeros(ROPE_LANE), _rot_cols(kr_cols), rope_tail]
    win = jnp.concatenate([w_in[:, :off_ff], w_in[:, off_cq:off_kr]] + tail_a + tail_b, axis=1).astype(BF16)
    bf = jnp.pad(b_f, (0, 128 - FOX_HEADS)).reshape(1, 128)
    per_q = MLA_NOPE_DIM + MLA_ROPE_DIM
    wq = w_uq.reshape(MLA_Q_LORA, MLA_HEADS, per_q)
    q_rope = wq[:, :, MLA_NOPE_DIM:]
    in_row = lambda cols: jnp.pad(cols, ((0, 0), (0, 0), (ROPE_LANE, HEAD_ROW - ROPE_LANE - MLA_ROPE_DIM)))
    wuq = jnp.concatenate([wq[:, :, :MLA_NOPE_DIM].reshape(MLA_Q_LORA, -1), in_row(q_rope).reshape(MLA_Q_LORA, -1),
                           in_row(_rot_cols(q_rope)).reshape(MLA_Q_LORA, -1)], axis=1).astype(BF16)
    sel = np.zeros((128, 2 * FOX_HEADS * HEAD_ROW), np.float32)
    for hd in range(FOX_HEADS):
        q0, k0 = hd * HEAD_ROW + FOX_HEAD_DIM, (FOX_HEADS + hd) * HEAD_ROW + FOX_HEAD_DIM
        for k in range(3):
            sel[k * FOX_HEADS + hd, q0 + k] = 1.0
            sel[3 * FOX_HEADS, q0 + 3 + k] = 1.0
            sel[3 * FOX_HEADS, k0 + k] = 1.0
            sel[k * FOX_HEADS + hd, k0 + 3 + k] = -1.0
    wkv = w_ukv.reshape(MLA_KV_LORA, MLA_HEADS, MLA_NOPE_DIM + MLA_V_DIM)
    wukv = jnp.concatenate([wkv[:, :, :MLA_NOPE_DIM].reshape(MLA_KV_LORA, -1),
                            wkv[:, :, MLA_NOPE_DIM:].reshape(MLA_KV_LORA, -1)], axis=1).astype(BF16)
    return dict(
        gmix=g_mix.reshape(1, -1), win=win, bf=bf, sel=jnp.asarray(sel, BF16), gq=g_q.reshape(1, -1), wuq=wuq, gkv=g_kv.reshape(1, -1),
        wukv=wukv, wo=w_o.astype(BF16), gffn=g_ffn.reshape(1, -1), wpq=w_pq.astype(BF16),
        k1=sub_keys[0].astype(BF16), k2=sub_keys[1].astype(BF16),
        u=_to_bf16(peer_u), v=_to_bf16(peer_v), gfin=g_final.reshape(1, -1))


def _ffn(x, o, w):
    B, S, D = x.shape
    T = B * S
    tm = min(256, T)
    tt = min(512, T)
    if S % tm:
        o = jnp.transpose(o, (1, 0, 2)).reshape(1, o.shape[1], T)
    x1, h2, rank, e2, cnt, w1 = _route(x.reshape(T, D), o, w, tm, tt)
    return _peer(x1, h2, rank, e2, cnt, w1, w, tt, 32).reshape(B, S, D)


def kernel(x_prompt, x_sample, cache_fox_k, cache_fox_v, cache_fox_logf, cache_mla_ckv, cache_mla_krope, g_mix, w_in, b_f, g_q, w_uq, g_kv, w_ukv, w_o, g_ffn, w_pq, peer_sub_keys, peer_u, peer_v, g_final):
    assert g_mix.shape[0] == 1, "single-layer model"
    w = _prep_weights(g_mix[0], w_in[0], b_f[0], g_q[0], w_uq[0], g_kv[0], w_ukv[0], w_o[0], g_ffn[0], w_pq[0],
                      peer_sub_keys[0], peer_u[0], peer_v[0], g_final)
    B, S, D = x_prompt.shape
    Bs, Ss, _ = x_sample.shape
    P = cache_fox_k.shape[2]

    tables_p = _rope_tables(jnp.arange(S, dtype=jnp.int32))
    tm = min(512, S)
    fk, fv, logf, ckv, kr, qh, kh, vh = _proj(x_prompt, jnp.zeros((B, 1, 128), F32), tables_p, w, tm)
    tq = min(256, S)
    o_p = _attention(qh, kh, vh, ATTN_HEAD_GROUP, tq, tq, 0, S)
    y_p = _ffn(x_prompt, o_p, w)

    tk = 256
    L = P + Ss
    kh_c, vh_c, c_last = _cache_rows(
        cache_fox_k[0].reshape(Bs, P, FOX_WIDTH), cache_fox_v[0].reshape(Bs, P, FOX_WIDTH),
        jnp.pad(cache_fox_logf[0], ((0, 0), (0, 0), (0, 128 - FOX_HEADS))),
        cache_mla_ckv[0], cache_mla_krope[0], w["wukv"], L + (-L) % tk)
    tables_s = _rope_tables(P + jnp.arange(Ss, dtype=jnp.int32))
    sfk, sfv, slogf, sckv, skr, sqh, skh, svh = _proj(x_sample, c_last, tables_s, w, Ss)
    kh_all = lax.dynamic_update_slice(kh_c, skh, (0, 0, P, 0))
    vh_all = lax.dynamic_update_slice(vh_c, svh, (0, 0, P, 0))
    o_s = _attention(sqh, kh_all, vh_all, ATTN_HEAD_GROUP, Ss, tk, P, L)
    y_s = _ffn(x_sample, o_s, w)

    return (y_p, y_s,
            fk.reshape(1, B, S, FOX_HEADS, FOX_HEAD_DIM), fv.reshape(1, B, S, FOX_HEADS, FOX_HEAD_DIM),
            logf[None], ckv[None], kr[None],
            sfk.reshape(1, Bs, Ss, FOX_HEADS, FOX_HEAD_DIM), sfv.reshape(1, Bs, Ss, FOX_HEADS, FOX_HEAD_DIM),
            slogf[None], sckv[None], skr[None])
```

```python
import functools
import math

import numpy as np
import jax
import jax.numpy as jnp
from jax import lax
from jax.experimental import pallas as pl
from jax.experimental.pallas import tpu as pltpu

F32 = jnp.float32
BF16 = jnp.bfloat16

EPS = 1e-6
CHUNK = 64
ROPE_BASE = 10000.0

FOX_HEADS = 8
FOX_HEAD_DIM = 64
FOX_WIDTH = FOX_HEADS * FOX_HEAD_DIM
MLA_HEADS = 8
MLA_NOPE_DIM = 64
MLA_ROPE_DIM = 32
MLA_V_DIM = 64
MLA_Q_LORA = 256
MLA_KV_LORA = 128
ATTN_HEADS = FOX_HEADS + MLA_HEADS
HEAD_ROW = 128
V_DIM = 64

PEER_HEADS = 8
PEER_N_KEYS = 128
PEER_HALF = 128
PEER_TOPK = 16
N_RANK = PEER_TOPK + 1
RANK_ROWS = 24
ROUTE_HEAD_GROUP = 4

Z_FQ = 0
Z_FK = Z_FQ + FOX_WIDTH
Z_FV = Z_FK + FOX_WIDTH
Z_CQ = Z_FV + FOX_WIDTH
Z_CKV = Z_CQ + MLA_Q_LORA
Z_TAIL = Z_CKV + MLA_KV_LORA
Z_TAILB = Z_TAIL + 128
Z_WIDTH = Z_TAILB + 128
ROPE_LANE = 64

NEG = -1e30
LOG2E = math.log2(math.e)
ATTN_HEAD_GROUP = 8
V7X_VMEM_LIMIT_BYTES = 56 * 1024 * 1024


def _params(sem, flags=None):
    return pltpu.CompilerParams(dimension_semantics=sem, vmem_limit_bytes=V7X_VMEM_LIMIT_BYTES, flags=flags)


def _rms(x, g):
    return x * lax.rsqrt(jnp.mean(x * x, axis=-1, keepdims=True) + EPS) * g


def _log_sigmoid(x):
    return jnp.minimum(x, 0.0) - jnp.log1p(jnp.exp(-jnp.abs(x)))


def _split3(x):
    hi = x.astype(BF16)
    r1 = x - hi.astype(F32)
    mid = r1.astype(BF16)
    lo = (r1 - mid.astype(F32)).astype(BF16)
    return hi, mid, lo


def _cumsum_rows(tri, x, carry):
    acc = carry
    for piece in _split3(x):
        acc = acc + jnp.dot(tri, piece, preferred_element_type=F32)
    return acc


def _fox_aux(c_col):
    hi, mid, lo = (p.astype(F32) for p in _split3(c_col * LOG2E))
    lane = lax.broadcasted_iota(jnp.int32, (1, HEAD_ROW - FOX_HEAD_DIM), 1)
    q_aux = jnp.where(lane == 0, hi, jnp.where(lane == 1, mid, jnp.where(
        lane == 2, lo, jnp.where(lane < 6, 1.0, 0.0))))
    k_aux = jnp.where(lane < 3, 1.0, jnp.where(lane == 3, -hi, jnp.where(
        lane == 4, -mid, jnp.where(lane == 5, -lo, 0.0))))
    return q_aux, k_aux


def _write_fox_kv(kh_ref, vh_ref, k, v, c_slab):
    rows = slice(0, k.shape[0])
    for h in range(FOX_HEADS):
        sl = slice(h * FOX_HEAD_DIM, (h + 1) * FOX_HEAD_DIM)
        _, k_aux = _fox_aux(c_slab[:, h:h + 1])
        kh_ref[0, h, rows, 0:FOX_HEAD_DIM] = k[:, sl].astype(BF16)
        kh_ref[0, h, rows, FOX_HEAD_DIM:HEAD_ROW] = k_aux.astype(BF16)
        vh_ref[0, h, rows, :] = v[:, sl].astype(BF16)


def _write_mla_kv(kh_ref, vh_ref, kv, krope):
    rows = slice(0, kv.shape[0])
    k_rope = krope.astype(BF16)
    pad = jnp.zeros((kv.shape[0], HEAD_ROW - MLA_NOPE_DIM - MLA_ROPE_DIM), BF16)
    for h in range(MLA_HEADS):
        sl = slice(h * MLA_NOPE_DIM, (h + 1) * MLA_NOPE_DIM)
        kh_ref[0, FOX_HEADS + h, rows, 0:MLA_NOPE_DIM] = kv[:, sl].astype(BF16)
        kh_ref[0, FOX_HEADS + h, rows, MLA_NOPE_DIM:MLA_NOPE_DIM + MLA_ROPE_DIM] = k_rope
        kh_ref[0, FOX_HEADS + h, rows, MLA_NOPE_DIM + MLA_ROPE_DIM:HEAD_ROW] = pad
        vsl = slice(MLA_HEADS * MLA_NOPE_DIM + h * MLA_V_DIM, MLA_HEADS * MLA_NOPE_DIM + (h + 1) * MLA_V_DIM)
        vh_ref[0, FOX_HEADS + h, rows, :] = kv[:, vsl].astype(BF16)


def _proj_body(x_ref, c0_ref, tri_ref, cos_ref, sin_ref, gmix_ref, win_ref, bf_ref, gq_ref, wuq_ref, gkv_ref,
               wukv_ref, sel_ref,
               fk_ref, fv_ref, logf_ref, ckv_ref, kr_ref, qh_ref, kh_ref, vh_ref, carry_ref):
    @pl.when(pl.program_id(1) == 0)
    def _():
        carry_ref[...] = c0_ref[0]

    rows = x_ref.shape[1]
    h = _rms(x_ref[0], gmix_ref[...]).astype(BF16)
    z = jnp.dot(h, win_ref[...], preferred_element_type=F32)

    fq = z[:, Z_FQ:Z_FK]
    fk = z[:, Z_FK:Z_FV]
    fv = z[:, Z_FV:Z_CQ]
    fk_ref[0] = fk
    fv_ref[0] = fv

    tail_a = z[:, Z_TAIL:Z_TAILB]
    tail_b = z[:, Z_TAILB:Z_WIDTH]
    logf = _log_sigmoid(tail_a + bf_ref[...])
    logf_ref[0] = logf[:, 0:FOX_HEADS]
    c_slab = _cumsum_rows(tri_ref[...], logf, carry_ref[...])
    carry_ref[...] = c_slab[rows - 1:rows, :]

    lane = lax.broadcasted_iota(jnp.int32, (1, 128), 1)
    hi, mid, lo = (p.astype(F32) for p in _split3(c_slab * LOG2E))
    pieces = jnp.where(lane < FOX_HEADS, hi, jnp.where(lane < 2 * FOX_HEADS, pltpu.roll(mid, FOX_HEADS, 1), jnp.where(
        lane < 3 * FOX_HEADS, pltpu.roll(lo, 2 * FOX_HEADS, 1), jnp.where(lane == 3 * FOX_HEADS, 1.0, 0.0))))
    aux = jnp.dot(pieces.astype(BF16), sel_ref[...], preferred_element_type=F32)

    cos = cos_ref[...]
    sin = sin_ref[...]
    krope = tail_a * cos + tail_b * sin
    kr_ref[0] = krope[:, ROPE_LANE:ROPE_LANE + MLA_ROPE_DIM]

    cqn = _rms(z[:, Z_CQ:Z_CKV], gq_ref[...]).astype(BF16)
    q = jnp.dot(cqn, wuq_ref[...], preferred_element_type=F32)
    ckv = _rms(z[:, Z_CKV:Z_TAIL], gkv_ref[...])
    ckv_ref[0] = ckv
    kv = jnp.dot(ckv.astype(BF16), wukv_ref[...], preferred_element_type=F32)

    fox_scale = FOX_HEAD_DIM ** -0.5 * LOG2E
    mla_scale = (MLA_NOPE_DIM + MLA_ROPE_DIM) ** -0.5 * LOG2E
    low = lane < FOX_HEAD_DIM
    nope_w = MLA_HEADS * MLA_NOPE_DIM
    row_w = MLA_HEADS * HEAD_ROW

    def head_src(block, hd):
        slab = block[:, (hd // 2) * 128:(hd // 2 + 1) * 128]
        return slab if hd % 2 == 0 else pltpu.roll(slab, 64, 1)

    for hd in range(FOX_HEADS):
        rs = slice(hd * HEAD_ROW, (hd + 1) * HEAD_ROW)
        ks = slice(row_w + hd * HEAD_ROW, row_w + (hd + 1) * HEAD_ROW)
        qh_ref[0, hd] = jnp.where(low, head_src(fq, hd) * fox_scale, aux[:, rs]).astype(BF16)
        kh_ref[0, hd] = jnp.where(low, head_src(fk, hd), aux[:, ks]).astype(BF16)
        vh_ref[0, hd] = fv[:, hd * V_DIM:(hd + 1) * V_DIM].astype(BF16)
    for hd in range(MLA_HEADS):
        rope = (q[:, nope_w + hd * HEAD_ROW:nope_w + (hd + 1) * HEAD_ROW] * cos
                + q[:, nope_w + row_w + hd * HEAD_ROW:nope_w + row_w + (hd + 1) * HEAD_ROW] * sin)
        qh_ref[0, FOX_HEADS + hd] = (jnp.where(low, head_src(q, hd), rope) * mla_scale).astype(BF16)
        kh_ref[0, FOX_HEADS + hd] = jnp.where(low, head_src(kv, hd), krope).astype(BF16)
        vh_ref[0, FOX_HEADS + hd] = kv[:, nope_w + hd * V_DIM:nope_w + (hd + 1) * V_DIM].astype(BF16)


def _proj(x, c0, tables, w, tm):
    B, S, D = x.shape
    cos, sin = tables
    tri = jnp.tril(jnp.ones((tm, tm), F32)).astype(BF16)
    const = lambda shape: pl.BlockSpec(shape, lambda b, t: (0,) * len(shape))
    tab = pl.BlockSpec((tm, 128), lambda b, t: (t, 0))
    in_specs = [
        pl.BlockSpec((1, tm, D), lambda b, t: (b, t, 0)),
        pl.BlockSpec((1, 1, 128), lambda b, t: (b, 0, 0)),
        const((tm, tm)), tab, tab,
        const((1, D)), const((D, Z_WIDTH)), const((1, 128)),
        const((1, MLA_Q_LORA)), const(w["wuq"].shape), const((1, MLA_KV_LORA)), const(w["wukv"].shape),
        const(w["sel"].shape),
    ]
    tok = lambda width: pl.BlockSpec((1, tm, width), lambda b, t: (b, t, 0))
    head = lambda width: pl.BlockSpec((1, ATTN_HEADS, tm, width), lambda b, t: (b, 0, t, 0))
    out_specs = [tok(FOX_WIDTH), tok(FOX_WIDTH), tok(FOX_HEADS), tok(MLA_KV_LORA), tok(MLA_ROPE_DIM),
                 head(HEAD_ROW), head(HEAD_ROW), head(V_DIM)]
    sds = jax.ShapeDtypeStruct
    out_shape = [sds((B, S, FOX_WIDTH), F32), sds((B, S, FOX_WIDTH), F32), sds((B, S, FOX_HEADS), F32),
                 sds((B, S, MLA_KV_LORA), F32), sds((B, S, MLA_ROPE_DIM), F32),
                 sds((B, ATTN_HEADS, S, HEAD_ROW), BF16), sds((B, ATTN_HEADS, S, HEAD_ROW), BF16),
                 sds((B, ATTN_HEADS, S, V_DIM), BF16)]
    return pl.pallas_call(
        _proj_body, grid=(B, S // tm), in_specs=in_specs, out_specs=out_specs, out_shape=out_shape,
        scratch_shapes=[pltpu.VMEM((1, 128), F32)],
        compiler_params=_params(("arbitrary", "arbitrary")), name="proj",
    )(x, c0, tri, cos, sin, w["gmix"], w["win"], w["bf"], w["gq"], w["wuq"], w["gkv"], w["wukv"], w["sel"])


def _cache_body(ck_ref, cv_ref, clogf_ref, cckv_ref, ckr_ref, tri_ref, wukv_ref, kh_ref, vh_ref, clast_ref):
    rows = ck_ref.shape[1]
    c_slab = _cumsum_rows(tri_ref[...], clogf_ref[0], jnp.zeros((1, 128), F32))
    clast_ref[0] = c_slab[rows - 1:rows, :]
    _write_fox_kv(kh_ref, vh_ref, ck_ref[0], cv_ref[0], c_slab)
    kv = jnp.dot(cckv_ref[0].astype(BF16), wukv_ref[...], preferred_element_type=F32)
    _write_mla_kv(kh_ref, vh_ref, kv, ckr_ref[0])
    kh_ref[0, :, rows:, :] = jnp.zeros((ATTN_HEADS, kh_ref.shape[2] - rows, HEAD_ROW), BF16)
    vh_ref[0, :, rows:, :] = jnp.zeros((ATTN_HEADS, vh_ref.shape[2] - rows, V_DIM), BF16)


def _cache_rows(ck, cv, clogf_slab, cckv, ckr, wukv, total_rows):
    B, P, _ = ck.shape
    tri = jnp.tril(jnp.ones((P, P), F32)).astype(BF16)
    per_b = lambda width: pl.BlockSpec((1, P, width), lambda b: (b, 0, 0))
    const = lambda shape: pl.BlockSpec(shape, lambda b: (0,) * len(shape))
    head = lambda width: pl.BlockSpec((1, ATTN_HEADS, total_rows, width), lambda b: (b, 0, 0, 0))
    sds = jax.ShapeDtypeStruct
    return pl.pallas_call(
        _cache_body, grid=(B,),
        in_specs=[per_b(FOX_WIDTH), per_b(FOX_WIDTH), per_b(128), per_b(MLA_KV_LORA), per_b(MLA_ROPE_DIM),
                  const((P, P)), const(wukv.shape)],
        out_specs=[head(HEAD_ROW), head(V_DIM), pl.BlockSpec((1, 1, 128), lambda b: (b, 0, 0))],
        out_shape=[sds((B, ATTN_HEADS, total_rows, HEAD_ROW), BF16), sds((B, ATTN_HEADS, total_rows, V_DIM), BF16),
                   sds((B, 1, 128), F32)],
        compiler_params=_params(("arbitrary",)), name="cache_rows",
    )(ck, cv, clogf_slab, cckv, ckr, tri, wukv)


def _reduce_rows(pair_op, final_op, x):
    while x.shape[0] > 8 and x.shape[0] % 16 == 0:
        half = x.shape[0] // 2
        x = pair_op(x[:half], x[half:])
    return final_op(x, axis=0, keepdims=True)


def _attn_body(q_ref, k_ref, v_ref, o_ref, *, hg, tq, tk, nkv_total, q_off, kv_len):
    grp = pl.program_id(1)
    qi = pl.program_id(2)
    q_lo = q_off + qi * tq
    qpos = q_lo + lax.broadcasted_iota(jnp.int32, (1, tq), 1)
    lim = jnp.where(grp >= FOX_HEADS // hg, qpos | (CHUNK - 1), qpos)
    lim = jnp.minimum(lim, kv_len - 1)
    n_full = jnp.minimum(lax.div(q_lo + 1, tk), nkv_total)
    n_all = jnp.minimum(lax.div((q_lo + tq - 1) | (CHUNK - 1), tk) + 1, nkv_total)

    def step(j, carry, masked):
        off = pl.multiple_of(j * tk, tk)
        if masked:
            visible = off + lax.broadcasted_iota(jnp.int32, (tk, 1), 0) <= lim
        out = []
        scores = [lax.dot_general(k_ref[0, hh, pl.ds(off, tk), :], q_ref[0, hh], (((1,), (1,)), ((), ())),
                                  preferred_element_type=F32) for hh in range(hg)]
        for hh in range(hg):
            m, l, acc = carry[3 * hh:3 * hh + 3]
            vb = v_ref[0, hh, pl.ds(off, tk), :]
            s = scores[hh]
            if masked:
                s = jnp.where(visible, s, NEG)
            m_new = jnp.maximum(m, _reduce_rows(jnp.maximum, jnp.max, s))
            alpha = jnp.exp2(m - m_new)
            p = jnp.exp2(s - m_new)
            l = alpha * l + _reduce_rows(jnp.add, jnp.sum, p)
            pv = lax.dot_general(vb, p.astype(BF16), (((0,), (0,)), ((), ())), preferred_element_type=F32)
            out += [m_new, l, alpha * acc + pv]
        return tuple(out)

    init = (jnp.full((1, tq), NEG, F32), jnp.zeros((1, tq), F32), jnp.zeros((V_DIM, tq), F32)) * hg
    carry = lax.fori_loop(0, n_full, functools.partial(step, masked=False), init)
    carry = lax.fori_loop(n_full, n_all, functools.partial(step, masked=True), carry)
    o_ref[0] = jnp.concatenate([carry[3 * hh + 2] / carry[3 * hh + 1] for hh in range(hg)], axis=0).astype(BF16)


def _attention(qh, kh, vh, hg, tq, tk, q_off, kv_len):
    B, _, Tq, _ = qh.shape
    Tk = kh.shape[2]
    body = functools.partial(_attn_body, hg=hg, tq=tq, tk=tk, nkv_total=Tk // tk, q_off=q_off, kv_len=kv_len)
    return pl.pallas_call(
        body, grid=(B, ATTN_HEADS // hg, Tq // tq),
        in_specs=[pl.BlockSpec((1, hg, tq, HEAD_ROW), lambda b, g, i: (b, g, i, 0)),
                  pl.BlockSpec((1, hg, Tk, HEAD_ROW), lambda b, g, i: (b, g, 0, 0)),
                  pl.BlockSpec((1, hg, Tk, V_DIM), lambda b, g, i: (b, g, 0, 0))],
        out_specs=pl.BlockSpec((1, hg * V_DIM, tq), lambda b, g, i: (b, g, i)),
        out_shape=jax.ShapeDtypeStruct((B, ATTN_HEADS * V_DIM, Tq), BF16),
        compiler_params=_params(("arbitrary", "arbitrary", "arbitrary")), name="attn",
    )(qh, kh, vh)


def _sorting_network(n):
    size = 1 << (n - 1).bit_length()

    def merge(lo, hi, r):
        step = r * 2
        if step < hi - lo:
            yield from merge(lo, hi, step)
            yield from merge(lo + r, hi, step)
            yield from ((i, i + r) for i in range(lo + r, hi - r, step))
        else:
            yield (lo, lo + r)

    def sort(lo, hi):
        if hi - lo >= 1:
            mid = lo + (hi - lo) // 2
            yield from sort(lo, mid)
            yield from sort(mid + 1, hi)
            yield from merge(lo, hi, 1)

    return tuple((i, j) for i, j in sort(0, size - 1) if j < n)


def _largest(groups, n_out):
    g = list(groups)
    for i, j in _sorting_network(len(g)):
        g[i], g[j] = jnp.maximum(g[i], g[j]), jnp.minimum(g[i], g[j])
    below = jnp.full(g[0].shape, -jnp.inf, F32)
    out = []
    for r in range(n_out):
        m = jnp.max(g[0], axis=0, keepdims=True)
        out.append(m)
        hit = g[0] == m
        for k in range(min(n_out - 1 - r, len(g))):
            g[k] = jnp.where(hit, g[k + 1] if k + 1 < len(g) else below, g[k])
    return out


def _top_ranks(work, rows_ref, want_rank):
    rows_ref[...] = jnp.full(rows_ref.shape, -jnp.inf, F32)
    for r, m in enumerate(_largest([work[8 * k:8 * (k + 1), :] for k in range(work.shape[0] // 8)], N_RANK)):
        rows_ref[r:r + 1, :] = m
    if not want_rank:
        return None
    rank = jnp.where(rows_ref[0:1, :] > work, 1.0, 0.0)
    for r in range(1, N_RANK):
        rank = rank + jnp.where(rows_ref[r:r + 1, :] > work, 1.0, 0.0)
    return rank


def _route_body(x_ref, o_ref, wo_ref, gffn_ref, wpq_ref, k1_ref, k2_ref,
                x1_ref, h2_ref, rank_ref, e2_ref, cnt_ref, w1_ref, qp_ref, *rank_rows):
    x1 = x_ref[...] + lax.dot_general(o_ref[0], wo_ref[...], (((0,), (0,)), ((), ())), preferred_element_type=F32)
    x1_ref[...] = x1
    h2 = _rms(x1, gffn_ref[...]).astype(BF16)
    h2_ref[...] = h2
    qp_ref[...] = jnp.dot(h2, wpq_ref[...], preferred_element_type=F32)
    key_dim = 2 * PEER_HALF

    def head(h, a_ref, b_ref):
        qh = qp_ref[:, pl.ds(pl.multiple_of(h * key_dim, key_dim), key_dim)]
        qn = qh * lax.rsqrt(jnp.mean(qh * qh, axis=-1, keepdims=True) + EPS)
        nt = (((1,), (1,)), ((), ()))
        s1 = lax.dot_general(k1_ref[h], qn[:, :PEER_HALF].astype(BF16), nt, preferred_element_type=F32)
        s2 = lax.dot_general(k2_ref[h], qn[:, PEER_HALF:].astype(BF16), nt, preferred_element_type=F32)
        _top_ranks(s1, a_ref, False)
        rank = _top_ranks(s2, b_ref, True)
        groups = [a_ref[0:1, :] + b_ref[8 * k:8 * (k + 1), :] for k in range(RANK_ROWS // 8)]
        groups += [a_ref[r:r + 1, :] + b_ref[0:8, :] for r in range(1, 8)]
        groups += [a_ref[8 * k:8 * (k + 1), :] + b_ref[0:1, :] for k in range(1, RANK_ROWS // 8)]
        pair_scores = _largest(groups, N_RANK)
        top = pair_scores[0]
        zsum = jnp.zeros_like(top)
        for v in pair_scores[:PEER_TOPK]:
            zsum = zsum + jnp.exp(v - top)
        tau = 0.5 * (pair_scores[PEER_TOPK - 1] + pair_scores[PEER_TOPK])
        thr = tau - s1
        cnt = jnp.where(b_ref[0:1, :] >= thr, 1.0, 0.0)
        for c in range(1, N_RANK):
            cnt = cnt + jnp.where(b_ref[c:c + 1, :] >= thr, 1.0, 0.0)
        rank_ref[0, h] = rank.astype(BF16)
        e2_ref[0, h] = jnp.exp(s2 - b_ref[0:1, :]).astype(BF16)
        cnt_ref[0, h] = cnt
        w1_ref[0, h] = (0.5 * jnp.exp(s1 - a_ref[0:1, :])) / zsum

    group = len(rank_rows) // 2

    def head_group(p, carry):
        for k in range(group):
            head(group * p + k, rank_rows[2 * k], rank_rows[2 * k + 1])
        return carry

    lax.fori_loop(0, PEER_HEADS // group, head_group, 0)


def _route(x2d, o_t, w, tm, tt):
    T, D = x2d.shape
    per_seq = o_t.shape[2] // tm
    r = tt // tm
    kq = w["wpq"].shape[1]
    const = lambda shape: pl.BlockSpec(shape, lambda i: (0,) * len(shape))
    lane_out = pl.BlockSpec((1, PEER_HEADS, PEER_N_KEYS, tm), lambda i: (i // r, 0, 0, i % r))
    sds = jax.ShapeDtypeStruct
    lane_shape = lambda dtype: sds((T // tt, PEER_HEADS, PEER_N_KEYS, tt), dtype)
    return pl.pallas_call(
        _route_body, grid=(T // tm,),
        in_specs=[pl.BlockSpec((tm, D), lambda i: (i, 0)),
                  pl.BlockSpec((1, o_t.shape[1], tm), lambda i: (i // per_seq, 0, i % per_seq)),
                  const((D, D)), const((1, D)), const((D, kq)),
                  const((PEER_HEADS, PEER_N_KEYS, PEER_HALF)), const((PEER_HEADS, PEER_N_KEYS, PEER_HALF))],
        out_specs=[pl.BlockSpec((tm, D), lambda i: (i, 0)), pl.BlockSpec((tm, D), lambda i: (i, 0)),
                   lane_out, lane_out, lane_out, lane_out],
        out_shape=[sds((T, D), F32), sds((T, D), BF16),
                   lane_shape(BF16), lane_shape(BF16), lane_shape(F32), lane_shape(F32)],
        scratch_shapes=[pltpu.VMEM((tm, kq), F32)] + [pltpu.VMEM((RANK_ROWS, tm), F32)] * (2 * ROUTE_HEAD_GROUP),
        compiler_params=_params(("arbitrary",)), name="route",
    )(x2d, o_t, w["wo"], w["gffn"], w["wpq"], w["k1"], w["k2"])


GATE_ROWS = 64
GATE_LANES = 256
CHUNK_I = 2
OUT_CHUNKS = 1


def _row_tile(ref, h, i, ls):
    tile = jnp.broadcast_to(ref[0, h, i:i + 1, ls], (8, GATE_LANES)).astype(BF16)
    return jnp.concatenate([tile] * (GATE_ROWS // 8), axis=0)


def _peer_body(x1_ref, h2_ref, rank_ref, e2_ref, cnt_ref, w1_ref, u_ref, v_ref, gfin_ref, y_ref,
               acc_ref, *chunk_refs, tt, rows_per_step):
    j = pl.program_id(1)

    @pl.when(j == 0)
    def _():
        acc_ref[...] = jnp.zeros_like(acc_ref)

    n_chunks = len(chunk_refs) * OUT_CHUNKS // (OUT_CHUNKS + 1)
    a_refs, act_refs = chunk_refs[:n_chunks], chunk_refs[n_chunks:]
    inv_sqrt2 = 1.0 / math.sqrt(2.0)
    rows = CHUNK_I * PEER_N_KEYS

    def score_matmul(c):
        cs = slice(c * rows, (c + 1) * rows)
        a_refs[c][...] = lax.dot_general(u_ref[cs, :], h2_ref[...], (((1,), (1,)), ((), ())),
                                         preferred_element_type=F32)

    score_matmul(0)
    score_matmul(1)
    for c in range(n_chunks):
        if c + 2 < n_chunks:
            score_matmul(c + 2)
        act_ref = act_refs[c // OUT_CHUNKS]
        act_row = (c % OUT_CHUNKS) * rows
        a = a_refs[c]
        for k in range(CHUNK_I):
            i = c * CHUNK_I + k
            for tc in range(tt // GATE_LANES):
                ls = slice(tc * GATE_LANES, (tc + 1) * GATE_LANES)
                cnt = [_row_tile(cnt_ref, h, i, ls) for h in range(PEER_HEADS)]
                w1 = [_row_tile(w1_ref, h, i, ls) for h in range(PEER_HEADS)]
                for jb in range(PEER_N_KEYS // GATE_ROWS):
                    js = slice(jb * GATE_ROWS, (jb + 1) * GATE_ROWS)
                    gate = None
                    for h in range(PEER_HEADS):
                        sel = rank_ref[0, h, js, ls] < cnt[h]
                        term = jnp.where(sel, e2_ref[0, h, js, ls] * w1[h], 0.0)
                        gate = term if gate is None else gate + term
                    r0 = k * PEER_N_KEYS + jb * GATE_ROWS
                    av = a[r0:r0 + GATE_ROWS, ls]
                    gelu2 = av * (1.0 + lax.erf(av * inv_sqrt2))
                    act_ref[act_row + r0:act_row + r0 + GATE_ROWS, ls] = gelu2.astype(BF16) * gate
        if c % OUT_CHUNKS == OUT_CHUNKS - 1:
            vs = slice((c + 1 - OUT_CHUNKS) * rows, (c + 1) * rows)
            acc_ref[...] += lax.dot_general(act_ref[...], v_ref[vs, :], (((0,), (0,)), ((), ())),
                                            preferred_element_type=F32)

    @pl.when(j == pl.num_programs(1) - 1)
    def _():
        y_ref[...] = _rms(x1_ref[...] + acc_ref[...], gfin_ref[...])


def _peer(x1, h2, rank, e2, cnt, w1, w, tt, rows_per_step):
    T, D = x1.shape
    n_exp = w["v"].shape[0]
    eb = rows_per_step * PEER_N_KEYS
    lane_in = pl.BlockSpec((1, PEER_HEADS, PEER_N_KEYS, tt), lambda t, j: (t, 0, 0, 0))
    row_in = pl.BlockSpec((1, PEER_HEADS, rows_per_step, tt), lambda t, j: (t, 0, j, 0))
    body = functools.partial(_peer_body, tt=tt, rows_per_step=rows_per_step)
    n_chunks = rows_per_step // CHUNK_I
    a_chunks = [pltpu.VMEM((CHUNK_I * PEER_N_KEYS, tt), F32)] * n_chunks
    act_chunks = [pltpu.VMEM((OUT_CHUNKS * CHUNK_I * PEER_N_KEYS, tt), BF16)] * (n_chunks // OUT_CHUNKS)
    return pl.pallas_call(
        body, grid=(T // tt, n_exp // eb),
        in_specs=[pl.BlockSpec((tt, D), lambda t, j: (t, 0)), pl.BlockSpec((tt, D), lambda t, j: (t, 0)),
                  lane_in, lane_in, row_in, row_in,
                  pl.BlockSpec((eb, D), lambda t, j: (j, 0)), pl.BlockSpec((eb, D), lambda t, j: (j, 0)),
                  pl.BlockSpec((1, D), lambda t, j: (0, 0))],
        out_specs=pl.BlockSpec((tt, D), lambda t, j: (t, 0)),
        out_shape=jax.ShapeDtypeStruct((T, D), F32),
        scratch_shapes=[pltpu.VMEM((tt, D), F32)] + a_chunks + act_chunks,
        compiler_params=_params(("arbitrary", "arbitrary")), name="peer",
    )(x1, h2, rank, e2, cnt, w1, w["u"], w["v"], w["gfin"])


def _rope_tables(pos):
    half = MLA_ROPE_DIM // 2
    inv = ROPE_BASE ** (-jnp.arange(half, dtype=F32) / half)
    ang = pos.astype(F32)[:, None] * inv[None, :]
    pad = ((0, 0), (ROPE_LANE, HEAD_ROW - ROPE_LANE - MLA_ROPE_DIM))
    return (jnp.pad(jnp.concatenate([jnp.cos(ang)] * 2, axis=-1), pad),
            jnp.pad(jnp.concatenate([jnp.sin(ang)] * 2, axis=-1), pad))


def _rot_cols(wcols):
    half = wcols.shape[-1] // 2
    return jnp.concatenate([-wcols[..., half:], wcols[..., :half]], axis=-1)


def _prep_weights(g_mix, w_in, b_f, g_q, w_uq, g_kv, w_ukv, w_o, g_ffn, w_pq, sub_keys, peer_u, peer_v, g_final):
    D = w_in.shape[0]
    off_ff = 3 * FOX_WIDTH
    off_cq = off_ff + FOX_HEADS
    off_ckv = off_cq + MLA_Q_LORA
    off_kr = off_ckv + MLA_KV_LORA
    kr_cols = w_in[:, off_kr:off_kr + MLA_ROPE_DIM]
    zeros = lambda n: jnp.zeros((D, n), F32)
    rope_tail = zeros(HEAD_ROW - ROPE_LANE - MLA_ROPE_DIM)
    tail_a = [w_in[:, off_ff:off_cq], zeros(ROPE_LANE - FOX_HEADS), kr_cols, rope_tail]
    tail_b = [zeros(ROPE_LANE), _rot_cols(kr_cols), rope_tail]
    win = jnp.concatenate([w_in[:, :off_ff], w_in[:, off_cq:off_kr]] + tail_a + tail_b, axis=1).astype(BF16)
    bf = jnp.pad(b_f, (0, 128 - FOX_HEADS)).reshape(1, 128)
    per_q = MLA_NOPE_DIM + MLA_ROPE_DIM
    wq = w_uq.reshape(MLA_Q_LORA, MLA_HEADS, per_q)
    q_rope = wq[:, :, MLA_NOPE_DIM:]
    in_row = lambda cols: jnp.pad(cols, ((0, 0), (0, 0), (ROPE_LANE, HEAD_ROW - ROPE_LANE - MLA_ROPE_DIM)))
    wuq = jnp.concatenate([wq[:, :, :MLA_NOPE_DIM].reshape(MLA_Q_LORA, -1), in_row(q_rope).reshape(MLA_Q_LORA, -1),
                           in_row(_rot_cols(q_rope)).reshape(MLA_Q_LORA, -1)], axis=1).astype(BF16)
    sel = np.zeros((128, 2 * FOX_HEADS * HEAD_ROW), np.float32)
    for hd in range(FOX_HEADS):
        q0, k0 = hd * HEAD_ROW + FOX_HEAD_DIM, (FOX_HEADS + hd) * HEAD_ROW + FOX_HEAD_DIM
        for k in range(3):
            sel[k * FOX_HEADS + hd, q0 + k] = 1.0
            sel[3 * FOX_HEADS, q0 + 3 + k] = 1.0
            sel[3 * FOX_HEADS, k0 + k] = 1.0
            sel[k * FOX_HEADS + hd, k0 + 3 + k] = -1.0
    wkv = w_ukv.reshape(MLA_KV_LORA, MLA_HEADS, MLA_NOPE_DIM + MLA_V_DIM)
    wukv = jnp.concatenate([wkv[:, :, :MLA_NOPE_DIM].reshape(MLA_KV_LORA, -1),
                            wkv[:, :, MLA_NOPE_DIM:].reshape(MLA_KV_LORA, -1)], axis=1).astype(BF16)
    return dict(
        gmix=g_mix.reshape(1, -1), win=win, bf=bf, sel=jnp.asarray(sel, BF16), gq=g_q.reshape(1, -1), wuq=wuq, gkv=g_kv.reshape(1, -1),
        wukv=wukv, wo=w_o.astype(BF16), gffn=g_ffn.reshape(1, -1), wpq=w_pq.astype(BF16),
        k1=sub_keys[0].astype(BF16), k2=sub_keys[1].astype(BF16),
        u=peer_u.astype(BF16), v=peer_v.astype(BF16), gfin=g_final.reshape(1, -1))


def _ffn(x, o, w):
    B, S, D = x.shape
    T = B * S
    tm = min(256, T)
    tt = min(512, T)
    if S % tm:
        o = jnp.transpose(o, (1, 0, 2)).reshape(1, o.shape[1], T)
    x1, h2, rank, e2, cnt, w1 = _route(x.reshape(T, D), o, w, tm, tt)
    return _peer(x1, h2, rank, e2, cnt, w1, w, tt, 32).reshape(B, S, D)


def kernel(x_prompt, x_sample, cache_fox_k, cache_fox_v, cache_fox_logf, cache_mla_ckv, cache_mla_krope, g_mix, w_in, b_f, g_q, w_uq, g_kv, w_ukv, w_o, g_ffn, w_pq, peer_sub_keys, peer_u, peer_v, g_final):
    assert g_mix.shape[0] == 1, "single-layer model"
    w = _prep_weights(g_mix[0], w_in[0], b_f[0], g_q[0], w_uq[0], g_kv[0], w_ukv[0], w_o[0], g_ffn[0], w_pq[0],
                      peer_sub_keys[0], peer_u[0], peer_v[0], g_final)
    B, S, D = x_prompt.shape
    Bs, Ss, _ = x_sample.shape
    P = cache_fox_k.shape[2]

    tables_p = _rope_tables(jnp.arange(S, dtype=jnp.int32))
    tm = min(512, S)
    fk, fv, logf, ckv, kr, qh, kh, vh = _proj(x_prompt, jnp.zeros((B, 1, 128), F32), tables_p, w, tm)
    tq = min(256, S)
    o_p = _attention(qh, kh, vh, ATTN_HEAD_GROUP, tq, tq, 0, S)
    y_p = _ffn(x_prompt, o_p, w)

    tk = 256
    L = P + Ss
    kh_c, vh_c, c_last = _cache_rows(
        cache_fox_k[0].reshape(Bs, P, FOX_WIDTH), cache_fox_v[0].reshape(Bs, P, FOX_WIDTH),
        jnp.pad(cache_fox_logf[0], ((0, 0), (0, 0), (0, 128 - FOX_HEADS))),
        cache_mla_ckv[0], cache_mla_krope[0], w["wukv"], L + (-L) % tk)
    tables_s = _rope_tables(P + jnp.arange(Ss, dtype=jnp.int32))
    sfk, sfv, slogf, sckv, skr, sqh, skh, svh = _proj(x_sample, c_last, tables_s, w, Ss)
    kh_all = lax.dynamic_update_slice(kh_c, skh, (0, 0, P, 0))
    vh_all = lax.dynamic_update_slice(vh_c, svh, (0, 0, P, 0))
    o_s = _attention(sqh, kh_all, vh_all, ATTN_HEAD_GROUP, Ss, tk, P, L)
    y_s = _ffn(x_sample, o_s, w)

    return (y_p, y_s,
            fk.reshape(1, B, S, FOX_HEADS, FOX_HEAD_DIM), fv.reshape(1, B, S, FOX_HEADS, FOX_HEAD_DIM),
            logf[None], ckv[None], kr[None],
            sfk.reshape(1, Bs, Ss, FOX_HEADS, FOX_HEAD_DIM), sfv.reshape(1, Bs, Ss, FOX_HEADS, FOX_HEAD_DIM),
            slogf[None], sckv[None], skr[None])
```

```python
import functools
import math

import numpy as np
import jax
import jax.numpy as jnp
from jax import lax
from jax.experimental import pallas as pl
from jax.experimental.pallas import tpu as pltpu

F32 = jnp.float32
BF16 = jnp.bfloat16
LANES = 128

EPS = 1e-6
CHUNK = 64
ROPE_BASE = 10000.0

FOX_HEADS = 8
FOX_HEAD_DIM = 64
FOX_WIDTH = FOX_HEADS * FOX_HEAD_DIM
MLA_HEADS = 8
MLA_NOPE_DIM = 64
MLA_ROPE_DIM = 32
MLA_V_DIM = 64
MLA_Q_LORA = 256
MLA_KV_LORA = 128
ATTN_HEADS = FOX_HEADS + MLA_HEADS
HEAD_ROW = 128
V_DIM = 64

PEER_HEADS = 8
PEER_N_KEYS = 128
PEER_HALF = 128
PEER_TOPK = 16
N_RANK = PEER_TOPK + 1
RANK_ROWS = 24
ROUTE_HEAD_GROUP = 4

Z_FQ = 0
Z_FK = Z_FQ + FOX_WIDTH
Z_FV = Z_FK + FOX_WIDTH
Z_CQ = Z_FV + FOX_WIDTH
Z_CKV = Z_CQ + MLA_Q_LORA
Z_TAIL = Z_CKV + MLA_KV_LORA
Z_TAILB = Z_TAIL + LANES
Z_WIDTH = Z_TAILB + LANES
ROPE_LANE = 64

PROJ_ROWS = 512
ATTN_BLOCK = 256
ROUTE_TOKENS = 256
PEER_TOKENS = 512
PEER_ROWS_PER_STEP = 32

NEG = -1e30
LOG2E = math.log2(math.e)
ATTN_HEAD_GROUP = 8
V7X_VMEM_LIMIT_BYTES = 56 * 1024 * 1024


def _params(sem):
    return pltpu.CompilerParams(dimension_semantics=sem, vmem_limit_bytes=V7X_VMEM_LIMIT_BYTES)


def _rms(x, g):
    return x * lax.rsqrt(jnp.mean(x * x, axis=-1, keepdims=True) + EPS) * g


def _log_sigmoid(x):
    return jnp.minimum(x, 0.0) - jnp.log1p(jnp.exp(-jnp.abs(x)))


def _split3(x):
    hi = x.astype(BF16)
    r1 = x - hi.astype(F32)
    mid = r1.astype(BF16)
    lo = (r1 - mid.astype(F32)).astype(BF16)
    return hi, mid, lo


def _cumsum_rows(tri, x, carry):
    acc = carry
    for piece in _split3(x):
        acc = acc + jnp.dot(tri, piece, preferred_element_type=F32)
    return acc


def _fox_aux(c_col):
    hi, mid, lo = (p.astype(F32) for p in _split3(c_col * LOG2E))
    lane = lax.broadcasted_iota(jnp.int32, (1, HEAD_ROW - FOX_HEAD_DIM), 1)
    q_aux = jnp.where(lane == 0, hi, jnp.where(lane == 1, mid, jnp.where(
        lane == 2, lo, jnp.where(lane < 6, 1.0, 0.0))))
    k_aux = jnp.where(lane < 3, 1.0, jnp.where(lane == 3, -hi, jnp.where(
        lane == 4, -mid, jnp.where(lane == 5, -lo, 0.0))))
    return q_aux, k_aux


def _write_fox_kv(kh_ref, vh_ref, k, v, c_slab):
    rows = slice(0, k.shape[0])
    for h in range(FOX_HEADS):
        sl = slice(h * FOX_HEAD_DIM, (h + 1) * FOX_HEAD_DIM)
        _, k_aux = _fox_aux(c_slab[:, h:h + 1])
        kh_ref[0, h, rows, 0:FOX_HEAD_DIM] = k[:, sl].astype(BF16)
        kh_ref[0, h, rows, FOX_HEAD_DIM:HEAD_ROW] = k_aux.astype(BF16)
        vh_ref[0, h, rows, :] = v[:, sl].astype(BF16)


def _write_mla_kv(kh_ref, vh_ref, kv, krope):
    rows = slice(0, kv.shape[0])
    k_rope = krope.astype(BF16)
    pad = jnp.zeros((kv.shape[0], HEAD_ROW - MLA_NOPE_DIM - MLA_ROPE_DIM), BF16)
    for h in range(MLA_HEADS):
        sl = slice(h * MLA_NOPE_DIM, (h + 1) * MLA_NOPE_DIM)
        kh_ref[0, FOX_HEADS + h, rows, 0:MLA_NOPE_DIM] = kv[:, sl].astype(BF16)
        kh_ref[0, FOX_HEADS + h, rows, MLA_NOPE_DIM:MLA_NOPE_DIM + MLA_ROPE_DIM] = k_rope
        kh_ref[0, FOX_HEADS + h, rows, MLA_NOPE_DIM + MLA_ROPE_DIM:HEAD_ROW] = pad
        vsl = slice(MLA_HEADS * MLA_NOPE_DIM + h * MLA_V_DIM, MLA_HEADS * MLA_NOPE_DIM + (h + 1) * MLA_V_DIM)
        vh_ref[0, FOX_HEADS + h, rows, :] = kv[:, vsl].astype(BF16)


def _proj_body(x_ref, c0_ref, tri_ref, cos_ref, sin_ref, gmix_ref, win_ref, bf_ref, gq_ref, wuq_ref, gkv_ref,
               wukv_ref, sel_ref,
               fk_ref, fv_ref, logf_ref, ckv_ref, kr_ref, qh_ref, kh_ref, vh_ref, carry_ref):
    @pl.when(pl.program_id(1) == 0)
    def _():
        carry_ref[...] = c0_ref[0]

    rows = x_ref.shape[1]
    h = _rms(x_ref[0], gmix_ref[...]).astype(BF16)
    z = jnp.dot(h, win_ref[...], preferred_element_type=F32)

    fq = z[:, Z_FQ:Z_FK]
    fk = z[:, Z_FK:Z_FV]
    fv = z[:, Z_FV:Z_CQ]
    fk_ref[0] = fk
    fv_ref[0] = fv

    tail_a = z[:, Z_TAIL:Z_TAILB]
    tail_b = z[:, Z_TAILB:Z_WIDTH]
    logf = _log_sigmoid(tail_a + bf_ref[...])
    logf_ref[0] = logf[:, 0:FOX_HEADS]
    c_slab = _cumsum_rows(tri_ref[...], logf, carry_ref[...])
    carry_ref[...] = c_slab[rows - 1:rows, :]

    lane = lax.broadcasted_iota(jnp.int32, (1, LANES), 1)
    hi, mid, lo = (p.astype(F32) for p in _split3(c_slab * LOG2E))
    pieces = jnp.where(lane < FOX_HEADS, hi, jnp.where(lane < 2 * FOX_HEADS, pltpu.roll(mid, FOX_HEADS, 1), jnp.where(
        lane < 3 * FOX_HEADS, pltpu.roll(lo, 2 * FOX_HEADS, 1), jnp.where(lane == 3 * FOX_HEADS, 1.0, 0.0))))
    aux = jnp.dot(pieces.astype(BF16), sel_ref[...], preferred_element_type=F32)

    cos = cos_ref[...]
    sin = sin_ref[...]
    krope = tail_a * cos + tail_b * sin
    kr_ref[0] = krope[:, ROPE_LANE:ROPE_LANE + MLA_ROPE_DIM]

    cqn = _rms(z[:, Z_CQ:Z_CKV], gq_ref[...]).astype(BF16)
    q = jnp.dot(cqn, wuq_ref[...], preferred_element_type=F32)
    ckv = _rms(z[:, Z_CKV:Z_TAIL], gkv_ref[...])
    ckv_ref[0] = ckv
    kv = jnp.dot(ckv.astype(BF16), wukv_ref[...], preferred_element_type=F32)

    fox_scale = FOX_HEAD_DIM ** -0.5 * LOG2E
    mla_scale = (MLA_NOPE_DIM + MLA_ROPE_DIM) ** -0.5 * LOG2E
    low = lane < FOX_HEAD_DIM
    nope_w = MLA_HEADS * MLA_NOPE_DIM
    row_w = MLA_HEADS * HEAD_ROW

    def head_src(block, hd):
        slab = block[:, (hd // 2) * LANES:(hd // 2 + 1) * LANES]
        return slab if hd % 2 == 0 else pltpu.roll(slab, 64, 1)

    for hd in range(FOX_HEADS):
        rs = slice(hd * HEAD_ROW, (hd + 1) * HEAD_ROW)
        ks = slice(row_w + hd * HEAD_ROW, row_w + (hd + 1) * HEAD_ROW)
        qh_ref[0, hd] = jnp.where(low, head_src(fq, hd) * fox_scale, aux[:, rs]).astype(BF16)
        kh_ref[0, hd] = jnp.where(low, head_src(fk, hd), aux[:, ks]).astype(BF16)
        vh_ref[0, hd] = fv[:, hd * V_DIM:(hd + 1) * V_DIM].astype(BF16)
    for hd in range(MLA_HEADS):
        rope = (q[:, nope_w + hd * HEAD_ROW:nope_w + (hd + 1) * HEAD_ROW] * cos
                + q[:, nope_w + row_w + hd * HEAD_ROW:nope_w + row_w + (hd + 1) * HEAD_ROW] * sin)
        qh_ref[0, FOX_HEADS + hd] = (jnp.where(low, head_src(q, hd), rope) * mla_scale).astype(BF16)
        kh_ref[0, FOX_HEADS + hd] = jnp.where(low, head_src(kv, hd), krope).astype(BF16)
        vh_ref[0, FOX_HEADS + hd] = kv[:, nope_w + hd * V_DIM:nope_w + (hd + 1) * V_DIM].astype(BF16)


def _proj(x, c0, tables, w, tm):
    B, S, D = x.shape
    cos, sin = tables
    tri = jnp.tril(jnp.ones((tm, tm), F32)).astype(BF16)
    const = lambda shape: pl.BlockSpec(shape, lambda b, t: (0,) * len(shape))
    tab = pl.BlockSpec((tm, LANES), lambda b, t: (t, 0))
    in_specs = [
        pl.BlockSpec((1, tm, D), lambda b, t: (b, t, 0)),
        pl.BlockSpec((1, 1, LANES), lambda b, t: (b, 0, 0)),
        const((tm, tm)), tab, tab,
        const((1, D)), const((D, Z_WIDTH)), const((1, LANES)),
        const((1, MLA_Q_LORA)), const(w["wuq"].shape), const((1, MLA_KV_LORA)), const(w["wukv"].shape),
        const(w["sel"].shape),
    ]
    tok = lambda width: pl.BlockSpec((1, tm, width), lambda b, t: (b, t, 0))
    head = lambda width: pl.BlockSpec((1, ATTN_HEADS, tm, width), lambda b, t: (b, 0, t, 0))
    out_specs = [tok(FOX_WIDTH), tok(FOX_WIDTH), tok(FOX_HEADS), tok(MLA_KV_LORA), tok(MLA_ROPE_DIM),
                 head(HEAD_ROW), head(HEAD_ROW), head(V_DIM)]
    sds = jax.ShapeDtypeStruct
    out_shape = [sds((B, S, FOX_WIDTH), F32), sds((B, S, FOX_WIDTH), F32), sds((B, S, FOX_HEADS), F32),
                 sds((B, S, MLA_KV_LORA), F32), sds((B, S, MLA_ROPE_DIM), F32),
                 sds((B, ATTN_HEADS, S, HEAD_ROW), BF16), sds((B, ATTN_HEADS, S, HEAD_ROW), BF16),
                 sds((B, ATTN_HEADS, S, V_DIM), BF16)]
    return pl.pallas_call(
        _proj_body, grid=(B, S // tm), in_specs=in_specs, out_specs=out_specs, out_shape=out_shape,
        scratch_shapes=[pltpu.VMEM((1, LANES), F32)],
        compiler_params=_params(("arbitrary", "arbitrary")), name="proj",
    )(x, c0, tri, cos, sin, w["gmix"], w["win"], w["bf"], w["gq"], w["wuq"], w["gkv"], w["wukv"], w["sel"])


def _cache_body(ck_ref, cv_ref, clogf_ref, cckv_ref, ckr_ref, tri_ref, wukv_ref, kh_ref, vh_ref, clast_ref):
    rows = ck_ref.shape[1]
    c_slab = _cumsum_rows(tri_ref[...], clogf_ref[0], jnp.zeros((1, LANES), F32))
    clast_ref[0] = c_slab[rows - 1:rows, :]
    _write_fox_kv(kh_ref, vh_ref, ck_ref[0], cv_ref[0], c_slab)
    kv = jnp.dot(cckv_ref[0].astype(BF16), wukv_ref[...], preferred_element_type=F32)
    _write_mla_kv(kh_ref, vh_ref, kv, ckr_ref[0])
    kh_ref[0, :, rows:, :] = jnp.zeros((ATTN_HEADS, kh_ref.shape[2] - rows, HEAD_ROW), BF16)
    vh_ref[0, :, rows:, :] = jnp.zeros((ATTN_HEADS, vh_ref.shape[2] - rows, V_DIM), BF16)


def _cache_rows(ck, cv, clogf_slab, cckv, ckr, wukv, total_rows):
    B, P, _ = ck.shape
    tri = jnp.tril(jnp.ones((P, P), F32)).astype(BF16)
    per_b = lambda width: pl.BlockSpec((1, P, width), lambda b: (b, 0, 0))
    const = lambda shape: pl.BlockSpec(shape, lambda b: (0,) * len(shape))
    head = lambda width: pl.BlockSpec((1, ATTN_HEADS, total_rows, width), lambda b: (b, 0, 0, 0))
    sds = jax.ShapeDtypeStruct
    return pl.pallas_call(
        _cache_body, grid=(B,),
        in_specs=[per_b(FOX_WIDTH), per_b(FOX_WIDTH), per_b(LANES), per_b(MLA_KV_LORA), per_b(MLA_ROPE_DIM),
                  const((P, P)), const(wukv.shape)],
        out_specs=[head(HEAD_ROW), head(V_DIM), pl.BlockSpec((1, 1, LANES), lambda b: (b, 0, 0))],
        out_shape=[sds((B, ATTN_HEADS, total_rows, HEAD_ROW), BF16), sds((B, ATTN_HEADS, total_rows, V_DIM), BF16),
                   sds((B, 1, LANES), F32)],
        compiler_params=_params(("arbitrary",)), name="cache_rows",
    )(ck, cv, clogf_slab, cckv, ckr, tri, wukv)


def _reduce_rows(pair_op, final_op, x):
    while x.shape[0] > 8 and x.shape[0] % 16 == 0:
        half = x.shape[0] // 2
        x = pair_op(x[:half], x[half:])
    return final_op(x, axis=0, keepdims=True)


def _attn_body(q_ref, k_ref, v_ref, o_ref, *, hg, tq, tk, nkv_total, q_off, kv_len):
    grp = pl.program_id(1)
    qi = pl.program_id(2)
    q_lo = q_off + qi * tq
    qpos = q_lo + lax.broadcasted_iota(jnp.int32, (1, tq), 1)
    lim = jnp.where(grp >= FOX_HEADS // hg, qpos | (CHUNK - 1), qpos)
    lim = jnp.minimum(lim, kv_len - 1)
    n_full = jnp.minimum(lax.div(q_lo + 1, tk), nkv_total)
    n_all = jnp.minimum(lax.div((q_lo + tq - 1) | (CHUNK - 1), tk) + 1, nkv_total)

    def step(j, carry, masked):
        off = pl.multiple_of(j * tk, tk)
        if masked:
            visible = off + lax.broadcasted_iota(jnp.int32, (tk, 1), 0) <= lim
        out = []
        scores = [lax.dot_general(k_ref[0, hh, pl.ds(off, tk), :], q_ref[0, hh], (((1,), (1,)), ((), ())),
                                  preferred_element_type=F32) for hh in range(hg)]
        for hh in range(hg):
            m, l, acc = carry[3 * hh:3 * hh + 3]
            vb = v_ref[0, hh, pl.ds(off, tk), :]
            s = scores[hh]
            if masked:
                s = jnp.where(visible, s, NEG)
            m_new = jnp.maximum(m, _reduce_rows(jnp.maximum, jnp.max, s))
            alpha = jnp.exp2(m - m_new)
            p = jnp.exp2(s - m_new)
            l = alpha * l + _reduce_rows(jnp.add, jnp.sum, p)
            pv = lax.dot_general(vb, p.astype(BF16), (((0,), (0,)), ((), ())), preferred_element_type=F32)
            out += [m_new, l, alpha * acc + pv]
        return tuple(out)

    init = (jnp.full((1, tq), NEG, F32), jnp.zeros((1, tq), F32), jnp.zeros((V_DIM, tq), F32)) * hg
    carry = lax.fori_loop(0, n_full, functools.partial(step, masked=False), init)
    carry = lax.fori_loop(n_full, n_all, functools.partial(step, masked=True), carry)
    o_ref[0] = jnp.concatenate([carry[3 * hh + 2] / carry[3 * hh + 1] for hh in range(hg)], axis=0).astype(BF16)


def _attention(qh, kh, vh, hg, tq, tk, q_off, kv_len):
    B, _, Tq, _ = qh.shape
    Tk = kh.shape[2]
    body = functools.partial(_attn_body, hg=hg, tq=tq, tk=tk, nkv_total=Tk // tk, q_off=q_off, kv_len=kv_len)
    return pl.pallas_call(
        body, grid=(B, ATTN_HEADS // hg, Tq // tq),
        in_specs=[pl.BlockSpec((1, hg, tq, HEAD_ROW), lambda b, g, i: (b, g, i, 0)),
                  pl.BlockSpec((1, hg, Tk, HEAD_ROW), lambda b, g, i: (b, g, 0, 0)),
                  pl.BlockSpec((1, hg, Tk, V_DIM), lambda b, g, i: (b, g, 0, 0))],
        out_specs=pl.BlockSpec((1, hg * V_DIM, tq), lambda b, g, i: (b, g, i)),
        out_shape=jax.ShapeDtypeStruct((B, ATTN_HEADS * V_DIM, Tq), BF16),
        compiler_params=_params(("arbitrary", "arbitrary", "arbitrary")), name="attn",
    )(qh, kh, vh)


def _sorting_network(n):
    size = 1 << (n - 1).bit_length()

    def merge(lo, hi, r):
        step = r * 2
        if step < hi - lo:
            yield from merge(lo, hi, step)
            yield from merge(lo + r, hi, step)
            yield from ((i, i + r) for i in range(lo + r, hi - r, step))
        else:
            yield (lo, lo + r)

    def sort(lo, hi):
        if hi - lo >= 1:
            mid = lo + (hi - lo) // 2
            yield from sort(lo, mid)
            yield from sort(mid + 1, hi)
            yield from merge(lo, hi, 1)

    return tuple((i, j) for i, j in sort(0, size - 1) if j < n)


def _largest(groups, n_out):
    g = list(groups)
    for i, j in _sorting_network(len(g)):
        g[i], g[j] = jnp.maximum(g[i], g[j]), jnp.minimum(g[i], g[j])
    below = jnp.full(g[0].shape, -jnp.inf, F32)
    out = []
    for r in range(n_out):
        m = jnp.max(g[0], axis=0, keepdims=True)
        out.append(m)
        hit = g[0] == m
        for k in range(min(n_out - 1 - r, len(g))):
            g[k] = jnp.where(hit, g[k + 1] if k + 1 < len(g) else below, g[k])
    return out


def _top_ranks(work, rows_ref, want_rank):
    rows_ref[...] = jnp.full(rows_ref.shape, -jnp.inf, F32)
    for r, m in enumerate(_largest([work[8 * k:8 * (k + 1), :] for k in range(work.shape[0] // 8)], N_RANK)):
        rows_ref[r:r + 1, :] = m
    if not want_rank:
        return None
    rank = jnp.where(rows_ref[0:1, :] > work, 1.0, 0.0)
    for r in range(1, N_RANK):
        rank = rank + jnp.where(rows_ref[r:r + 1, :] > work, 1.0, 0.0)
    return rank


def _route_body(x_ref, o_ref, wo_ref, gffn_ref, wpq_ref, k1_ref, k2_ref,
                x1_ref, h2_ref, rank_ref, e2_ref, cnt_ref, w1_ref, qp_ref, *rank_rows):
    x1 = x_ref[...] + lax.dot_general(o_ref[0], wo_ref[...], (((0,), (0,)), ((), ())), preferred_element_type=F32)
    x1_ref[...] = x1
    h2 = _rms(x1, gffn_ref[...]).astype(BF16)
    h2_ref[...] = h2
    qp_ref[...] = jnp.dot(h2, wpq_ref[...], preferred_element_type=F32)
    key_dim = 2 * PEER_HALF

    def head(h, a_ref, b_ref):
        qh = qp_ref[:, pl.ds(pl.multiple_of(h * key_dim, key_dim), key_dim)]
        qn = qh * lax.rsqrt(jnp.mean(qh * qh, axis=-1, keepdims=True) + EPS)
        nt = (((1,), (1,)), ((), ()))
        s1 = lax.dot_general(k1_ref[h], qn[:, :PEER_HALF].astype(BF16), nt, preferred_element_type=F32)
        s2 = lax.dot_general(k2_ref[h], qn[:, PEER_HALF:].astype(BF16), nt, preferred_element_type=F32)
        _top_ranks(s1, a_ref, False)
        rank = _top_ranks(s2, b_ref, True)
        groups = [a_ref[0:1, :] + b_ref[8 * k:8 * (k + 1), :] for k in range(RANK_ROWS // 8)]
        groups += [a_ref[r:r + 1, :] + b_ref[0:8, :] for r in range(1, 8)]
        groups += [a_ref[8 * k:8 * (k + 1), :] + b_ref[0:1, :] for k in range(1, RANK_ROWS // 8)]
        pair_scores = _largest(groups, N_RANK)
        top = pair_scores[0]
        zsum = jnp.zeros_like(top)
        for v in pair_scores[:PEER_TOPK]:
            zsum = zsum + jnp.exp(v - top)
        tau = 0.5 * (pair_scores[PEER_TOPK - 1] + pair_scores[PEER_TOPK])
        thr = tau - s1
        cnt = jnp.where(b_ref[0:1, :] >= thr, 1.0, 0.0)
        for c in range(1, N_RANK):
            cnt = cnt + jnp.where(b_ref[c:c + 1, :] >= thr, 1.0, 0.0)
        rank_ref[0, h] = rank.astype(BF16)
        e2_ref[0, h] = jnp.exp(s2 - b_ref[0:1, :]).astype(BF16)
        cnt_ref[0, h] = cnt
        w1_ref[0, h] = (0.5 * jnp.exp(s1 - a_ref[0:1, :])) / zsum

    group = len(rank_rows) // 2

    def head_group(p, carry):
        for k in range(group):
            head(group * p + k, rank_rows[2 * k], rank_rows[2 * k + 1])
        return carry

    lax.fori_loop(0, PEER_HEADS // group, head_group, 0)


def _route(x2d, o_t, w, tm, tt):
    T, D = x2d.shape
    per_seq = o_t.shape[2] // tm
    r = tt // tm
    kq = w["wpq"].shape[1]
    const = lambda shape: pl.BlockSpec(shape, lambda i: (0,) * len(shape))
    lane_out = pl.BlockSpec((1, PEER_HEADS, PEER_N_KEYS, tm), lambda i: (i // r, 0, 0, i % r))
    sds = jax.ShapeDtypeStruct
    lane_shape = lambda dtype: sds((T // tt, PEER_HEADS, PEER_N_KEYS, tt), dtype)
    return pl.pallas_call(
        _route_body, grid=(T // tm,),
        in_specs=[pl.BlockSpec((tm, D), lambda i: (i, 0)),
                  pl.BlockSpec((1, o_t.shape[1], tm), lambda i: (i // per_seq, 0, i % per_seq)),
                  const((D, D)), const((1, D)), const((D, kq)),
                  const((PEER_HEADS, PEER_N_KEYS, PEER_HALF)), const((PEER_HEADS, PEER_N_KEYS, PEER_HALF))],
        out_specs=[pl.BlockSpec((tm, D), lambda i: (i, 0)), pl.BlockSpec((tm, D), lambda i: (i, 0)),
                   lane_out, lane_out, lane_out, lane_out],
        out_shape=[sds((T, D), F32), sds((T, D), BF16),
                   lane_shape(BF16), lane_shape(BF16), lane_shape(F32), lane_shape(F32)],
        scratch_shapes=[pltpu.VMEM((tm, kq), F32)] + [pltpu.VMEM((RANK_ROWS, tm), F32)] * (2 * ROUTE_HEAD_GROUP),
        compiler_params=_params(("arbitrary",)), name="route",
    )(x2d, o_t, w["wo"], w["gffn"], w["wpq"], w["k1"], w["k2"])


GATE_ROWS = 64
GATE_LANES = 256
CHUNK_I = 2


def _row_tile(ref, h, i, ls):
    tile = jnp.broadcast_to(ref[0, h, i:i + 1, ls], (8, GATE_LANES)).astype(BF16)
    return jnp.concatenate([tile] * (GATE_ROWS // 8), axis=0)


def _peer_body(x1_ref, h2_ref, rank_ref, e2_ref, cnt_ref, w1_ref, u_ref, v_ref, gfin_ref, y_ref,
               acc_ref, *chunk_refs, tt):
    j = pl.program_id(1)

    @pl.when(j == 0)
    def _():
        acc_ref[...] = jnp.zeros_like(acc_ref)

    n_chunks = len(chunk_refs) // 2
    a_refs, act_refs = chunk_refs[:n_chunks], chunk_refs[n_chunks:]
    inv_sqrt2 = 1.0 / math.sqrt(2.0)
    rows = CHUNK_I * PEER_N_KEYS

    def score_matmul(c):
        cs = slice(c * rows, (c + 1) * rows)
        a_refs[c][...] = lax.dot_general(u_ref[cs, :], h2_ref[...], (((1,), (1,)), ((), ())),
                                         preferred_element_type=F32)

    score_matmul(0)
    score_matmul(1)
    for c in range(n_chunks):
        if c + 2 < n_chunks:
            score_matmul(c + 2)
        act_ref = act_refs[c]
        a = a_refs[c]
        for k in range(CHUNK_I):
            i = c * CHUNK_I + k
            for tc in range(tt // GATE_LANES):
                ls = slice(tc * GATE_LANES, (tc + 1) * GATE_LANES)
                cnt = [_row_tile(cnt_ref, h, i, ls) for h in range(PEER_HEADS)]
                w1 = [_row_tile(w1_ref, h, i, ls) for h in range(PEER_HEADS)]
                for jb in range(PEER_N_KEYS // GATE_ROWS):
                    js = slice(jb * GATE_ROWS, (jb + 1) * GATE_ROWS)
                    gate = None
                    for h in range(PEER_HEADS):
                        sel = rank_ref[0, h, js, ls] < cnt[h]
                        term = jnp.where(sel, e2_ref[0, h, js, ls] * w1[h], 0.0)
                        gate = term if gate is None else gate + term
                    r0 = k * PEER_N_KEYS + jb * GATE_ROWS
                    av = a[r0:r0 + GATE_ROWS, ls]
                    gelu2 = av * (1.0 + lax.erf(av * inv_sqrt2))
                    act_ref[r0:r0 + GATE_ROWS, ls] = gelu2.astype(BF16) * gate
        acc_ref[...] += lax.dot_general(act_ref[...], v_ref[c * rows:(c + 1) * rows, :], (((0,), (0,)), ((), ())),
                                        preferred_element_type=F32)

    @pl.when(j == pl.num_programs(1) - 1)
    def _():
        y_ref[...] = _rms(x1_ref[...] + acc_ref[...], gfin_ref[...])


def _peer(x1, h2, rank, e2, cnt, w1, w, tt, rows_per_step):
    T, D = x1.shape
    n_exp = w["v"].shape[0]
    eb = rows_per_step * PEER_N_KEYS
    lane_in = pl.BlockSpec((1, PEER_HEADS, PEER_N_KEYS, tt), lambda t, j: (t, 0, 0, 0))
    row_in = pl.BlockSpec((1, PEER_HEADS, rows_per_step, tt), lambda t, j: (t, 0, j, 0))
    body = functools.partial(_peer_body, tt=tt)
    n_chunks = rows_per_step // CHUNK_I
    a_chunks = [pltpu.VMEM((CHUNK_I * PEER_N_KEYS, tt), F32)] * n_chunks
    act_chunks = [pltpu.VMEM((CHUNK_I * PEER_N_KEYS, tt), BF16)] * n_chunks
    return pl.pallas_call(
        body, grid=(T // tt, n_exp // eb),
        in_specs=[pl.BlockSpec((tt, D), lambda t, j: (t, 0)), pl.BlockSpec((tt, D), lambda t, j: (t, 0)),
                  lane_in, lane_in, row_in, row_in,
                  pl.BlockSpec((eb, D), lambda t, j: (j, 0)), pl.BlockSpec((eb, D), lambda t, j: (j, 0)),
                  pl.BlockSpec((1, D), lambda t, j: (0, 0))],
        out_specs=pl.BlockSpec((tt, D), lambda t, j: (t, 0)),
        out_shape=jax.ShapeDtypeStruct((T, D), F32),
        scratch_shapes=[pltpu.VMEM((tt, D), F32)] + a_chunks + act_chunks,
        compiler_params=_params(("arbitrary", "arbitrary")), name="peer",
    )(x1, h2, rank, e2, cnt, w1, w["u"], w["v"], w["gfin"])


def _rope_tables(pos):
    half = MLA_ROPE_DIM // 2
    inv = ROPE_BASE ** (-jnp.arange(half, dtype=F32) / half)
    ang = pos.astype(F32)[:, None] * inv[None, :]
    pad = ((0, 0), (ROPE_LANE, HEAD_ROW - ROPE_LANE - MLA_ROPE_DIM))
    return (jnp.pad(jnp.concatenate([jnp.cos(ang)] * 2, axis=-1), pad),
            jnp.pad(jnp.concatenate([jnp.sin(ang)] * 2, axis=-1), pad))


def _rot_cols(wcols):
    half = wcols.shape[-1] // 2
    return jnp.concatenate([-wcols[..., half:], wcols[..., :half]], axis=-1)


def _prep_weights(g_mix, w_in, b_f, g_q, w_uq, g_kv, w_ukv, w_o, g_ffn, w_pq, sub_keys, peer_u, peer_v, g_final):
    D = w_in.shape[0]
    off_ff = 3 * FOX_WIDTH
    off_cq = off_ff + FOX_HEADS
    off_ckv = off_cq + MLA_Q_LORA
    off_kr = off_ckv + MLA_KV_LORA
    kr_cols = w_in[:, off_kr:off_kr + MLA_ROPE_DIM]
    zeros = lambda n: jnp.zeros((D, n), F32)
    rope_tail = zeros(HEAD_ROW - ROPE_LANE - MLA_ROPE_DIM)
    tail_a = [w_in[:, off_ff:off_cq], zeros(ROPE_LANE - FOX_HEADS), kr_cols, rope_tail]
    tail_b = [zeros(ROPE_LANE), _rot_cols(kr_cols), rope_tail]
    win = jnp.concatenate([w_in[:, :off_ff], w_in[:, off_cq:off_kr]] + tail_a + tail_b, axis=1).astype(BF16)
    bf = jnp.pad(b_f, (0, LANES - FOX_HEADS)).reshape(1, LANES)
    per_q = MLA_NOPE_DIM + MLA_ROPE_DIM
    wq = w_uq.reshape(MLA_Q_LORA, MLA_HEADS, per_q)
    q_rope = wq[:, :, MLA_NOPE_DIM:]
    in_row = lambda cols: jnp.pad(cols, ((0, 0), (0, 0), (ROPE_LANE, HEAD_ROW - ROPE_LANE - MLA_ROPE_DIM)))
    wuq = jnp.concatenate([wq[:, :, :MLA_NOPE_DIM].reshape(MLA_Q_LORA, -1), in_row(q_rope).reshape(MLA_Q_LORA, -1),
                           in_row(_rot_cols(q_rope)).reshape(MLA_Q_LORA, -1)], axis=1).astype(BF16)
    sel = np.zeros((LANES, 2 * FOX_HEADS * HEAD_ROW), np.float32)
    for hd in range(FOX_HEADS):
        q0, k0 = hd * HEAD_ROW + FOX_HEAD_DIM, (FOX_HEADS + hd) * HEAD_ROW + FOX_HEAD_DIM
        for k in range(3):
            sel[k * FOX_HEADS + hd, q0 + k] = 1.0
            sel[3 * FOX_HEADS, q0 + 3 + k] = 1.0
            sel[3 * FOX_HEADS, k0 + k] = 1.0
            sel[k * FOX_HEADS + hd, k0 + 3 + k] = -1.0
    wkv = w_ukv.reshape(MLA_KV_LORA, MLA_HEADS, MLA_NOPE_DIM + MLA_V_DIM)
    wukv = jnp.concatenate([wkv[:, :, :MLA_NOPE_DIM].reshape(MLA_KV_LORA, -1),
                            wkv[:, :, MLA_NOPE_DIM:].reshape(MLA_KV_LORA, -1)], axis=1).astype(BF16)
    return dict(
        gmix=g_mix.reshape(1, -1), win=win, bf=bf, sel=jnp.asarray(sel, BF16), gq=g_q.reshape(1, -1), wuq=wuq,
        gkv=g_kv.reshape(1, -1), wukv=wukv, wo=w_o.astype(BF16), gffn=g_ffn.reshape(1, -1), wpq=w_pq.astype(BF16),
        k1=sub_keys[0].astype(BF16), k2=sub_keys[1].astype(BF16),
        u=peer_u.astype(BF16), v=peer_v.astype(BF16), gfin=g_final.reshape(1, -1))


def _ffn(x, o, w):
    B, S, D = x.shape
    T = B * S
    tm = min(ROUTE_TOKENS, T)
    tt = min(PEER_TOKENS, T)
    if S % tm:
        o = jnp.transpose(o, (1, 0, 2)).reshape(1, o.shape[1], T)
    x1, h2, rank, e2, cnt, w1 = _route(x.reshape(T, D), o, w, tm, tt)
    return _peer(x1, h2, rank, e2, cnt, w1, w, tt, PEER_ROWS_PER_STEP).reshape(B, S, D)


def kernel(x_prompt, x_sample, cache_fox_k, cache_fox_v, cache_fox_logf, cache_mla_ckv, cache_mla_krope, g_mix, w_in, b_f, g_q, w_uq, g_kv, w_ukv, w_o, g_ffn, w_pq, peer_sub_keys, peer_u, peer_v, g_final):
    assert g_mix.shape[0] == 1, "single-layer model"
    w = _prep_weights(g_mix[0], w_in[0], b_f[0], g_q[0], w_uq[0], g_kv[0], w_ukv[0], w_o[0], g_ffn[0], w_pq[0],
                      peer_sub_keys[0], peer_u[0], peer_v[0], g_final)
    B, S, D = x_prompt.shape
    Bs, Ss, _ = x_sample.shape
    P = cache_fox_k.shape[2]

    tables_p = _rope_tables(jnp.arange(S, dtype=jnp.int32))
    tm = min(PROJ_ROWS, S)
    fk, fv, logf, ckv, kr, qh, kh, vh = _proj(x_prompt, jnp.zeros((B, 1, LANES), F32), tables_p, w, tm)
    tq = min(ATTN_BLOCK, S)
    o_p = _attention(qh, kh, vh, ATTN_HEAD_GROUP, tq, tq, 0, S)
    y_p = _ffn(x_prompt, o_p, w)

    tk = ATTN_BLOCK
    L = P + Ss
    kh_c, vh_c, c_last = _cache_rows(
        cache_fox_k[0].reshape(Bs, P, FOX_WIDTH), cache_fox_v[0].reshape(Bs, P, FOX_WIDTH),
        jnp.pad(cache_fox_logf[0], ((0, 0), (0, 0), (0, LANES - FOX_HEADS))),
        cache_mla_ckv[0], cache_mla_krope[0], w["wukv"], L + (-L) % tk)
    tables_s = _rope_tables(P + jnp.arange(Ss, dtype=jnp.int32))
    sfk, sfv, slogf, sckv, skr, sqh, skh, svh = _proj(x_sample, c_last, tables_s, w, Ss)
    kh_all = lax.dynamic_update_slice(kh_c, skh, (0, 0, P, 0))
    vh_all = lax.dynamic_update_slice(vh_c, svh, (0, 0, P, 0))
    o_s = _attention(sqh, kh_all, vh_all, ATTN_HEAD_GROUP, Ss, tk, P, L)
    y_s = _ffn(x_sample, o_s, w)

    return (y_p, y_s,
            fk.reshape(1, B, S, FOX_HEADS, FOX_HEAD_DIM), fv.reshape(1, B, S, FOX_HEADS, FOX_HEAD_DIM),
            logf[None], ckv[None], kr[None],
            sfk.reshape(1, Bs, Ss, FOX_HEADS, FOX_HEAD_DIM), sfv.reshape(1, Bs, Ss, FOX_HEADS, FOX_HEAD_DIM),
            slogf[None], sckv[None], skr[None])
```

```python
import functools
import math

import numpy as np
import jax
import jax.numpy as jnp
from jax import lax
from jax.experimental import pallas as pl
from jax.experimental.pallas import tpu as pltpu

F32 = jnp.float32
BF16 = jnp.bfloat16
LANES = 128

EPS = 1e-6
CHUNK = 64
ROPE_BASE = 10000.0

FOX_HEADS = 8
FOX_HEAD_DIM = 64
FOX_WIDTH = FOX_HEADS * FOX_HEAD_DIM
MLA_HEADS = 8
MLA_NOPE_DIM = 64
MLA_ROPE_DIM = 32
MLA_V_DIM = 64
MLA_Q_LORA = 256
MLA_KV_LORA = 128
ATTN_HEADS = FOX_HEADS + MLA_HEADS
HEAD_ROW = 128
V_DIM = 64

PEER_HEADS = 8
PEER_N_KEYS = 128
PEER_HALF = 128
PEER_TOPK = 16
N_RANK = PEER_TOPK + 1
RANK_ROWS = 24
ROUTE_HEAD_GROUP = 4

Z_FQ = 0
Z_FK = Z_FQ + FOX_WIDTH
Z_FV = Z_FK + FOX_WIDTH
Z_CQ = Z_FV + FOX_WIDTH
Z_CKV = Z_CQ + MLA_Q_LORA
Z_TAIL = Z_CKV + MLA_KV_LORA
Z_TAILB = Z_TAIL + LANES
Z_WIDTH = Z_TAILB + LANES
ROPE_LANE = 64

PROJ_ROWS = 512
ATTN_BLOCK = 256
ROUTE_TOKENS = 256
PEER_TOKENS = 512
PEER_ROWS_PER_STEP = 32

NEG = -1e30
LOG2E = math.log2(math.e)
ATTN_HEAD_GROUP = 8
V7X_VMEM_LIMIT_BYTES = 56 * 1024 * 1024


def _params(sem):
    return pltpu.CompilerParams(dimension_semantics=sem, vmem_limit_bytes=V7X_VMEM_LIMIT_BYTES)


def _rms(x, g):
    return x * lax.rsqrt(jnp.mean(x * x, axis=-1, keepdims=True) + EPS) * g


def _log_sigmoid(x):
    return jnp.minimum(x, 0.0) - jnp.log1p(jnp.exp(-jnp.abs(x)))


def _split3(x):
    hi = x.astype(BF16)
    r1 = x - hi.astype(F32)
    mid = r1.astype(BF16)
    lo = (r1 - mid.astype(F32)).astype(BF16)
    return hi, mid, lo


def _cumsum_rows(tri, x, carry):
    acc = carry
    for piece in _split3(x):
        acc = acc + jnp.dot(tri, piece, preferred_element_type=F32)
    return acc


def _fox_aux(c_col):
    hi, mid, lo = (p.astype(F32) for p in _split3(c_col * LOG2E))
    lane = lax.broadcasted_iota(jnp.int32, (1, HEAD_ROW - FOX_HEAD_DIM), 1)
    q_aux = jnp.where(lane == 0, hi, jnp.where(lane == 1, mid, jnp.where(
        lane == 2, lo, jnp.where(lane < 6, 1.0, 0.0))))
    k_aux = jnp.where(lane < 3, 1.0, jnp.where(lane == 3, -hi, jnp.where(
        lane == 4, -mid, jnp.where(lane == 5, -lo, 0.0))))
    return q_aux, k_aux


def _write_fox_kv(kh_ref, vh_ref, k, v, c_slab):
    rows = slice(0, k.shape[0])
    for h in range(FOX_HEADS):
        sl = slice(h * FOX_HEAD_DIM, (h + 1) * FOX_HEAD_DIM)
        _, k_aux = _fox_aux(c_slab[:, h:h + 1])
        kh_ref[0, h, rows, 0:FOX_HEAD_DIM] = k[:, sl].astype(BF16)
        kh_ref[0, h, rows, FOX_HEAD_DIM:HEAD_ROW] = k_aux.astype(BF16)
        vh_ref[0, h, rows, :] = v[:, sl].astype(BF16)


def _write_mla_kv(kh_ref, vh_ref, kv, krope):
    rows = slice(0, kv.shape[0])
    k_rope = krope.astype(BF16)
    pad = jnp.zeros((kv.shape[0], HEAD_ROW - MLA_NOPE_DIM - MLA_ROPE_DIM), BF16)
    for h in range(MLA_HEADS):
        sl = slice(h * MLA_NOPE_DIM, (h + 1) * MLA_NOPE_DIM)
        kh_ref[0, FOX_HEADS + h, rows, 0:MLA_NOPE_DIM] = kv[:, sl].astype(BF16)
        kh_ref[0, FOX_HEADS + h, rows, MLA_NOPE_DIM:MLA_NOPE_DIM + MLA_ROPE_DIM] = k_rope
        kh_ref[0, FOX_HEADS + h, rows, MLA_NOPE_DIM + MLA_ROPE_DIM:HEAD_ROW] = pad
        vsl = slice(MLA_HEADS * MLA_NOPE_DIM + h * MLA_V_DIM, MLA_HEADS * MLA_NOPE_DIM + (h + 1) * MLA_V_DIM)
        vh_ref[0, FOX_HEADS + h, rows, :] = kv[:, vsl].astype(BF16)


def _proj_body(x_ref, c0_ref, tri_ref, cos_ref, sin_ref, gmix_ref, win_ref, bf_ref, gq_ref, wuq_ref, gkv_ref,
               wukv_ref, sel_ref,
               fk_ref, fv_ref, logf_ref, ckv_ref, kr_ref, qh_ref, kh_ref, vh_ref, carry_ref):
    @pl.when(pl.program_id(1) == 0)
    def _():
        carry_ref[...] = c0_ref[0]

    rows = x_ref.shape[1]
    h = _rms(x_ref[0], gmix_ref[...]).astype(BF16)
    z = jnp.dot(h, win_ref[...], preferred_element_type=F32)

    fq = z[:, Z_FQ:Z_FK]
    fk = z[:, Z_FK:Z_FV]
    fv = z[:, Z_FV:Z_CQ]
    fk_ref[0] = fk
    fv_ref[0] = fv

    tail_a = z[:, Z_TAIL:Z_TAILB]
    tail_b = z[:, Z_TAILB:Z_WIDTH]
    logf = _log_sigmoid(tail_a + bf_ref[...])
    logf_ref[0] = logf[:, 0:FOX_HEADS]
    c_slab = _cumsum_rows(tri_ref[...], logf, carry_ref[...])
    carry_ref[...] = c_slab[rows - 1:rows, :]

    lane = lax.broadcasted_iota(jnp.int32, (1, LANES), 1)
    hi, mid, lo = (p.astype(F32) for p in _split3(c_slab * LOG2E))
    pieces = jnp.where(lane < FOX_HEADS, hi, jnp.where(lane < 2 * FOX_HEADS, pltpu.roll(mid, FOX_HEADS, 1), jnp.where(
        lane < 3 * FOX_HEADS, pltpu.roll(lo, 2 * FOX_HEADS, 1), jnp.where(lane == 3 * FOX_HEADS, 1.0, 0.0))))
    aux = jnp.dot(pieces.astype(BF16), sel_ref[...], preferred_element_type=F32)

    cos = cos_ref[...]
    sin = sin_ref[...]
    krope = tail_a * cos + tail_b * sin
    kr_ref[0] = krope[:, ROPE_LANE:ROPE_LANE + MLA_ROPE_DIM]

    cqn = _rms(z[:, Z_CQ:Z_CKV], gq_ref[...]).astype(BF16)
    q = jnp.dot(cqn, wuq_ref[...], preferred_element_type=F32)
    ckv = _rms(z[:, Z_CKV:Z_TAIL], gkv_ref[...])
    ckv_ref[0] = ckv
    kv = jnp.dot(ckv.astype(BF16), wukv_ref[...], preferred_element_type=F32)

    fox_scale = FOX_HEAD_DIM ** -0.5 * LOG2E
    mla_scale = (MLA_NOPE_DIM + MLA_ROPE_DIM) ** -0.5 * LOG2E
    low = lane < FOX_HEAD_DIM
    nope_w = MLA_HEADS * MLA_NOPE_DIM
    row_w = MLA_HEADS * HEAD_ROW

    def head_src(block, hd):
        slab = block[:, (hd // 2) * LANES:(hd // 2 + 1) * LANES]
        return slab if hd % 2 == 0 else pltpu.roll(slab, 64, 1)

    for hd in range(FOX_HEADS):
        rs = slice(hd * HEAD_ROW, (hd + 1) * HEAD_ROW)
        ks = slice(row_w + hd * HEAD_ROW, row_w + (hd + 1) * HEAD_ROW)
        qh_ref[0, hd] = jnp.where(low, head_src(fq, hd) * fox_scale, aux[:, rs]).astype(BF16)
        kh_ref[0, hd] = jnp.where(low, head_src(fk, hd), aux[:, ks]).astype(BF16)
        vh_ref[0, hd] = fv[:, hd * V_DIM:(hd + 1) * V_DIM].astype(BF16)
    for hd in range(MLA_HEADS):
        rope = (q[:, nope_w + hd * HEAD_ROW:nope_w + (hd + 1) * HEAD_ROW] * cos
                + q[:, nope_w + row_w + hd * HEAD_ROW:nope_w + row_w + (hd + 1) * HEAD_ROW] * sin)
        qh_ref[0, FOX_HEADS + hd] = (jnp.where(low, head_src(q, hd), rope) * mla_scale).astype(BF16)
        kh_ref[0, FOX_HEADS + hd] = jnp.where(low, head_src(kv, hd), krope).astype(BF16)
        vh_ref[0, FOX_HEADS + hd] = kv[:, nope_w + hd * V_DIM:nope_w + (hd + 1) * V_DIM].astype(BF16)


def _proj(x, c0, tables, w, tm):
    B, S, D = x.shape
    cos, sin = tables
    tri = jnp.tril(jnp.ones((tm, tm), F32)).astype(BF16)
    const = lambda shape: pl.BlockSpec(shape, lambda b, t: (0,) * len(shape))
    tab = pl.BlockSpec((tm, LANES), lambda b, t: (t, 0))
    in_specs = [
        pl.BlockSpec((1, tm, D), lambda b, t: (b, t, 0)),
        pl.BlockSpec((1, 1, LANES), lambda b, t: (b, 0, 0)),
        const((tm, tm)), tab, tab,
        const((1, D)), const((D, Z_WIDTH)), const((1, LANES)),
        const((1, MLA_Q_LORA)), const(w["wuq"].shape), const((1, MLA_KV_LORA)), const(w["wukv"].shape),
        const(w["sel"].shape),
    ]
    tok = lambda width: pl.BlockSpec((1, tm, width), lambda b, t: (b, t, 0))
    head = lambda width: pl.BlockSpec((1, ATTN_HEADS, tm, width), lambda b, t: (b, 0, t, 0))
    out_specs = [tok(FOX_WIDTH), tok(FOX_WIDTH), tok(FOX_HEADS), tok(MLA_KV_LORA), tok(MLA_ROPE_DIM),
                 head(HEAD_ROW), head(HEAD_ROW), head(V_DIM)]
    sds = jax.ShapeDtypeStruct
    out_shape = [sds((B, S, FOX_WIDTH), F32), sds((B, S, FOX_WIDTH), F32), sds((B, S, FOX_HEADS), F32),
                 sds((B, S, MLA_KV_LORA), F32), sds((B, S, MLA_ROPE_DIM), F32),
                 sds((B, ATTN_HEADS, S, HEAD_ROW), BF16), sds((B, ATTN_HEADS, S, HEAD_ROW), BF16),
                 sds((B, ATTN_HEADS, S, V_DIM), BF16)]
    return pl.pallas_call(
        _proj_body, grid=(B, S // tm), in_specs=in_specs, out_specs=out_specs, out_shape=out_shape,
        scratch_shapes=[pltpu.VMEM((1, LANES), F32)],
        compiler_params=_params(("arbitrary", "arbitrary")), name="proj",
    )(x, c0, tri, cos, sin, w["gmix"], w["win"], w["bf"], w["gq"], w["wuq"], w["gkv"], w["wukv"], w["sel"])


def _cache_body(ck_ref, cv_ref, clogf_ref, cckv_ref, ckr_ref, tri_ref, wukv_ref, kh_ref, vh_ref, clast_ref):
    rows = ck_ref.shape[1]
    c_slab = _cumsum_rows(tri_ref[...], clogf_ref[0], jnp.zeros((1, LANES), F32))
    clast_ref[0] = c_slab[rows - 1:rows, :]
    _write_fox_kv(kh_ref, vh_ref, ck_ref[0], cv_ref[0], c_slab)
    kv = jnp.dot(cckv_ref[0].astype(BF16), wukv_ref[...], preferred_element_type=F32)
    _write_mla_kv(kh_ref, vh_ref, kv, ckr_ref[0])
    kh_ref[0, :, rows:, :] = jnp.zeros((ATTN_HEADS, kh_ref.shape[2] - rows, HEAD_ROW), BF16)
    vh_ref[0, :, rows:, :] = jnp.zeros((ATTN_HEADS, vh_ref.shape[2] - rows, V_DIM), BF16)


def _cache_rows(ck, cv, clogf_slab, cckv, ckr, wukv, total_rows):
    B, P, _ = ck.shape
    tri = jnp.tril(jnp.ones((P, P), F32)).astype(BF16)
    per_b = lambda width: pl.BlockSpec((1, P, width), lambda b: (b, 0, 0))
    const = lambda shape: pl.BlockSpec(shape, lambda b: (0,) * len(shape))
    head = lambda width: pl.BlockSpec((1, ATTN_HEADS, total_rows, width), lambda b: (b, 0, 0, 0))
    sds = jax.ShapeDtypeStruct
    return pl.pallas_call(
        _cache_body, grid=(B,),
        in_specs=[per_b(FOX_WIDTH), per_b(FOX_WIDTH), per_b(LANES), per_b(MLA_KV_LORA), per_b(MLA_ROPE_DIM),
                  const((P, P)), const(wukv.shape)],
        out_specs=[head(HEAD_ROW), head(V_DIM), pl.BlockSpec((1, 1, LANES), lambda b: (b, 0, 0))],
        out_shape=[sds((B, ATTN_HEADS, total_rows, HEAD_ROW), BF16), sds((B, ATTN_HEADS, total_rows, V_DIM), BF16),
                   sds((B, 1, LANES), F32)],
        compiler_params=_params(("arbitrary",)), name="cache_rows",
    )(ck, cv, clogf_slab, cckv, ckr, tri, wukv)


def _reduce_rows(pair_op, final_op, x):
    while x.shape[0] > 8 and x.shape[0] % 16 == 0:
        half = x.shape[0] // 2
        x = pair_op(x[:half], x[half:])
    return final_op(x, axis=0, keepdims=True)


def _attn_body(q_ref, k_ref, v_ref, o_ref, *, hg, tq, tk, nkv_total, q_off, kv_len):
    grp = pl.program_id(1)
    qi = pl.program_id(2)
    q_lo = q_off + qi * tq
    qpos = q_lo + lax.broadcasted_iota(jnp.int32, (1, tq), 1)
    lim = jnp.where(grp >= FOX_HEADS // hg, qpos | (CHUNK - 1), qpos)
    lim = jnp.minimum(lim, kv_len - 1)
    n_full = jnp.minimum(lax.div(q_lo + 1, tk), nkv_total)
    n_all = jnp.minimum(lax.div((q_lo + tq - 1) | (CHUNK - 1), tk) + 1, nkv_total)

    def step(j, carry, masked):
        off = pl.multiple_of(j * tk, tk)
        if masked:
            visible = off + lax.broadcasted_iota(jnp.int32, (tk, 1), 0) <= lim
        out = []
        scores = [lax.dot_general(k_ref[0, hh, pl.ds(off, tk), :], q_ref[0, hh], (((1,), (1,)), ((), ())),
                                  preferred_element_type=F32) for hh in range(hg)]
        for hh in range(hg):
            m, l, acc = carry[3 * hh:3 * hh + 3]
            vb = v_ref[0, hh, pl.ds(off, tk), :]
            s = scores[hh]
            if masked:
                s = jnp.where(visible, s, NEG)
            m_new = jnp.maximum(m, _reduce_rows(jnp.maximum, jnp.max, s))
            alpha = jnp.exp2(m - m_new)
            p = jnp.exp2(s - m_new)
            l = alpha * l + _reduce_rows(jnp.add, jnp.sum, p)
            pv = lax.dot_general(vb, p.astype(BF16), (((0,), (0,)), ((), ())), preferred_element_type=F32)
            out += [m_new, l, alpha * acc + pv]
        return tuple(out)

    init = (jnp.full((1, tq), NEG, F32), jnp.zeros((1, tq), F32), jnp.zeros((V_DIM, tq), F32)) * hg
    carry = lax.fori_loop(0, n_full, functools.partial(step, masked=False), init)
    carry = lax.fori_loop(n_full, n_all, functools.partial(step, masked=True), carry)
    o_ref[0] = jnp.concatenate([carry[3 * hh + 2] / carry[3 * hh + 1] for hh in range(hg)], axis=0).astype(BF16)


def _attention(qh, kh, vh, hg, tq, tk, q_off, kv_len):
    B, _, Tq, _ = qh.shape
    Tk = kh.shape[2]
    body = functools.partial(_attn_body, hg=hg, tq=tq, tk=tk, nkv_total=Tk // tk, q_off=q_off, kv_len=kv_len)
    return pl.pallas_call(
        body, grid=(B, ATTN_HEADS // hg, Tq // tq),
        in_specs=[pl.BlockSpec((1, hg, tq, HEAD_ROW), lambda b, g, i: (b, g, i, 0)),
                  pl.BlockSpec((1, hg, Tk, HEAD_ROW), lambda b, g, i: (b, g, 0, 0)),
                  pl.BlockSpec((1, hg, Tk, V_DIM), lambda b, g, i: (b, g, 0, 0))],
        out_specs=pl.BlockSpec((1, hg * V_DIM, tq), lambda b, g, i: (b, g, i)),
        out_shape=jax.ShapeDtypeStruct((B, ATTN_HEADS * V_DIM, Tq), BF16),
        compiler_params=_params(("arbitrary", "arbitrary", "arbitrary")), name="attn",
    )(qh, kh, vh)


def _sorting_network(n):
    size = 1 << (n - 1).bit_length()

    def merge(lo, hi, r):
        step = r * 2
        if step < hi - lo:
            yield from merge(lo, hi, step)
            yield from merge(lo + r, hi, step)
            yield from ((i, i + r) for i in range(lo + r, hi - r, step))
        else:
            yield (lo, lo + r)

    def sort(lo, hi):
        if hi - lo >= 1:
            mid = lo + (hi - lo) // 2
            yield from sort(lo, mid)
            yield from sort(mid + 1, hi)
            yield from merge(lo, hi, 1)

    return tuple((i, j) for i, j in sort(0, size - 1) if j < n)


def _largest(groups, n_out):
    g = list(groups)
    for i, j in _sorting_network(len(g)):
        g[i], g[j] = jnp.maximum(g[i], g[j]), jnp.minimum(g[i], g[j])
    below = jnp.full(g[0].shape, -jnp.inf, F32)
    out = []
    for r in range(n_out):
        m = jnp.max(g[0], axis=0, keepdims=True)
        out.append(m)
        hit = g[0] == m
        for k in range(min(n_out - 1 - r, len(g))):
            g[k] = jnp.where(hit, g[k + 1] if k + 1 < len(g) else below, g[k])
    return out


def _top_ranks(work, rows_ref, want_rank):
    rows_ref[...] = jnp.full(rows_ref.shape, -jnp.inf, F32)
    for r, m in enumerate(_largest([work[8 * k:8 * (k + 1), :] for k in range(work.shape[0] // 8)], N_RANK)):
        rows_ref[r:r + 1, :] = m
    if not want_rank:
        return None
    rank = jnp.where(rows_ref[0:1, :] > work, 1.0, 0.0)
    for r in range(1, N_RANK):
        rank = rank + jnp.where(rows_ref[r:r + 1, :] > work, 1.0, 0.0)
    return rank


def _route_body(x_ref, o_ref, wo_ref, gffn_ref, wpq_ref, k1_ref, k2_ref,
                x1_ref, h2_ref, rank_ref, e2_ref, cnt_ref, w1_ref, qp_ref, *rank_rows):
    x1 = x_ref[...] + lax.dot_general(o_ref[0], wo_ref[...], (((0,), (0,)), ((), ())), preferred_element_type=F32)
    x1_ref[...] = x1
    h2 = _rms(x1, gffn_ref[...]).astype(BF16)
    h2_ref[...] = h2
    qp_ref[...] = jnp.dot(h2, wpq_ref[...], preferred_element_type=F32)
    key_dim = 2 * PEER_HALF

    def head(h, a_ref, b_ref):
        qh = qp_ref[:, pl.ds(pl.multiple_of(h * key_dim, key_dim), key_dim)]
        qn = qh * lax.rsqrt(jnp.mean(qh * qh, axis=-1, keepdims=True) + EPS)
        nt = (((1,), (1,)), ((), ()))
        s1 = lax.dot_general(k1_ref[h], qn[:, :PEER_HALF].astype(BF16), nt, preferred_element_type=F32)
        s2 = lax.dot_general(k2_ref[h], qn[:, PEER_HALF:].astype(BF16), nt, preferred_element_type=F32)
        _top_ranks(s1, a_ref, False)
        rank = _top_ranks(s2, b_ref, True)
        groups = [a_ref[0:1, :] + b_ref[8 * k:8 * (k + 1), :] for k in range(RANK_ROWS // 8)]
        groups += [a_ref[r:r + 1, :] + b_ref[0:8, :] for r in range(1, 8)]
        groups += [a_ref[8 * k:8 * (k + 1), :] + b_ref[0:1, :] for k in range(1, RANK_ROWS // 8)]
        pair_scores = _largest(groups, N_RANK)
        top = pair_scores[0]
        zsum = jnp.zeros_like(top)
        for v in pair_scores[:PEER_TOPK]:
            zsum = zsum + jnp.exp(v - top)
        tau = 0.5 * (pair_scores[PEER_TOPK - 1] + pair_scores[PEER_TOPK])
        thr = tau - s1
        cnt = jnp.where(b_ref[0:1, :] >= thr, 1.0, 0.0)
        for c in range(1, N_RANK):
            cnt = cnt + jnp.where(b_ref[c:c + 1, :] >= thr, 1.0, 0.0)
        rank_ref[0, h] = rank.astype(BF16)
        e2_ref[0, h] = jnp.exp(s2 - b_ref[0:1, :]).astype(BF16)
        cnt_ref[0, h] = cnt
        w1_ref[0, h] = (0.5 * jnp.exp(s1 - a_ref[0:1, :])) / zsum

    group = len(rank_rows) // 2

    def head_group(p, carry):
        for k in range(group):
            head(group * p + k, rank_rows[2 * k], rank_rows[2 * k + 1])
        return carry

    lax.fori_loop(0, PEER_HEADS // group, head_group, 0)


def _route(x2d, o_t, w, tm, tt):
    T, D = x2d.shape
    per_seq = o_t.shape[2] // tm
    r = tt // tm
    kq = w["wpq"].shape[1]
    const = lambda shape: pl.BlockSpec(shape, lambda i: (0,) * len(shape))
    lane_out = pl.BlockSpec((1, PEER_HEADS, PEER_N_KEYS, tm), lambda i: (i // r, 0, 0, i % r))
    sds = jax.ShapeDtypeStruct
    lane_shape = lambda dtype: sds((T // tt, PEER_HEADS, PEER_N_KEYS, tt), dtype)
    return pl.pallas_call(
        _route_body, grid=(T // tm,),
        in_specs=[pl.BlockSpec((tm, D), lambda i: (i, 0)),
                  pl.BlockSpec((1, o_t.shape[1], tm), lambda i: (i // per_seq, 0, i % per_seq)),
                  const((D, D)), const((1, D)), const((D, kq)),
                  const((PEER_HEADS, PEER_N_KEYS, PEER_HALF)), const((PEER_HEADS, PEER_N_KEYS, PEER_HALF))],
        out_specs=[pl.BlockSpec((tm, D), lambda i: (i, 0)), pl.BlockSpec((tm, D), lambda i: (i, 0)),
                   lane_out, lane_out, lane_out, lane_out],
        out_shape=[sds((T, D), F32), sds((T, D), BF16),
                   lane_shape(BF16), lane_shape(BF16), lane_shape(F32), lane_shape(F32)],
        scratch_shapes=[pltpu.VMEM((tm, kq), F32)] + [pltpu.VMEM((RANK_ROWS, tm), F32)] * (2 * ROUTE_HEAD_GROUP),
        compiler_params=_params(("arbitrary",)), name="route",
    )(x2d, o_t, w["wo"], w["gffn"], w["wpq"], w["k1"], w["k2"])


GATE_ROWS = 64
GATE_LANES = 256
CHUNK_I = 2


def _row_tile(ref, h, i, ls):
    tile = jnp.broadcast_to(ref[0, h, i:i + 1, ls], (8, GATE_LANES)).astype(BF16)
    return jnp.concatenate([tile] * (GATE_ROWS // 8), axis=0)


def _peer_body(x1_ref, h2_ref, rank_ref, e2_ref, cnt_ref, w1_ref, u_ref, v_ref, gfin_ref, y_ref,
               acc_ref, *chunk_refs, tt):
    j = pl.program_id(1)

    @pl.when(j == 0)
    def _():
        acc_ref[...] = jnp.zeros_like(acc_ref)

    n_chunks = len(chunk_refs) // 2
    a_refs, act_refs = chunk_refs[:n_chunks], chunk_refs[n_chunks:]
    inv_sqrt2 = 1.0 / math.sqrt(2.0)
    rows = CHUNK_I * PEER_N_KEYS

    def score_matmul(c):
        cs = slice(c * rows, (c + 1) * rows)
        a_refs[c][...] = lax.dot_general(u_ref[cs, :], h2_ref[...], (((1,), (1,)), ((), ())),
                                         preferred_element_type=F32)

    score_matmul(0)
    score_matmul(1)
    for c in range(n_chunks):
        if c + 2 < n_chunks:
            score_matmul(c + 2)
        act_ref = act_refs[c]
        a = a_refs[c]
        for k in range(CHUNK_I):
            i = c * CHUNK_I + k
            for tc in range(tt // GATE_LANES):
                ls = slice(tc * GATE_LANES, (tc + 1) * GATE_LANES)
                cnt = [_row_tile(cnt_ref, h, i, ls) for h in range(PEER_HEADS)]
                w1 = [_row_tile(w1_ref, h, i, ls) for h in range(PEER_HEADS)]
                for jb in range(PEER_N_KEYS // GATE_ROWS):
                    js = slice(jb * GATE_ROWS, (jb + 1) * GATE_ROWS)
                    gate = None
                    for h in range(PEER_HEADS):
                        sel = rank_ref[0, h, js, ls] < cnt[h]
                        term = jnp.where(sel, e2_ref[0, h, js, ls] * w1[h], 0.0)
                        gate = term if gate is None else gate + term
                    r0 = k * PEER_N_KEYS + jb * GATE_ROWS
                    av = a[r0:r0 + GATE_ROWS, ls].astype(BF16)
                    gelu2 = av * (1.0 + lax.erf(av * inv_sqrt2))
                    act_ref[r0:r0 + GATE_ROWS, ls] = gelu2 * gate
        acc_ref[...] += lax.dot_general(act_ref[...], v_ref[c * rows:(c + 1) * rows, :], (((0,), (0,)), ((), ())),
                                        preferred_element_type=F32)

    @pl.when(j == pl.num_programs(1) - 1)
    def _():
        y_ref[...] = _rms(x1_ref[...] + acc_ref[...], gfin_ref[...])


def _peer(x1, h2, rank, e2, cnt, w1, w, tt, rows_per_step):
    T, D = x1.shape
    n_exp = w["v"].shape[0]
    eb = rows_per_step * PEER_N_KEYS
    lane_in = pl.BlockSpec((1, PEER_HEADS, PEER_N_KEYS, tt), lambda t, j: (t, 0, 0, 0))
    row_in = pl.BlockSpec((1, PEER_HEADS, rows_per_step, tt), lambda t, j: (t, 0, j, 0))
    body = functools.partial(_peer_body, tt=tt)
    n_chunks = rows_per_step // CHUNK_I
    a_chunks = [pltpu.VMEM((CHUNK_I * PEER_N_KEYS, tt), F32)] * n_chunks
    act_chunks = [pltpu.VMEM((CHUNK_I * PEER_N_KEYS, tt), BF16)] * n_chunks
    return pl.pallas_call(
        body, grid=(T // tt, n_exp // eb),
        in_specs=[pl.BlockSpec((tt, D), lambda t, j: (t, 0)), pl.BlockSpec((tt, D), lambda t, j: (t, 0)),
                  lane_in, lane_in, row_in, row_in,
                  pl.BlockSpec((eb, D), lambda t, j: (j, 0)), pl.BlockSpec((eb, D), lambda t, j: (j, 0)),
                  pl.BlockSpec((1, D), lambda t, j: (0, 0))],
        out_specs=pl.BlockSpec((tt, D), lambda t, j: (t, 0)),
        out_shape=jax.ShapeDtypeStruct((T, D), F32),
        scratch_shapes=[pltpu.VMEM((tt, D), F32)] + a_chunks + act_chunks,
        compiler_params=_params(("arbitrary", "arbitrary")), name="peer",
    )(x1, h2, rank, e2, cnt, w1, w["u"], w["v"], w["gfin"])


def _rope_tables(pos):
    half = MLA_ROPE_DIM // 2
    inv = ROPE_BASE ** (-jnp.arange(half, dtype=F32) / half)
    ang = pos.astype(F32)[:, None] * inv[None, :]
    pad = ((0, 0), (ROPE_LANE, HEAD_ROW - ROPE_LANE - MLA_ROPE_DIM))
    return (jnp.pad(jnp.concatenate([jnp.cos(ang)] * 2, axis=-1), pad),
            jnp.pad(jnp.concatenate([jnp.sin(ang)] * 2, axis=-1), pad))


def _rot_cols(wcols):
    half = wcols.shape[-1] // 2
    return jnp.concatenate([-wcols[..., half:], wcols[..., :half]], axis=-1)


def _prep_weights(g_mix, w_in, b_f, g_q, w_uq, g_kv, w_ukv, w_o, g_ffn, w_pq, sub_keys, peer_u, peer_v, g_final):
    D = w_in.shape[0]
    off_ff = 3 * FOX_WIDTH
    off_cq = off_ff + FOX_HEADS
    off_ckv = off_cq + MLA_Q_LORA
    off_kr = off_ckv + MLA_KV_LORA
    kr_cols = w_in[:, off_kr:off_kr + MLA_ROPE_DIM]
    zeros = lambda n: jnp.zeros((D, n), F32)
    rope_tail = zeros(HEAD_ROW - ROPE_LANE - MLA_ROPE_DIM)
    tail_a = [w_in[:, off_ff:off_cq], zeros(ROPE_LANE - FOX_HEADS), kr_cols, rope_tail]
    tail_b = [zeros(ROPE_LANE), _rot_cols(kr_cols), rope_tail]
    win = jnp.concatenate([w_in[:, :off_ff], w_in[:, off_cq:off_kr]] + tail_a + tail_b, axis=1).astype(BF16)
    bf = jnp.pad(b_f, (0, LANES - FOX_HEADS)).reshape(1, LANES)
    per_q = MLA_NOPE_DIM + MLA_ROPE_DIM
    wq = w_uq.reshape(MLA_Q_LORA, MLA_HEADS, per_q)
    q_rope = wq[:, :, MLA_NOPE_DIM:]
    in_row = lambda cols: jnp.pad(cols, ((0, 0), (0, 0), (ROPE_LANE, HEAD_ROW - ROPE_LANE - MLA_ROPE_DIM)))
    wuq = jnp.concatenate([wq[:, :, :MLA_NOPE_DIM].reshape(MLA_Q_LORA, -1), in_row(q_rope).reshape(MLA_Q_LORA, -1),
                           in_row(_rot_cols(q_rope)).reshape(MLA_Q_LORA, -1)], axis=1).astype(BF16)
    sel = np.zeros((LANES, 2 * FOX_HEADS * HEAD_ROW), np.float32)
    for hd in range(FOX_HEADS):
        q0, k0 = hd * HEAD_ROW + FOX_HEAD_DIM, (FOX_HEADS + hd) * HEAD_ROW + FOX_HEAD_DIM
        for k in range(3):
            sel[k * FOX_HEADS + hd, q0 + k] = 1.0
            sel[3 * FOX_HEADS, q0 + 3 + k] = 1.0
            sel[3 * FOX_HEADS, k0 + k] = 1.0
            sel[k * FOX_HEADS + hd, k0 + 3 + k] = -1.0
    wkv = w_ukv.reshape(MLA_KV_LORA, MLA_HEADS, MLA_NOPE_DIM + MLA_V_DIM)
    wukv = jnp.concatenate([wkv[:, :, :MLA_NOPE_DIM].reshape(MLA_KV_LORA, -1),
                            wkv[:, :, MLA_NOPE_DIM:].reshape(MLA_KV_LORA, -1)], axis=1).astype(BF16)
    return dict(
        gmix=g_mix.reshape(1, -1), win=win, bf=bf, sel=jnp.asarray(sel, BF16), gq=g_q.reshape(1, -1), wuq=wuq,
        gkv=g_kv.reshape(1, -1), wukv=wukv, wo=w_o.astype(BF16), gffn=g_ffn.reshape(1, -1), wpq=w_pq.astype(BF16),
        k1=sub_keys[0].astype(BF16), k2=sub_keys[1].astype(BF16),
        u=peer_u.astype(BF16), v=peer_v.astype(BF16), gfin=g_final.reshape(1, -1))


def _ffn(x, o, w):
    B, S, D = x.shape
    T = B * S
    tm = min(ROUTE_TOKENS, T)
    tt = min(PEER_TOKENS, T)
    if S % tm:
        o = jnp.transpose(o, (1, 0, 2)).reshape(1, o.shape[1], T)
    x1, h2, rank, e2, cnt, w1 = _route(x.reshape(T, D), o, w, tm, tt)
    return _peer(x1, h2, rank, e2, cnt, w1, w, tt, PEER_ROWS_PER_STEP).reshape(B, S, D)


def kernel(x_prompt, x_sample, cache_fox_k, cache_fox_v, cache_fox_logf, cache_mla_ckv, cache_mla_krope, g_mix, w_in, b_f, g_q, w_uq, g_kv, w_ukv, w_o, g_ffn, w_pq, peer_sub_keys, peer_u, peer_v, g_final):
    assert g_mix.shape[0] == 1, "single-layer model"
    w = _prep_weights(g_mix[0], w_in[0], b_f[0], g_q[0], w_uq[0], g_kv[0], w_ukv[0], w_o[0], g_ffn[0], w_pq[0],
                      peer_sub_keys[0], peer_u[0], peer_v[0], g_final)
    B, S, D = x_prompt.shape
    Bs, Ss, _ = x_sample.shape
    P = cache_fox_k.shape[2]

    tables_p = _rope_tables(jnp.arange(S, dtype=jnp.int32))
    tm = min(PROJ_ROWS, S)
    fk, fv, logf, ckv, kr, qh, kh, vh = _proj(x_prompt, jnp.zeros((B, 1, LANES), F32), tables_p, w, tm)
    tq = min(ATTN_BLOCK, S)
    o_p = _attention(qh, kh, vh, ATTN_HEAD_GROUP, tq, tq, 0, S)
    y_p = _ffn(x_prompt, o_p, w)

    tk = ATTN_BLOCK
    L = P + Ss
    kh_c, vh_c, c_last = _cache_rows(
        cache_fox_k[0].reshape(Bs, P, FOX_WIDTH), cache_fox_v[0].reshape(Bs, P, FOX_WIDTH),
        jnp.pad(cache_fox_logf[0], ((0, 0), (0, 0), (0, LANES - FOX_HEADS))),
        cache_mla_ckv[0], cache_mla_krope[0], w["wukv"], L + (-L) % tk)
    tables_s = _rope_tables(P + jnp.arange(Ss, dtype=jnp.int32))
    sfk, sfv, slogf, sckv, skr, sqh, skh, svh = _proj(x_sample, c_last, tables_s, w, Ss)
    kh_all = lax.dynamic_update_slice(kh_c, skh, (0, 0, P, 0))
    vh_all = lax.dynamic_update_slice(vh_c, svh, (0, 0, P, 0))
    o_s = _attention(sqh, kh_all, vh_all, ATTN_HEAD_GROUP, Ss, tk, P, L)
    y_s = _ffn(x_sample, o_s, w)

    return (y_p, y_s,
            fk.reshape(1, B, S, FOX_HEADS, FOX_HEAD_DIM), fv.reshape(1, B, S, FOX_HEADS, FOX_HEAD_DIM),
            logf[None], ckv[None], kr[None],
            sfk.reshape(1, Bs, Ss, FOX_HEADS, FOX_HEAD_DIM), sfv.reshape(1, Bs, Ss, FOX_HEADS, FOX_HEAD_DIM),
            slogf[None], sckv[None], skr[None])
```

```python
import functools
import math

import numpy as np
import jax
import jax.numpy as jnp
from jax import lax
from jax.experimental import pallas as pl
from jax.experimental.pallas import tpu as pltpu

F32 = jnp.float32
BF16 = jnp.bfloat16
LANES = 128

EPS = 1e-6
CHUNK = 64
ROPE_BASE = 10000.0

FOX_HEADS = 8
FOX_HEAD_DIM = 64
FOX_WIDTH = FOX_HEADS * FOX_HEAD_DIM
MLA_HEADS = 8
MLA_NOPE_DIM = 64
MLA_ROPE_DIM = 32
MLA_V_DIM = 64
MLA_Q_LORA = 256
MLA_KV_LORA = 128
ATTN_HEADS = FOX_HEADS + MLA_HEADS
HEAD_ROW = 128
V_DIM = 64

PEER_HEADS = 8
PEER_N_KEYS = 128
PEER_HALF = 128
PEER_TOPK = 16
N_RANK = PEER_TOPK + 1
RANK_ROWS = 24
ROUTE_HEAD_GROUP = 4

Z_FQ = 0
Z_FK = Z_FQ + FOX_WIDTH
Z_FV = Z_FK + FOX_WIDTH
Z_CQ = Z_FV + FOX_WIDTH
Z_CKV = Z_CQ + MLA_Q_LORA
Z_TAIL = Z_CKV + MLA_KV_LORA
Z_TAILB = Z_TAIL + LANES
Z_WIDTH = Z_TAILB + LANES
ROPE_LANE = 64

PROJ_ROWS = 512
ATTN_BLOCK = 256
ROUTE_TOKENS = 256
PEER_TOKENS = 512
PEER_ROWS_PER_STEP = 32

NEG = -1e30
LOG2E = math.log2(math.e)
ATTN_HEAD_GROUP = 8
V7X_VMEM_LIMIT_BYTES = 56 * 1024 * 1024


def _params(sem):
    return pltpu.CompilerParams(dimension_semantics=sem, vmem_limit_bytes=V7X_VMEM_LIMIT_BYTES)


def _rms(x, g):
    return x * lax.rsqrt(jnp.mean(x * x, axis=-1, keepdims=True) + EPS) * g


def _log_sigmoid(x):
    return jnp.minimum(x, 0.0) - jnp.log1p(jnp.exp(-jnp.abs(x)))


def _split3(x):
    hi = x.astype(BF16)
    r1 = x - hi.astype(F32)
    mid = r1.astype(BF16)
    lo = (r1 - mid.astype(F32)).astype(BF16)
    return hi, mid, lo


def _cumsum_rows(tri, x, carry):
    acc = carry
    for piece in _split3(x):
        acc = acc + jnp.dot(tri, piece, preferred_element_type=F32)
    return acc


def _fox_aux(c_col):
    hi, mid, lo = (p.astype(F32) for p in _split3(c_col * LOG2E))
    lane = lax.broadcasted_iota(jnp.int32, (1, HEAD_ROW - FOX_HEAD_DIM), 1)
    q_aux = jnp.where(lane == 0, hi, jnp.where(lane == 1, mid, jnp.where(
        lane == 2, lo, jnp.where(lane < 6, 1.0, 0.0))))
    k_aux = jnp.where(lane < 3, 1.0, jnp.where(lane == 3, -hi, jnp.where(
        lane == 4, -mid, jnp.where(lane == 5, -lo, 0.0))))
    return q_aux, k_aux


def _write_fox_kv(kh_ref, vh_ref, k, v, c_slab):
    rows = slice(0, k.shape[0])
    for h in range(FOX_HEADS):
        sl = slice(h * FOX_HEAD_DIM, (h + 1) * FOX_HEAD_DIM)
        _, k_aux = _fox_aux(c_slab[:, h:h + 1])
        kh_ref[0, h, rows, 0:FOX_HEAD_DIM] = k[:, sl].astype(BF16)
        kh_ref[0, h, rows, FOX_HEAD_DIM:HEAD_ROW] = k_aux.astype(BF16)
        vh_ref[0, h, rows, :] = v[:, sl].astype(BF16)


def _write_mla_kv(kh_ref, vh_ref, kv, krope):
    rows = slice(0, kv.shape[0])
    k_rope = krope.astype(BF16)
    pad = jnp.zeros((kv.shape[0], HEAD_ROW - MLA_NOPE_DIM - MLA_ROPE_DIM), BF16)
    for h in range(MLA_HEADS):
        sl = slice(h * MLA_NOPE_DIM, (h + 1) * MLA_NOPE_DIM)
        kh_ref[0, FOX_HEADS + h, rows, 0:MLA_NOPE_DIM] = kv[:, sl].astype(BF16)
        kh_ref[0, FOX_HEADS + h, rows, MLA_NOPE_DIM:MLA_NOPE_DIM + MLA_ROPE_DIM] = k_rope
        kh_ref[0, FOX_HEADS + h, rows, MLA_NOPE_DIM + MLA_ROPE_DIM:HEAD_ROW] = pad
        vsl = slice(MLA_HEADS * MLA_NOPE_DIM + h * MLA_V_DIM, MLA_HEADS * MLA_NOPE_DIM + (h + 1) * MLA_V_DIM)
        vh_ref[0, FOX_HEADS + h, rows, :] = kv[:, vsl].astype(BF16)


def _proj_body(x_ref, c0_ref, tri_ref, cos_ref, sin_ref, gmix_ref, win_ref, bf_ref, gq_ref, wuq_ref, gkv_ref,
               wukv_ref, sel_ref,
               fk_ref, fv_ref, logf_ref, ckv_ref, kr_ref, qh_ref, kh_ref, vh_ref, carry_ref):
    @pl.when(pl.program_id(1) == 0)
    def _():
        carry_ref[...] = c0_ref[0]

    rows = x_ref.shape[1]
    h = _rms(x_ref[0], gmix_ref[...]).astype(BF16)
    z = jnp.dot(h, win_ref[...], preferred_element_type=F32)

    fq = z[:, Z_FQ:Z_FK]
    fk = z[:, Z_FK:Z_FV]
    fv = z[:, Z_FV:Z_CQ]
    fk_ref[0] = fk
    fv_ref[0] = fv

    tail_a = z[:, Z_TAIL:Z_TAILB]
    tail_b = z[:, Z_TAILB:Z_WIDTH]
    logf = _log_sigmoid(tail_a + bf_ref[...])
    logf_ref[0] = logf[:, 0:FOX_HEADS]
    c_slab = _cumsum_rows(tri_ref[...], logf, carry_ref[...])
    carry_ref[...] = c_slab[rows - 1:rows, :]

    lane = lax.broadcasted_iota(jnp.int32, (1, LANES), 1)
    hi, mid, lo = (p.astype(F32) for p in _split3(c_slab * LOG2E))
    pieces = jnp.where(lane < FOX_HEADS, hi, jnp.where(lane < 2 * FOX_HEADS, pltpu.roll(mid, FOX_HEADS, 1), jnp.where(
        lane < 3 * FOX_HEADS, pltpu.roll(lo, 2 * FOX_HEADS, 1), jnp.where(lane == 3 * FOX_HEADS, 1.0, 0.0))))
    aux = jnp.dot(pieces.astype(BF16), sel_ref[...], preferred_element_type=F32)

    cos = cos_ref[...]
    sin = sin_ref[...]
    krope = tail_a * cos + tail_b * sin
    kr_ref[0] = krope[:, ROPE_LANE:ROPE_LANE + MLA_ROPE_DIM]

    cqn = _rms(z[:, Z_CQ:Z_CKV], gq_ref[...]).astype(BF16)
    q = jnp.dot(cqn, wuq_ref[...], preferred_element_type=F32)
    ckv = _rms(z[:, Z_CKV:Z_TAIL], gkv_ref[...])
    ckv_ref[0] = ckv
    kv = jnp.dot(ckv.astype(BF16), wukv_ref[...], preferred_element_type=F32)

    fox_scale = FOX_HEAD_DIM ** -0.5 * LOG2E
    mla_scale = (MLA_NOPE_DIM + MLA_ROPE_DIM) ** -0.5 * LOG2E
    low = lane < FOX_HEAD_DIM
    nope_w = MLA_HEADS * MLA_NOPE_DIM
    row_w = MLA_HEADS * HEAD_ROW

    def head_src(block, hd):
        slab = block[:, (hd // 2) * LANES:(hd // 2 + 1) * LANES]
        return slab if hd % 2 == 0 else pltpu.roll(slab, 64, 1)

    for hd in range(FOX_HEADS):
        rs = slice(hd * HEAD_ROW, (hd + 1) * HEAD_ROW)
        ks = slice(row_w + hd * HEAD_ROW, row_w + (hd + 1) * HEAD_ROW)
        qh_ref[0, hd] = jnp.where(low, head_src(fq, hd) * fox_scale, aux[:, rs]).astype(BF16)
        kh_ref[0, hd] = jnp.where(low, head_src(fk, hd), aux[:, ks]).astype(BF16)
        vh_ref[0, hd] = fv[:, hd * V_DIM:(hd + 1) * V_DIM].astype(BF16)
    for hd in range(MLA_HEADS):
        rope = (q[:, nope_w + hd * HEAD_ROW:nope_w + (hd + 1) * HEAD_ROW] * cos
                + q[:, nope_w + row_w + hd * HEAD_ROW:nope_w + row_w + (hd + 1) * HEAD_ROW] * sin)
        qh_ref[0, FOX_HEADS + hd] = (jnp.where(low, head_src(q, hd), rope) * mla_scale).astype(BF16)
        kh_ref[0, FOX_HEADS + hd] = jnp.where(low, head_src(kv, hd), krope).astype(BF16)
        vh_ref[0, FOX_HEADS + hd] = kv[:, nope_w + hd * V_DIM:nope_w + (hd + 1) * V_DIM].astype(BF16)


def _proj(x, c0, tables, w, tm):
    B, S, D = x.shape
    cos, sin = tables
    tri = jnp.tril(jnp.ones((tm, tm), F32)).astype(BF16)
    const = lambda shape: pl.BlockSpec(shape, lambda b, t: (0,) * len(shape))
    tab = pl.BlockSpec((tm, LANES), lambda b, t: (t, 0))
    in_specs = [
        pl.BlockSpec((1, tm, D), lambda b, t: (b, t, 0)),
        pl.BlockSpec((1, 1, LANES), lambda b, t: (b, 0, 0)),
        const((tm, tm)), tab, tab,
        const((1, D)), const((D, Z_WIDTH)), const((1, LANES)),
        const((1, MLA_Q_LORA)), const(w["wuq"].shape), const((1, MLA_KV_LORA)), const(w["wukv"].shape),
        const(w["sel"].shape),
    ]
    tok = lambda width: pl.BlockSpec((1, tm, width), lambda b, t: (b, t, 0))
    head = lambda width: pl.BlockSpec((1, ATTN_HEADS, tm, width), lambda b, t: (b, 0, t, 0))
    out_specs = [tok(FOX_WIDTH), tok(FOX_WIDTH), tok(FOX_HEADS), tok(MLA_KV_LORA), tok(MLA_ROPE_DIM),
                 head(HEAD_ROW), head(HEAD_ROW), head(V_DIM)]
    sds = jax.ShapeDtypeStruct
    out_shape = [sds((B, S, FOX_WIDTH), F32), sds((B, S, FOX_WIDTH), F32), sds((B, S, FOX_HEADS), F32),
                 sds((B, S, MLA_KV_LORA), F32), sds((B, S, MLA_ROPE_DIM), F32),
                 sds((B, ATTN_HEADS, S, HEAD_ROW), BF16), sds((B, ATTN_HEADS, S, HEAD_ROW), BF16),
                 sds((B, ATTN_HEADS, S, V_DIM), BF16)]
    return pl.pallas_call(
        _proj_body, grid=(B, S // tm), in_specs=in_specs, out_specs=out_specs, out_shape=out_shape,
        scratch_shapes=[pltpu.VMEM((1, LANES), F32)],
        compiler_params=_params(("arbitrary", "arbitrary")), name="proj",
    )(x, c0, tri, cos, sin, w["gmix"], w["win"], w["bf"], w["gq"], w["wuq"], w["gkv"], w["wukv"], w["sel"])


def _cache_body(ck_ref, cv_ref, clogf_ref, cckv_ref, ckr_ref, tri_ref, wukv_ref, kh_ref, vh_ref, clast_ref):
    rows = ck_ref.shape[1]
    c_slab = _cumsum_rows(tri_ref[...], clogf_ref[0], jnp.zeros((1, LANES), F32))
    clast_ref[0] = c_slab[rows - 1:rows, :]
    _write_fox_kv(kh_ref, vh_ref, ck_ref[0], cv_ref[0], c_slab)
    kv = jnp.dot(cckv_ref[0].astype(BF16), wukv_ref[...], preferred_element_type=F32)
    _write_mla_kv(kh_ref, vh_ref, kv, ckr_ref[0])
    kh_ref[0, :, rows:, :] = jnp.zeros((ATTN_HEADS, kh_ref.shape[2] - rows, HEAD_ROW), BF16)
    vh_ref[0, :, rows:, :] = jnp.zeros((ATTN_HEADS, vh_ref.shape[2] - rows, V_DIM), BF16)


def _cache_rows(ck, cv, clogf_slab, cckv, ckr, wukv, total_rows):
    B, P, _ = ck.shape
    tri = jnp.tril(jnp.ones((P, P), F32)).astype(BF16)
    per_b = lambda width: pl.BlockSpec((1, P, width), lambda b: (b, 0, 0))
    const = lambda shape: pl.BlockSpec(shape, lambda b: (0,) * len(shape))
    head = lambda width: pl.BlockSpec((1, ATTN_HEADS, total_rows, width), lambda b: (b, 0, 0, 0))
    sds = jax.ShapeDtypeStruct
    return pl.pallas_call(
        _cache_body, grid=(B,),
        in_specs=[per_b(FOX_WIDTH), per_b(FOX_WIDTH), per_b(LANES), per_b(MLA_KV_LORA), per_b(MLA_ROPE_DIM),
                  const((P, P)), const(wukv.shape)],
        out_specs=[head(HEAD_ROW), head(V_DIM), pl.BlockSpec((1, 1, LANES), lambda b: (b, 0, 0))],
        out_shape=[sds((B, ATTN_HEADS, total_rows, HEAD_ROW), BF16), sds((B, ATTN_HEADS, total_rows, V_DIM), BF16),
                   sds((B, 1, LANES), F32)],
        compiler_params=_params(("arbitrary",)), name="cache_rows",
    )(ck, cv, clogf_slab, cckv, ckr, tri, wukv)


def _reduce_rows(pair_op, final_op, x):
    while x.shape[0] > 8 and x.shape[0] % 16 == 0:
        half = x.shape[0] // 2
        x = pair_op(x[:half], x[half:])
    return final_op(x, axis=0, keepdims=True)


def _attn_body(q_ref, k_ref, v_ref, o_ref, *, hg, tq, tk, nkv_total, q_off, kv_len):
    grp = pl.program_id(1)
    qi = pl.program_id(2)
    q_lo = q_off + qi * tq
    qpos = q_lo + lax.broadcasted_iota(jnp.int32, (1, tq), 1)
    lim = jnp.where(grp >= FOX_HEADS // hg, qpos | (CHUNK - 1), qpos)
    lim = jnp.minimum(lim, kv_len - 1)
    n_full = jnp.minimum(lax.div(q_lo + 1, tk), nkv_total)
    n_all = jnp.minimum(lax.div((q_lo + tq - 1) | (CHUNK - 1), tk) + 1, nkv_total)

    def step(j, carry, masked):
        off = pl.multiple_of(j * tk, tk)
        if masked:
            visible = off + lax.broadcasted_iota(jnp.int32, (tk, 1), 0) <= lim
        out = []
        scores = [lax.dot_general(k_ref[0, hh, pl.ds(off, tk), :], q_ref[0, hh], (((1,), (1,)), ((), ())),
                                  preferred_element_type=F32) for hh in range(hg)]
        for hh in range(hg):
            m, l, acc = carry[3 * hh:3 * hh + 3]
            vb = v_ref[0, hh, pl.ds(off, tk), :]
            s = scores[hh]
            if masked:
                s = jnp.where(visible, s, NEG)
            m_new = jnp.maximum(m, _reduce_rows(jnp.maximum, jnp.max, s))
            alpha = jnp.exp2(m - m_new)
            p = jnp.exp2(s - m_new)
            l = alpha * l + _reduce_rows(jnp.add, jnp.sum, p)
            pv = lax.dot_general(vb, p.astype(BF16), (((0,), (0,)), ((), ())), preferred_element_type=F32)
            out += [m_new, l, alpha * acc + pv]
        return tuple(out)

    init = (jnp.full((1, tq), NEG, F32), jnp.zeros((1, tq), F32), jnp.zeros((V_DIM, tq), F32)) * hg
    carry = lax.fori_loop(0, n_full, functools.partial(step, masked=False), init)
    carry = lax.fori_loop(n_full, n_all, functools.partial(step, masked=True), carry)
    o_ref[0] = jnp.concatenate([carry[3 * hh + 2] / carry[3 * hh + 1] for hh in range(hg)], axis=0).astype(BF16)


def _attention(qh, kh, vh, hg, tq, tk, q_off, kv_len):
    B, _, Tq, _ = qh.shape
    Tk = kh.shape[2]
    body = functools.partial(_attn_body, hg=hg, tq=tq, tk=tk, nkv_total=Tk // tk, q_off=q_off, kv_len=kv_len)
    return pl.pallas_call(
        body, grid=(B, ATTN_HEADS // hg, Tq // tq),
        in_specs=[pl.BlockSpec((1, hg, tq, HEAD_ROW), lambda b, g, i: (b, g, i, 0)),
                  pl.BlockSpec((1, hg, Tk, HEAD_ROW), lambda b, g, i: (b, g, 0, 0)),
                  pl.BlockSpec((1, hg, Tk, V_DIM), lambda b, g, i: (b, g, 0, 0))],
        out_specs=pl.BlockSpec((1, hg * V_DIM, tq), lambda b, g, i: (b, g, i)),
        out_shape=jax.ShapeDtypeStruct((B, ATTN_HEADS * V_DIM, Tq), BF16),
        compiler_params=_params(("arbitrary", "arbitrary", "arbitrary")), name="attn",
    )(qh, kh, vh)


def _sorting_network(n):
    size = 1 << (n - 1).bit_length()

    def merge(lo, hi, r):
        step = r * 2
        if step < hi - lo:
            yield from merge(lo, hi, step)
            yield from merge(lo + r, hi, step)
            yield from ((i, i + r) for i in range(lo + r, hi - r, step))
        else:
            yield (lo, lo + r)

    def sort(lo, hi):
        if hi - lo >= 1:
            mid = lo + (hi - lo) // 2
            yield from sort(lo, mid)
            yield from sort(mid + 1, hi)
            yield from merge(lo, hi, 1)

    return tuple((i, j) for i, j in sort(0, size - 1) if j < n)


def _largest(groups, n_out):
    g = list(groups)
    for i, j in _sorting_network(len(g)):
        g[i], g[j] = jnp.maximum(g[i], g[j]), jnp.minimum(g[i], g[j])
    below = jnp.full(g[0].shape, -jnp.inf, F32)
    out = []
    for r in range(n_out):
        m = jnp.max(g[0], axis=0, keepdims=True)
        out.append(m)
        hit = g[0] == m
        for k in range(min(n_out - 1 - r, len(g))):
            g[k] = jnp.where(hit, g[k + 1] if k + 1 < len(g) else below, g[k])
    return out


def _top_ranks(work, rows_ref, want_rank):
    rows_ref[...] = jnp.full(rows_ref.shape, -jnp.inf, F32)
    for r, m in enumerate(_largest([work[8 * k:8 * (k + 1), :] for k in range(work.shape[0] // 8)], N_RANK)):
        rows_ref[r:r + 1, :] = m
    if not want_rank:
        return None
    rank = jnp.where(rows_ref[0:1, :] > work, 1.0, 0.0)
    for r in range(1, N_RANK):
        rank = rank + jnp.where(rows_ref[r:r + 1, :] > work, 1.0, 0.0)
    return rank


def _route_body(x_ref, o_ref, wo_ref, gffn_ref, wpq_ref, k1_ref, k2_ref,
                x1_ref, h2_ref, rank_ref, e2_ref, cnt_ref, w1_ref, qp_ref, *rank_rows):
    x1 = x_ref[...] + lax.dot_general(o_ref[0], wo_ref[...], (((0,), (0,)), ((), ())), preferred_element_type=F32)
    x1_ref[...] = x1
    h2 = _rms(x1, gffn_ref[...]).astype(BF16)
    h2_ref[...] = h2
    qp_ref[...] = jnp.dot(h2, wpq_ref[...], preferred_element_type=F32)
    key_dim = 2 * PEER_HALF

    def head(h, a_ref, b_ref):
        qh = qp_ref[:, pl.ds(pl.multiple_of(h * key_dim, key_dim), key_dim)]
        qn = qh * lax.rsqrt(jnp.mean(qh * qh, axis=-1, keepdims=True) + EPS)
        nt = (((1,), (1,)), ((), ()))
        s1 = lax.dot_general(k1_ref[h], qn[:, :PEER_HALF].astype(BF16), nt, preferred_element_type=F32)
        s2 = lax.dot_general(k2_ref[h], qn[:, PEER_HALF:].astype(BF16), nt, preferred_element_type=F32)
        _top_ranks(s1, a_ref, False)
        rank = _top_ranks(s2, b_ref, True)
        groups = [a_ref[0:1, :] + b_ref[8 * k:8 * (k + 1), :] for k in range(RANK_ROWS // 8)]
        groups += [a_ref[r:r + 1, :] + b_ref[0:8, :] for r in range(1, 8)]
        groups += [a_ref[8 * k:8 * (k + 1), :] + b_ref[0:1, :] for k in range(1, RANK_ROWS // 8)]
        pair_scores = _largest(groups, N_RANK)
        top = pair_scores[0]
        zsum = jnp.zeros_like(top)
        for v in pair_scores[:PEER_TOPK]:
            zsum = zsum + jnp.exp(v - top)
        tau = 0.5 * (pair_scores[PEER_TOPK - 1] + pair_scores[PEER_TOPK])
        thr = tau - s1
        cnt = jnp.where(b_ref[0:1, :] >= thr, 1.0, 0.0)
        for c in range(1, N_RANK):
            cnt = cnt + jnp.where(b_ref[c:c + 1, :] >= thr, 1.0, 0.0)
        rank_ref[0, h] = rank.astype(BF16)
        e2_ref[0, h] = jnp.exp(s2 - b_ref[0:1, :]).astype(BF16)
        cnt_ref[0, h] = cnt
        w1_ref[0, h] = (0.5 * jnp.exp(s1 - a_ref[0:1, :])) / zsum

    group = len(rank_rows) // 2

    def head_group(p, carry):
        for k in range(group):
            head(group * p + k, rank_rows[2 * k], rank_rows[2 * k + 1])
        return carry

    lax.fori_loop(0, PEER_HEADS // group, head_group, 0)


def _route(x2d, o_t, w, tm, tt):
    T, D = x2d.shape
    per_seq = o_t.shape[2] // tm
    r = tt // tm
    kq = w["wpq"].shape[1]
    const = lambda shape: pl.BlockSpec(shape, lambda i: (0,) * len(shape))
    lane_out = pl.BlockSpec((1, PEER_HEADS, PEER_N_KEYS, tm), lambda i: (i // r, 0, 0, i % r))
    sds = jax.ShapeDtypeStruct
    lane_shape = lambda dtype: sds((T // tt, PEER_HEADS, PEER_N_KEYS, tt), dtype)
    return pl.pallas_call(
        _route_body, grid=(T // tm,),
        in_specs=[pl.BlockSpec((tm, D), lambda i: (i, 0)),
                  pl.BlockSpec((1, o_t.shape[1], tm), lambda i: (i // per_seq, 0, i % per_seq)),
                  const((D, D)), const((1, D)), const((D, kq)),
                  const((PEER_HEADS, PEER_N_KEYS, PEER_HALF)), const((PEER_HEADS, PEER_N_KEYS, PEER_HALF))],
        out_specs=[pl.BlockSpec((tm, D), lambda i: (i, 0)), pl.BlockSpec((tm, D), lambda i: (i, 0)),
                   lane_out, lane_out, lane_out, lane_out],
        out_shape=[sds((T, D), F32), sds((T, D), BF16),
                   lane_shape(BF16), lane_shape(BF16), lane_shape(F32), lane_shape(F32)],
        scratch_shapes=[pltpu.VMEM((tm, kq), F32)] + [pltpu.VMEM((RANK_ROWS, tm), F32)] * (2 * ROUTE_HEAD_GROUP),
        compiler_params=_params(("arbitrary",)), name="route",
    )(x2d, o_t, w["wo"], w["gffn"], w["wpq"], w["k1"], w["k2"])


GATE_ROWS = 64
GATE_LANES = 256
CHUNK_I = 2


def _row_tile(ref, h, i, ls):
    tile = jnp.broadcast_to(ref[0, h, i:i + 1, ls], (8, GATE_LANES)).astype(BF16)
    return jnp.concatenate([tile] * (GATE_ROWS // 8), axis=0)


def _scores_body(u_ref, h2_ref, a_ref):
    a_ref[0] = lax.dot_general(u_ref[...], h2_ref[...], (((1,), (1,)), ((), ())),
                               preferred_element_type=F32).astype(BF16)


def _scores(h2, u, tt, eb):
    T, D = h2.shape
    n_exp = u.shape[0]
    return pl.pallas_call(
        _scores_body, grid=(T // tt, n_exp // eb),
        in_specs=[pl.BlockSpec((eb, D), lambda t, j: (j, 0)), pl.BlockSpec((tt, D), lambda t, j: (t, 0))],
        out_specs=pl.BlockSpec((1, eb, tt), lambda t, j: (t, j, 0)),
        out_shape=jax.ShapeDtypeStruct((T // tt, n_exp, tt), BF16),
        compiler_params=_params(("arbitrary", "arbitrary")), name="scores",
    )(u, h2)


def _peer_body(x1_ref, rank_ref, e2_ref, cnt_ref, w1_ref, a_ref, v_ref, gfin_ref, y_ref,
               acc_ref, *act_refs, tt):
    j = pl.program_id(1)

    @pl.when(j == 0)
    def _():
        acc_ref[...] = jnp.zeros_like(acc_ref)

    n_chunks = len(act_refs)
    inv_sqrt2 = 1.0 / math.sqrt(2.0)
    rows = CHUNK_I * PEER_N_KEYS

    for c in range(n_chunks):
        act_ref = act_refs[c]
        a = a_ref.at[0, c * rows:(c + 1) * rows]
        for k in range(CHUNK_I):
            i = c * CHUNK_I + k
            for tc in range(tt // GATE_LANES):
                ls = slice(tc * GATE_LANES, (tc + 1) * GATE_LANES)
                cnt = [_row_tile(cnt_ref, h, i, ls) for h in range(PEER_HEADS)]
                w1 = [_row_tile(w1_ref, h, i, ls) for h in range(PEER_HEADS)]
                for jb in range(PEER_N_KEYS // GATE_ROWS):
                    js = slice(jb * GATE_ROWS, (jb + 1) * GATE_ROWS)
                    gate = None
                    for h in range(PEER_HEADS):
                        sel = rank_ref[0, h, js, ls] < cnt[h]
                        term = jnp.where(sel, e2_ref[0, h, js, ls] * w1[h], 0.0)
                        gate = term if gate is None else gate + term
                    r0 = k * PEER_N_KEYS + jb * GATE_ROWS
                    av = a[r0:r0 + GATE_ROWS, ls]
                    gelu2 = av * (1.0 + lax.erf(av * inv_sqrt2))
                    act_ref[r0:r0 + GATE_ROWS, ls] = gelu2 * gate
        acc_ref[...] += lax.dot_general(act_ref[...], v_ref[c * rows:(c + 1) * rows, :], (((0,), (0,)), ((), ())),
                                        preferred_element_type=F32)

    @pl.when(j == pl.num_programs(1) - 1)
    def _():
        y_ref[...] = _rms(x1_ref[...] + acc_ref[...], gfin_ref[...])


def _peer(x1, h2, rank, e2, cnt, w1, w, tt, rows_per_step):
    T, D = x1.shape
    n_exp = w["v"].shape[0]
    eb = rows_per_step * PEER_N_KEYS
    lane_in = pl.BlockSpec((1, PEER_HEADS, PEER_N_KEYS, tt), lambda t, j: (t, 0, 0, 0))
    row_in = pl.BlockSpec((1, PEER_HEADS, rows_per_step, tt), lambda t, j: (t, 0, j, 0))
    body = functools.partial(_peer_body, tt=tt)
    n_chunks = rows_per_step // CHUNK_I
    act_chunks = [pltpu.VMEM((CHUNK_I * PEER_N_KEYS, tt), BF16)] * n_chunks
    a = _scores(h2, w["u"], tt, eb)
    return pl.pallas_call(
        body, grid=(T // tt, n_exp // eb),
        in_specs=[pl.BlockSpec((tt, D), lambda t, j: (t, 0)),
                  lane_in, lane_in, row_in, row_in,
                  pl.BlockSpec((1, eb, tt), lambda t, j: (t, j, 0)), pl.BlockSpec((eb, D), lambda t, j: (j, 0)),
                  pl.BlockSpec((1, D), lambda t, j: (0, 0))],
        out_specs=pl.BlockSpec((tt, D), lambda t, j: (t, 0)),
        out_shape=jax.ShapeDtypeStruct((T, D), F32),
        scratch_shapes=[pltpu.VMEM((tt, D), F32)] + act_chunks,
        compiler_params=_params(("arbitrary", "arbitrary")), name="peer",
    )(x1, rank, e2, cnt, w1, a, w["v"], w["gfin"])


def _rope_tables(pos):
    half = MLA_ROPE_DIM // 2
    inv = ROPE_BASE ** (-jnp.arange(half, dtype=F32) / half)
    ang = pos.astype(F32)[:, None] * inv[None, :]
    pad = ((0, 0), (ROPE_LANE, HEAD_ROW - ROPE_LANE - MLA_ROPE_DIM))
    return (jnp.pad(jnp.concatenate([jnp.cos(ang)] * 2, axis=-1), pad),
            jnp.pad(jnp.concatenate([jnp.sin(ang)] * 2, axis=-1), pad))


def _rot_cols(wcols):
    half = wcols.shape[-1] // 2
    return jnp.concatenate([-wcols[..., half:], wcols[..., :half]], axis=-1)


def _prep_weights(g_mix, w_in, b_f, g_q, w_uq, g_kv, w_ukv, w_o, g_ffn, w_pq, sub_keys, peer_u, peer_v, g_final):
    D = w_in.shape[0]
    off_ff = 3 * FOX_WIDTH
    off_cq = off_ff + FOX_HEADS
    off_ckv = off_cq + MLA_Q_LORA
    off_kr = off_ckv + MLA_KV_LORA
    kr_cols = w_in[:, off_kr:off_kr + MLA_ROPE_DIM]
    zeros = lambda n: jnp.zeros((D, n), F32)
    rope_tail = zeros(HEAD_ROW - ROPE_LANE - MLA_ROPE_DIM)
    tail_a = [w_in[:, off_ff:off_cq], zeros(ROPE_LANE - FOX_HEADS), kr_cols, rope_tail]
    tail_b = [zeros(ROPE_LANE), _rot_cols(kr_cols), rope_tail]
    win = jnp.concatenate([w_in[:, :off_ff], w_in[:, off_cq:off_kr]] + tail_a + tail_b, axis=1).astype(BF16)
    bf = jnp.pad(b_f, (0, LANES - FOX_HEADS)).reshape(1, LANES)
    per_q = MLA_NOPE_DIM + MLA_ROPE_DIM
    wq = w_uq.reshape(MLA_Q_LORA, MLA_HEADS, per_q)
    q_rope = wq[:, :, MLA_NOPE_DIM:]
    in_row = lambda cols: jnp.pad(cols, ((0, 0), (0, 0), (ROPE_LANE, HEAD_ROW - ROPE_LANE - MLA_ROPE_DIM)))
    wuq = jnp.concatenate([wq[:, :, :MLA_NOPE_DIM].reshape(MLA_Q_LORA, -1), in_row(q_rope).reshape(MLA_Q_LORA, -1),
                           in_row(_rot_cols(q_rope)).reshape(MLA_Q_LORA, -1)], axis=1).astype(BF16)
    sel = np.zeros((LANES, 2 * FOX_HEADS * HEAD_ROW), np.float32)
    for hd in range(FOX_HEADS):
        q0, k0 = hd * HEAD_ROW + FOX_HEAD_DIM, (FOX_HEADS + hd) * HEAD_ROW + FOX_HEAD_DIM
        for k in range(3):
            sel[k * FOX_HEADS + hd, q0 + k] = 1.0
            sel[3 * FOX_HEADS, q0 + 3 + k] = 1.0
            sel[3 * FOX_HEADS, k0 + k] = 1.0
            sel[k * FOX_HEADS + hd, k0 + 3 + k] = -1.0
    wkv = w_ukv.reshape(MLA_KV_LORA, MLA_HEADS, MLA_NOPE_DIM + MLA_V_DIM)
    wukv = jnp.concatenate([wkv[:, :, :MLA_NOPE_DIM].reshape(MLA_KV_LORA, -1),
                            wkv[:, :, MLA_NOPE_DIM:].reshape(MLA_KV_LORA, -1)], axis=1).astype(BF16)
    return dict(
        gmix=g_mix.reshape(1, -1), win=win, bf=bf, sel=jnp.asarray(sel, BF16), gq=g_q.reshape(1, -1), wuq=wuq,
        gkv=g_kv.reshape(1, -1), wukv=wukv, wo=w_o.astype(BF16), gffn=g_ffn.reshape(1, -1), wpq=w_pq.astype(BF16),
        k1=sub_keys[0].astype(BF16), k2=sub_keys[1].astype(BF16),
        u=peer_u.astype(BF16), v=peer_v.astype(BF16), gfin=g_final.reshape(1, -1))


def _ffn(x, o, w):
    B, S, D = x.shape
    T = B * S
    tm = min(ROUTE_TOKENS, T)
    tt = min(PEER_TOKENS, T)
    if S % tm:
        o = jnp.transpose(o, (1, 0, 2)).reshape(1, o.shape[1], T)
    x1, h2, rank, e2, cnt, w1 = _route(x.reshape(T, D), o, w, tm, tt)
    return _peer(x1, h2, rank, e2, cnt, w1, w, tt, PEER_ROWS_PER_STEP).reshape(B, S, D)


def kernel(x_prompt, x_sample, cache_fox_k, cache_fox_v, cache_fox_logf, cache_mla_ckv, cache_mla_krope, g_mix, w_in, b_f, g_q, w_uq, g_kv, w_ukv, w_o, g_ffn, w_pq, peer_sub_keys, peer_u, peer_v, g_final):
    assert g_mix.shape[0] == 1, "single-layer model"
    w = _prep_weights(g_mix[0], w_in[0], b_f[0], g_q[0], w_uq[0], g_kv[0], w_ukv[0], w_o[0], g_ffn[0], w_pq[0],
                      peer_sub_keys[0], peer_u[0], peer_v[0], g_final)
    B, S, D = x_prompt.shape
    Bs, Ss, _ = x_sample.shape
    P = cache_fox_k.shape[2]

    tables_p = _rope_tables(jnp.arange(S, dtype=jnp.int32))
    tm = min(PROJ_ROWS, S)
    fk, fv, logf, ckv, kr, qh, kh, vh = _proj(x_prompt, jnp.zeros((B, 1, LANES), F32), tables_p, w, tm)
    tq = min(ATTN_BLOCK, S)
    o_p = _attention(qh, kh, vh, ATTN_HEAD_GROUP, tq, tq, 0, S)
    y_p = _ffn(x_prompt, o_p, w)

    tk = ATTN_BLOCK
    L = P + Ss
    kh_c, vh_c, c_last = _cache_rows(
        cache_fox_k[0].reshape(Bs, P, FOX_WIDTH), cache_fox_v[0].reshape(Bs, P, FOX_WIDTH),
        jnp.pad(cache_fox_logf[0], ((0, 0), (0, 0), (0, LANES - FOX_HEADS))),
        cache_mla_ckv[0], cache_mla_krope[0], w["wukv"], L + (-L) % tk)
    tables_s = _rope_tables(P + jnp.arange(Ss, dtype=jnp.int32))
    sfk, sfv, slogf, sckv, skr, sqh, skh, svh = _proj(x_sample, c_last, tables_s, w, Ss)
    kh_all = lax.dynamic_update_slice(kh_c, skh, (0, 0, P, 0))
    vh_all = lax.dynamic_update_slice(vh_c, svh, (0, 0, P, 0))
    o_s = _attention(sqh, kh_all, vh_all, ATTN_HEAD_GROUP, Ss, tk, P, L)
    y_s = _ffn(x_sample, o_s, w)

    return (y_p, y_s,
            fk.reshape(1, B, S, FOX_HEADS, FOX_HEAD_DIM), fv.reshape(1, B, S, FOX_HEADS, FOX_HEAD_DIM),
            logf[None], ckv[None], kr[None],
            sfk.reshape(1, Bs, Ss, FOX_HEADS, FOX_HEAD_DIM), sfv.reshape(1, Bs, Ss, FOX_HEADS, FOX_HEAD_DIM),
            slogf[None], sckv[None], skr[None])
```
